```python
import jax
import jax.numpy as jnp
from jax import lax
import numpy as np

D_MODEL = 1024
BATCH = 16
SEQ = 2048
DEPTH = 2
DEC_BATCH = 2
DEC_SEQ = 8192
PAST_LEN = 128

EPS = 1e-6
A_HEAD_DIM = 128
A_WIDTH = D_MODEL // 2
A_HEADS = A_WIDTH // A_HEAD_DIM
HGRN_CHUNK = 64
B_WIDTH = D_MODEL - A_WIDTH
B_CHUNK = 128
B_GROUPS = 4
B_GROUP_DIM = B_WIDTH // B_GROUPS
C_WIDTH = D_MODEL // 2
CONV_K = 31
CONV_PAD = (CONV_K - 1) // 2
D_WIDTH = D_MODEL - C_WIDTH
D_GROUPS = 4
D_GROUP_DIM = D_WIDTH // D_GROUPS
FFN_DIM = ((8 * D_MODEL // 3 + 255) // 256) * 256
N_EXPERTS = 8
TOP_K = 2
EXPERT_DIM = 7 * D_MODEL // 2
MOE_BLOCK = 256
N_EVEN = (DEPTH + 1) // 2
N_ODD = DEPTH // 2
AB_IN = 5 * A_WIDTH + 2 * B_WIDTH
CD_IN = 2 * C_WIDTH + D_WIDTH
AB_SPLITS = [A_WIDTH, 2 * A_WIDTH, 3 * A_WIDTH, 4 * A_WIDTH, 5 * A_WIDTH, 5 * A_WIDTH + B_WIDTH]
CD_SPLITS = [C_WIDTH, 2 * C_WIDTH]

kernel_name = 'hybrid_bidir_hgrn2_sgu_conv_fnet_encoder'


def rms_norm(x, g):
    xf = x.astype(jnp.float32)
    y = xf * lax.rsqrt(jnp.mean(xf * xf, axis=-1, keepdims=True) + EPS)
    return (y * g.astype(jnp.float32)).astype(x.dtype)


def layer_norm(x, g, b):
    xf = x.astype(jnp.float32)
    xc = xf - jnp.mean(xf, axis=-1, keepdims=True)
    y = xc * lax.rsqrt(jnp.mean(xc * xc, axis=-1, keepdims=True) + EPS)
    return (y * g.astype(jnp.float32) + b.astype(jnp.float32)).astype(x.dtype)


def swiglu(h, w1, w3, w2):
    return (jax.nn.silu(h @ w1) * (h @ w3)) @ w2


def gla_chunk_scan(q, log_f, k, v):
    bt, t, h, dk = q.shape
    dv = v.shape[-1]
    nc = t // HGRN_CHUNK

    def to_chunks(a):
        return a.reshape(bt, nc, HGRN_CHUNK, h, a.shape[-1]).transpose(1, 0, 3, 2, 4)

    lower = jnp.tril(jnp.ones((HGRN_CHUNK, HGRN_CHUNK), dtype=bool))[:, :, None]

    def step(state, inp):
        qc, lfc, kc, vc = inp
        b = jnp.cumsum(lfc, axis=2)
        b_last = b[:, :, -1:, :]
        o_inter = jnp.einsum('bhtk,bhkv->bhtv', qc * jnp.exp(b), state)
        decay = jnp.exp(jnp.where(lower, b[:, :, :, None, :] - b[:, :, None, :, :], -jnp.inf))
        scores = jnp.einsum('bhtk,bhtsk,bhsk->bhts', qc, decay, kc)
        o = o_inter + jnp.einsum('bhts,bhsv->bhtv', scores, vc)
        new_state = (state * jnp.exp(b_last)[:, :, 0, :, None]
                     + jnp.einsum('bhsk,bhsv->bhkv', kc * jnp.exp(b_last - b), vc))
        return new_state, o

    s0 = jnp.zeros((bt, h, dk, dv), jnp.float32)
    _, o = lax.scan(step, s0, (to_chunks(q), to_chunks(log_f), to_chunks(k), to_chunks(v)))
    return o.transpose(1, 0, 3, 2, 4).reshape(bt, t, h, dv)


def hgrn_lower_bound(lb_logits, layer):
    p = jax.nn.softmax(lb_logits.astype(jnp.float32), axis=0)
    return jnp.cumsum(p, axis=0)[layer]


def hgrn2_bidir(q_raw, f_fwd_raw, f_bwd_raw, i_raw, g_raw, lb, out_norm):
    bt, t, _ = q_raw.shape

    def heads(a):
        return a.astype(jnp.float32).reshape(bt, t, A_HEADS, A_HEAD_DIM)

    lbh = lb.reshape(A_HEADS, A_HEAD_DIM)
    q = jax.nn.silu(heads(q_raw))
    v = heads(i_raw)

    def forget(f_raw):
        f = lbh + (1.0 - lbh) * jax.nn.sigmoid(heads(f_raw))
        return jnp.log(f), 1.0 - f

    lf_f, k_f = forget(f_fwd_raw)
    lf_b, k_b = forget(f_bwd_raw)
    rev = lambda a: jnp.flip(a, axis=1)
    o = gla_chunk_scan(q, lf_f, k_f, v) + rev(gla_chunk_scan(rev(q), rev(lf_b), rev(k_b), rev(v)))
    o = rms_norm(o, out_norm.reshape(A_HEADS, A_HEAD_DIM)) * jax.nn.silu(heads(g_raw))
    return o.reshape(bt, t, A_WIDTH).astype(q_raw.dtype)


def chunked_sgu(u, v, norm_g, w_sp, b_sp):
    bt, t, _ = u.shape
    nc = t // B_CHUNK
    u = jax.nn.gelu(u)
    v = rms_norm(jax.nn.gelu(v), norm_g)
    vg = v.reshape(bt, nc, B_CHUNK, B_GROUPS, B_GROUP_DIM)
    mixed = jnp.einsum('gts,bcsgd->bctgd', w_sp, vg) + b_sp.T[None, None, :, :, None]
    return u * mixed.reshape(bt, t, B_WIDTH)


def conv_module(a, gate, w, b, ln_g, ln_b):
    y = a * jax.nn.sigmoid(gate)
    y = lax.conv_general_dilated(y, w[:, None, :].astype(y.dtype), window_strides=(1,),
                                 padding=[(CONV_PAD, CONV_PAD)],
                                 dimension_numbers=('NWC', 'WIO', 'NWC'),
                                 feature_group_count=C_WIDTH) + b
    return jax.nn.silu(layer_norm(y, ln_g, ln_b))


def fourier_mix(d):
    bt, t, _ = d.shape
    df = d.astype(jnp.float32).reshape(bt, t, D_GROUPS, D_GROUP_DIM)
    y = jnp.fft.fft2(df, axes=(1, 3), norm='ortho').real
    return y.reshape(bt, t, D_WIDTH).astype(d.dtype)


def moe_swiglu(h, router_w, w1, w3, w2):
    bt, t, d = h.shape
    n = bt * t
    nk = n * TOP_K
    n_blocks = (nk + MOE_BLOCK - 1) // MOE_BLOCK + N_EXPERTS
    xt = h.reshape(n, d)
    logits = (xt @ router_w).astype(jnp.float32)
    top_logits, top_idx = lax.top_k(logits, TOP_K)
    gates = jax.nn.softmax(top_logits, axis=-1)
    flat_e = top_idx.reshape(nk)
    flat_tok = jnp.repeat(jnp.arange(n, dtype=jnp.int32), TOP_K)
    flat_g = gates.reshape(nk)
    order = jnp.argsort(flat_e)
    sorted_e = flat_e[order]
    counts = jnp.bincount(flat_e, length=N_EXPERTS)
    padded = (counts + MOE_BLOCK - 1) // MOE_BLOCK * MOE_BLOCK
    start = jnp.cumsum(counts) - counts
    pend = jnp.cumsum(padded)
    pstart = pend - padded
    dest = pstart[sorted_e] + jnp.arange(nk, dtype=jnp.int32) - start[sorted_e]
    n_slots = n_blocks * MOE_BLOCK
    slot_tok = jnp.full((n_slots,), n, jnp.int32).at[dest].set(flat_tok[order])
    slot_gate = jnp.zeros((n_slots,), jnp.float32).at[dest].set(flat_g[order])
    block_e = jnp.minimum(jnp.searchsorted(pend, jnp.arange(n_blocks, dtype=jnp.int32) * MOE_BLOCK, side='right'),
                          N_EXPERTS - 1)
    x_pad = jnp.concatenate([xt, jnp.zeros((1, d), xt.dtype)], axis=0)
    xb = x_pad[slot_tok].reshape(n_blocks, MOE_BLOCK, d)

    def expert_block(args):
        xblk, e = args
        return swiglu(xblk, w1[e], w3[e], w2[e])

    yb = lax.map(expert_block, (xb, block_e)).reshape(n_slots, d)
    y = jnp.zeros((n + 1, d), yb.dtype).at[slot_tok].add(yb * slot_gate[:, None].astype(yb.dtype))
    return y[:n].reshape(bt, t, d)


def setup_inputs(seed: int = 0) -> dict:
    key = jax.random.key(seed)
    ks = iter(jax.random.split(key, 32))

    def nrm(shape, scale):
        return jax.random.normal(next(ks), shape, jnp.float32) * scale

    def gain(shape):
        return 1.0 + nrm(shape, 0.02)

    return {
        'x_prompt': nrm((BATCH, SEQ, D_MODEL), 1.0),
        'x_sample': nrm((DEC_BATCH, DEC_SEQ, D_MODEL), 1.0),
        'norm_mix': gain((DEPTH, D_MODEL)),
        'norm_ffn': gain((DEPTH, D_MODEL)),
        'norm_final': gain((D_MODEL,)),
        'w_in_ab': nrm((N_EVEN, D_MODEL, AB_IN), D_MODEL ** -0.5),
        'hgrn_lb_logits': nrm((DEPTH + 1, A_WIDTH), 0.5),
        'hgrn_out_norm': gain((N_EVEN, A_WIDTH)),
        'sgu_norm': gain((N_EVEN, B_WIDTH)),
        'sgu_w': nrm((N_EVEN, B_GROUPS, B_CHUNK, B_CHUNK), B_CHUNK ** -0.5),
        'sgu_b': 1.0 + nrm((N_EVEN, B_GROUPS, B_CHUNK), 0.1),
        'w_out_ab': nrm((N_EVEN, D_MODEL, D_MODEL), D_MODEL ** -0.5),
        'w_in_cd': nrm((N_ODD, D_MODEL, CD_IN), D_MODEL ** -0.5),
        'conv_w': nrm((N_ODD, CONV_K, C_WIDTH), CONV_K ** -0.5),
        'conv_b': nrm((N_ODD, C_WIDTH), 0.01),
        'conv_ln_g': gain((N_ODD, C_WIDTH)),
        'conv_ln_b': nrm((N_ODD, C_WIDTH), 0.01),
        'w_out_cd': nrm((N_ODD, D_MODEL, D_MODEL), D_MODEL ** -0.5),
        'ffn_w1': nrm((N_EVEN, D_MODEL, FFN_DIM), D_MODEL ** -0.5),
        'ffn_w3': nrm((N_EVEN, D_MODEL, FFN_DIM), D_MODEL ** -0.5),
        'ffn_w2': nrm((N_EVEN, FFN_DIM, D_MODEL), FFN_DIM ** -0.5),
        'router_w': nrm((N_ODD, D_MODEL, N_EXPERTS), D_MODEL ** -0.5),
        'moe_w1': nrm((N_ODD, N_EXPERTS, D_MODEL, EXPERT_DIM), D_MODEL ** -0.5),
        'moe_w3': nrm((N_ODD, N_EXPERTS, D_MODEL, EXPERT_DIM), D_MODEL ** -0.5),
        'moe_w2': nrm((N_ODD, N_EXPERTS, EXPERT_DIM, D_MODEL), EXPERT_DIM ** -0.5),
    }


def reference(x_prompt, x_sample, norm_mix, norm_ffn, norm_final, w_in_ab, hgrn_lb_logits, hgrn_out_norm,
              sgu_norm, sgu_w, sgu_b, w_out_ab, w_in_cd, conv_w, conv_b, conv_ln_g, conv_ln_b, w_out_cd,
              ffn_w1, ffn_w3, ffn_w2, router_w, moe_w1, moe_w3, moe_w2):
    def trunk(x):
        for l in range(DEPTH):
            h = rms_norm(x, norm_mix[l])
            if l % 2 == 0:
                j = l // 2
                z = h @ w_in_ab[j]
                q, f_fwd, f_bwd, i_in, g_out, u, v = jnp.split(z, AB_SPLITS, axis=-1)
                lb = hgrn_lower_bound(hgrn_lb_logits, l)
                a_out = hgrn2_bidir(q, f_fwd, f_bwd, i_in, g_out, lb, hgrn_out_norm[j])
                b_out = chunked_sgu(u, v, sgu_norm[j], sgu_w[j], sgu_b[j])
                x = x + jnp.concatenate([a_out, b_out], axis=-1) @ w_out_ab[j]
                x = x + swiglu(rms_norm(x, norm_ffn[l]), ffn_w1[j], ffn_w3[j], ffn_w2[j])
            else:
                j = l // 2
                z = h @ w_in_cd[j]
                c_val, c_gate, d_in = jnp.split(z, CD_SPLITS, axis=-1)
                c_out = conv_module(c_val, c_gate, conv_w[j], conv_b[j], conv_ln_g[j], conv_ln_b[j])
                d_out = fourier_mix(d_in)
                x = x + jnp.concatenate([c_out, d_out], axis=-1) @ w_out_cd[j]
                x = x + moe_swiglu(rms_norm(x, norm_ffn[l]), router_w[j], moe_w1[j], moe_w3[j], moe_w2[j])
        return rms_norm(x, norm_final)

    y_prompt = trunk(x_prompt)
    y_sample = trunk(x_sample)
    return (y_prompt, y_sample)
```

```python
import functools

import numpy as np
import jax
import jax.numpy as jnp
from jax import lax
from jax.experimental import pallas as pl
from jax.experimental.pallas import tpu as pltpu

F32 = jnp.float32
BF16 = jnp.bfloat16
EPS = 1e-6

D_MODEL = 1024
HALF = 512
HEAD = 128
N_HEADS = 4
HGRN_CHUNK = 64
HGRN_SUB = 16
CONV_K = 31
CONV_PAD = 15
CONV_HALO = 16
N_EXPERTS = 8
ROUTE_LANES = 128
MAX_EXP = 80.0

VMEM_LIMIT_BYTES = 56 * 1024 * 1024

TOK_TILE = 512
MOE_BLOCK = 1024
MOE_FTILE = 512
FFN_FTILE = 256


def _dot(a, b):
    return jnp.dot(a, b, preferred_element_type=F32)


def _dot_nt(a, b):
    return lax.dot_general(a, b, (((1,), (1,)), ((), ())), preferred_element_type=F32)


def _cparams(*sem):
    return pltpu.CompilerParams(dimension_semantics=sem, vmem_limit_bytes=VMEM_LIMIT_BYTES)


def _rms(x, g):
    ms = jnp.mean(x * x, axis=-1, keepdims=True)
    return x * lax.rsqrt(ms + EPS) * g


def _silu(x):
    return x * jax.nn.sigmoid(x)


def _gelu_tanh(x):
    return 0.5 * x * (1.0 + jnp.tanh(0.7978845608028654 * (x + 0.044715 * (x * x * x))))


def _norm_proj_kernel(x_ref, g_ref, w_ref, o_ref, *, tn):
    h = _rms(x_ref[...], g_ref[...]).astype(BF16)
    for c in range(o_ref.shape[1] // tn):
        o_ref[:, c * tn:(c + 1) * tn] = _dot(h, w_ref[:, c * tn:(c + 1) * tn])


def norm_proj(x, g, w_bf16, tm=TOK_TILE, tn=512):
    n, d = x.shape
    nout = w_bf16.shape[1]
    return pl.pallas_call(
        functools.partial(_norm_proj_kernel, tn=tn),
        grid=(n // tm,),
        in_specs=[pl.BlockSpec((tm, d), lambda i: (i, 0)),
                  pl.BlockSpec((1, d), lambda i: (0, 0)),
                  pl.BlockSpec((d, nout), lambda i: (0, 0))],
        out_specs=pl.BlockSpec((tm, nout), lambda i: (i, 0)),
        out_shape=jax.ShapeDtypeStruct((n, nout), F32),
        compiler_params=_cparams("parallel"),
        name="norm_proj_ab",
    )(x, g.reshape(1, d), w_bf16)


def _norm_proj_cd_kernel(x_ref, g_ref, w_ref, dft_ref, zc_ref, wv_ref):
    h = _rms(x_ref[...], g_ref[...]).astype(BF16)
    for c in range(2):
        zc_ref[:, c * HALF:(c + 1) * HALF] = _dot(h, w_ref[:, c * HALF:(c + 1) * HALF])
    d = _dot(h, w_ref[:, 2 * HALF:3 * HALF]).astype(BF16)
    for g in range(N_HEADS):
        r = _dot(d[:, g * HEAD:(g + 1) * HEAD], dft_ref[...])
        wv_ref[:, g * HEAD:(g + 1) * HEAD] = r[:, :HEAD].astype(BF16)
        wv_ref[:, HALF + g * HEAD:HALF + (g + 1) * HEAD] = r[:, HEAD:].astype(BF16)


def norm_proj_cd(x, g, w_bf16, dft_c, tm=TOK_TILE):
    n, d = x.shape
    return pl.pallas_call(
        _norm_proj_cd_kernel,
        grid=(n // tm,),
        in_specs=[pl.BlockSpec((tm, d), lambda i: (i, 0)),
                  pl.BlockSpec((1, d), lambda i: (0, 0)),
                  pl.BlockSpec((d, 3 * HALF), lambda i: (0, 0)),
                  pl.BlockSpec((HEAD, 2 * HEAD), lambda i: (0, 0))],
        out_specs=[pl.BlockSpec((tm, 2 * HALF), lambda i: (i, 0)),
                   pl.BlockSpec((tm, 2 * HALF), lambda i: (i, 0))],
        out_shape=[jax.ShapeDtypeStruct((n, 2 * HALF), F32),
                   jax.ShapeDtypeStruct((n, 2 * HALF), BF16)],
        compiler_params=_cparams("parallel"),
        name="norm_proj_cd",
    )(x, g.reshape(1, d), w_bf16, dft_c)


def _hgrn_kernel(reset_ref, q_ref, f_ref, v_ref, lbl_ref, o_ref, st_ref, *, reverse, layer, tt):
    c, sub = HGRN_CHUNK, HGRN_SUB
    n_sub = c // sub

    @pl.when(reset_ref[pl.program_id(0)] == 1)
    def _():
        st_ref[...] = jnp.zeros_like(st_ref)

    lg = lbl_ref[...]
    e = jnp.exp(lg - jnp.max(lg, axis=0, keepdims=True))
    p = e / jnp.sum(e, axis=0, keepdims=True)
    lb_all = p[0:1, :]
    for r in range(1, layer + 1):
        lb_all = lb_all + p[r:r + 1, :]

    row = lax.broadcasted_iota(jnp.int32, (c, c), 0)
    col = lax.broadcasted_iota(jnp.int32, (c, c), 1)
    tri = jnp.where((col >= row) if reverse else (col <= row), 1.0, 0.0).astype(BF16)

    def chunk_body(ci, carry):
        cc = (tt // c - 1 - ci) if reverse else ci
        r0 = pl.multiple_of(cc * c, c)
        for h in range(N_HEADS):
            cols = slice(h * HEAD, (h + 1) * HEAD)
            lb = lb_all[:, cols]
            q = _silu(q_ref[pl.ds(r0, c), cols])
            v = v_ref[pl.ds(r0, c), cols]
            f = lb + (1.0 - lb) * jax.nn.sigmoid(f_ref[pl.ds(r0, c), cols])
            lf = jnp.log(f)
            k = 1.0 - f
            lf_hi = lf.astype(BF16)
            lf_lo = (lf - lf_hi.astype(F32)).astype(BF16)
            b = _dot(tri, lf_hi) + _dot(tri, lf_lo)
            vb = v.astype(BF16)
            st = st_ref[h]
            o_inter = _dot_nt((q * jnp.exp(b)).astype(BF16), st.astype(BF16))
            for i in range(n_sub):
                rows = slice(i * sub, (i + 1) * sub)
                if reverse:
                    anchor = b[(i + 1) * sub:(i + 1) * sub + 1, :] if i + 1 < n_sub else jnp.zeros((1, HEAD), F32)
                    span = slice(i * sub, c)
                else:
                    anchor = b[i * sub - 1:i * sub, :] if i > 0 else jnp.zeros((1, HEAD), F32)
                    span = slice(0, (i + 1) * sub)
                n_span = span.stop - span.start
                qi = (q[rows] * jnp.exp(b[rows] - anchor)).astype(BF16)
                ki = (k[span] * jnp.exp(jnp.minimum(anchor - b[span], MAX_EXP))).astype(BF16)
                a = _dot_nt(qi, ki)
                tr = lax.broadcasted_iota(jnp.int32, (sub, n_span), 0)
                sc = lax.broadcasted_iota(jnp.int32, (sub, n_span), 1)
                keep = (sc >= tr) if reverse else (sc <= tr + i * sub)
                a = jnp.where(keep, a, 0.0).astype(BF16)
                o_i = o_inter[rows] + _dot(a, vb[span])
                o_ref[pl.ds(r0 + i * sub, sub), cols] = o_i
            edge = b[0:1, :] if reverse else b[c - 1:c, :]
            ks = (k * jnp.exp(edge - b)).astype(BF16)
            st_ref[h] = st * jnp.exp(edge) + _dot(v.T.astype(BF16), ks)
        return carry

    lax.fori_loop(0, tt // c, chunk_body, 0)


def hgrn_scan(z, lb_logits, resets, *, reverse, layer, f_col, tt=TOK_TILE):
    n = z.shape[0]
    nt = n // tt
    order = (lambda i, r: (nt - 1 - i, 0)) if reverse else (lambda i, r: (i, 0))
    blk = lambda cb: (lambda i, r: (order(i, r)[0], cb))
    grid_spec = pltpu.PrefetchScalarGridSpec(
        num_scalar_prefetch=1,
        grid=(nt,),
        in_specs=[pl.BlockSpec((tt, HALF), blk(0)),
                  pl.BlockSpec((tt, HALF), blk(f_col)),
                  pl.BlockSpec((tt, HALF), blk(3)),
                  pl.BlockSpec(lb_logits.shape, lambda i, r: (0, 0))],
        out_specs=pl.BlockSpec((tt, HALF), order),
        scratch_shapes=[pltpu.VMEM((N_HEADS, HEAD, HEAD), F32)],
    )
    return pl.pallas_call(
        functools.partial(_hgrn_kernel, reverse=reverse, layer=layer, tt=tt),
        grid_spec=grid_spec,
        out_shape=jax.ShapeDtypeStruct((n, HALF), F32),
        compiler_params=_cparams("arbitrary"),
        name="hgrn_bwd" if reverse else "hgrn_fwd",
    )(resets, z, z, z, lb_logits)


def _mix_ab_kernel(x_ref, of_ref, ob_ref, g_ref, u_ref, v_ref, onorm_ref, snorm_ref, wsp_ref, bsp_ref, wout_ref,
                   o_ref, *, tm):
    o = of_ref[...] + ob_ref[...]
    g = g_ref[...]
    parts = []
    for h in range(N_HEADS):
        cols = slice(h * HEAD, (h + 1) * HEAD)
        parts.append(_rms(o[:, cols], onorm_ref[:, cols]) * _silu(g[:, cols]))
    a = jnp.concatenate(parts, axis=1).astype(BF16)
    acc = x_ref[...] + _dot(a, wout_ref[0:HALF, :])

    u = _gelu_tanh(u_ref[...])
    vb = _rms(_gelu_tanh(v_ref[...]), snorm_ref[...]).astype(BF16)
    rows = []
    for c in range(tm // HEAD):
        mixed = []
        for gi in range(N_HEADS):
            vg = vb[c * HEAD:(c + 1) * HEAD, gi * HEAD:(gi + 1) * HEAD]
            mixed.append(_dot(wsp_ref[gi], vg) + bsp_ref[:, gi:gi + 1])
        rows.append(jnp.concatenate(mixed, axis=1))
    b = (u * jnp.concatenate(rows, axis=0)).astype(BF16)
    o_ref[...] = acc + _dot(b, wout_ref[HALF:2 * HALF, :])


def mix_ab(x, o_f, o_b, z, onorm, snorm, wsp_bf16, bsp_t, wout_bf16, tm=TOK_TILE):
    n, d = x.shape
    zblk = lambda cb: pl.BlockSpec((tm, HALF), lambda i: (i, cb))
    const = lambda shape: pl.BlockSpec(shape, lambda i: (0,) * len(shape))
    return pl.pallas_call(
        functools.partial(_mix_ab_kernel, tm=tm),
        grid=(n // tm,),
        in_specs=[pl.BlockSpec((tm, d), lambda i: (i, 0)),
                  pl.BlockSpec((tm, HALF), lambda i: (i, 0)),
                  pl.BlockSpec((tm, HALF), lambda i: (i, 0)),
                  zblk(4), zblk(5), zblk(6),
                  const((1, HALF)), const((1, HALF)),
                  const((N_HEADS, HEAD, HEAD)), const((HEAD, N_HEADS)),
                  const((d, d))],
        out_specs=pl.BlockSpec((tm, d), lambda i: (i, 0)),
        out_shape=jax.ShapeDtypeStruct((n, d), F32),
        compiler_params=_cparams("parallel"),
        name="mix_ab",
    )(x, o_f, o_b, z, z, z, onorm.reshape(1, HALF), snorm.reshape(1, HALF), wsp_bf16, bsp_t, wout_bf16)


def _ffn_kernel(x_ref, g_ref, w1_ref, w3_ref, w2_ref, o_ref, *, tf):
    x = x_ref[...]
    h = _rms(x, g_ref[...]).astype(BF16)
    acc = x
    for c in range(w1_ref.shape[1] // tf):
        cols = slice(c * tf, (c + 1) * tf)
        act = (_silu(_dot(h, w1_ref[:, cols])) * _dot(h, w3_ref[:, cols])).astype(BF16)
        acc = acc + _dot(act, w2_ref[cols, :])
    o_ref[...] = acc


def ffn(x, g, w1, w3, w2, tm=TOK_TILE, tf=FFN_FTILE):
    n, d = x.shape
    f = w1.shape[1]
    const = lambda shape: pl.BlockSpec(shape, lambda i: (0,) * len(shape))
    return pl.pallas_call(
        functools.partial(_ffn_kernel, tf=tf),
        grid=(n // tm,),
        in_specs=[pl.BlockSpec((tm, d), lambda i: (i, 0)), const((1, d)),
                  const((d, f)), const((d, f)), const((f, d))],
        out_specs=pl.BlockSpec((tm, d), lambda i: (i, 0)),
        out_shape=jax.ShapeDtypeStruct((n, d), F32),
        compiler_params=_cparams("parallel"),
        name="ffn_dense",
    )(x, g.reshape(1, d), w1, w3, w2)


def _conv_kernel(first_ref, last_ref, a_ref, gt_ref, ap_ref, gp_ref, an_ref, gn_ref, w_ref, b_ref, lng_ref,
                 lnb_ref, o_ref, ypad_ref, *, ts, rb):
    i = pl.program_id(0)
    halo = CONV_HALO
    keep_prev = jnp.where(first_ref[i] == 1, 0.0, 1.0)
    keep_next = jnp.where(last_ref[i] == 1, 0.0, 1.0)
    ypad_ref[0:halo, :] = ap_ref[...] * jax.nn.sigmoid(gp_ref[...]) * keep_prev
    ypad_ref[halo:halo + ts, :] = a_ref[...] * jax.nn.sigmoid(gt_ref[...])
    ypad_ref[halo + ts:2 * halo + ts, :] = an_ref[...] * jax.nn.sigmoid(gn_ref[...]) * keep_next

    def body(bi, carry):
        r0 = pl.multiple_of(bi * rb, rb)
        win = ypad_ref[pl.ds(r0, rb + 2 * halo), :]
        acc = jnp.zeros((rb, HALF), F32) + b_ref[...]
        for j in range(CONV_K):
            off = halo - CONV_PAD + j
            acc = acc + w_ref[j:j + 1, :] * win[off:off + rb, :]
        mu = jnp.mean(acc, axis=-1, keepdims=True)
        xc = acc - mu
        var = jnp.mean(xc * xc, axis=-1, keepdims=True)
        y = xc * lax.rsqrt(var + EPS) * lng_ref[...] + lnb_ref[...]
        o_ref[pl.ds(r0, rb), :] = _silu(y).astype(o_ref.dtype)
        return carry

    lax.fori_loop(0, ts // rb, body, 0)


def conv_module(zc, firsts, lasts, w, b, ln_g, ln_b, ts=TOK_TILE, rb=32):
    n = zc.shape[0]
    nt = n // ts
    hb = ts // CONV_HALO
    n_hb = n // CONV_HALO
    main = lambda cb: pl.BlockSpec((ts, HALF), lambda i, f, l: (i, cb))
    prev = lambda cb: pl.BlockSpec((CONV_HALO, HALF), lambda i, f, l: (jnp.maximum(i * hb - 1, 0), cb))
    nxt = lambda cb: pl.BlockSpec((CONV_HALO, HALF), lambda i, f, l: (jnp.minimum((i + 1) * hb, n_hb - 1), cb))
    const = lambda shape: pl.BlockSpec(shape, lambda i, f, l: (0,) * len(shape))
    grid_spec = pltpu.PrefetchScalarGridSpec(
        num_scalar_prefetch=2,
        grid=(nt,),
        in_specs=[main(0), main(1), prev(0), prev(1), nxt(0), nxt(1),
                  const((CONV_K, HALF)), const((1, HALF)), const((1, HALF)), const((1, HALF))],
        out_specs=pl.BlockSpec((ts, HALF), lambda i, f, l: (i, 0)),
        scratch_shapes=[pltpu.VMEM((ts + 2 * CONV_HALO, HALF), F32)],
    )
    return pl.pallas_call(
        functools.partial(_conv_kernel, ts=ts, rb=rb),
        grid_spec=grid_spec,
        out_shape=jax.ShapeDtypeStruct((n, HALF), BF16),
        compiler_params=_cparams("parallel"),
        name="conv_module",
    )(firsts, lasts, zc, zc, zc, zc, zc, zc, w, b.reshape(1, HALF), ln_g.reshape(1, HALF), ln_b.reshape(1, HALF))


def _seq_dft_kernel(c_ref, s_ref, wv_ref, o_ref, *, scale):
    acc = _dot(c_ref[...], wv_ref[:, 0:HALF]) - _dot(s_ref[...], wv_ref[:, HALF:2 * HALF])
    o_ref[...] = (acc * scale).astype(o_ref.dtype)


def seq_dft(wv, cos_t, sin_t, row0, n_seq, seq_len, tmo=256):
    tmo = min(tmo, seq_len)
    nt = seq_len // tmo
    b0 = row0 // seq_len
    scale = 1.0 / float(np.sqrt(seq_len * HEAD))
    return pl.pallas_call(
        functools.partial(_seq_dft_kernel, scale=scale),
        grid=(n_seq, nt),
        in_specs=[pl.BlockSpec((tmo, seq_len), lambda b, i: (i, 0)),
                  pl.BlockSpec((tmo, seq_len), lambda b, i: (i, 0)),
                  pl.BlockSpec((seq_len, 2 * HALF), lambda b, i: (b0 + b, 0))],
        out_specs=pl.BlockSpec((tmo, HALF), lambda b, i: (b * nt + i, 0)),
        out_shape=jax.ShapeDtypeStruct((n_seq * seq_len, HALF), BF16),
        compiler_params=_cparams("parallel", "parallel"),
        name="seq_dft_%d" % seq_len,
    )(cos_t, sin_t, wv)


def _mix_cd_router_kernel(x_ref, c_ref, d_ref, wout_ref, g_ref, rwh_ref, rwl_ref, x3_ref, h_ref, route_ref,
                          cnt_ref, carry_ref, *, tm):
    @pl.when(pl.program_id(0) == 0)
    def _():
        carry_ref[...] = jnp.zeros_like(carry_ref)

    x3 = x_ref[...] + _dot(c_ref[...], wout_ref[0:HALF, :]) + _dot(d_ref[...], wout_ref[HALF:2 * HALF, :])
    x3_ref[...] = x3
    h = _rms(x3, g_ref[...])
    h_ref[...] = h

    h_hi = h.astype(BF16)
    h_lo = (h - h_hi.astype(F32)).astype(BF16)
    logits = _dot(h_hi, rwh_ref[...]) + (_dot(h_lo, rwh_ref[...]) + _dot(h_hi, rwl_ref[...]))
    lane = lax.broadcasted_iota(jnp.int32, (tm, ROUTE_LANES), 1)
    neg = jnp.float32(-jnp.inf)
    logits = jnp.where(lane < N_EXPERTS, logits, neg)
    m1 = jnp.max(logits, axis=-1, keepdims=True)
    i1 = jnp.min(jnp.where(logits == m1, lane, ROUTE_LANES), axis=-1, keepdims=True)
    rest = jnp.where(lane == i1, neg, logits)
    m2 = jnp.max(rest, axis=-1, keepdims=True)
    i2 = jnp.min(jnp.where(rest == m2, lane, ROUTE_LANES), axis=-1, keepdims=True)
    e2 = jnp.exp(m2 - m1)
    g1 = 1.0 / (1.0 + e2)
    g2 = e2 / (1.0 + e2)

    onehot = jnp.where((lane == i1) | (lane == i2), 1.0, 0.0)
    row = lax.broadcasted_iota(jnp.int32, (tm, tm), 0)
    col = lax.broadcasted_iota(jnp.int32, (tm, tm), 1)
    strict = jnp.where(col < row, 1.0, 0.0).astype(BF16)
    before = _dot(strict, onehot.astype(BF16)) + carry_ref[0:1, :]
    r1 = jnp.sum(jnp.where(lane == i1, before, 0.0), axis=-1, keepdims=True)
    r2 = jnp.sum(jnp.where(lane == i2, before, 0.0), axis=-1, keepdims=True)
    total = carry_ref[0:1, :] + jnp.sum(onehot, axis=0, keepdims=True)
    carry_ref[...] = jnp.broadcast_to(total, carry_ref.shape)
    cnt_ref[...] = jnp.broadcast_to(total, cnt_ref.shape)

    route = jnp.where(lane == 0, i1.astype(F32), 0.0)
    route = jnp.where(lane == 1, i2.astype(F32), route)
    route = jnp.where(lane == 2, g1, route)
    route = jnp.where(lane == 3, g2, route)
    route = jnp.where(lane == 4, r1, route)
    route = jnp.where(lane == 5, r2, route)
    route_ref[...] = route


def mix_cd_router(x, c_out, d_out, wout_bf16, g, rw_hi, rw_lo, tm=TOK_TILE):
    n, d = x.shape
    const = lambda shape: pl.BlockSpec(shape, lambda i: (0,) * len(shape))
    tok = lambda w: pl.BlockSpec((tm, w), lambda i: (i, 0))
    return pl.pallas_call(
        functools.partial(_mix_cd_router_kernel, tm=tm),
        grid=(n // tm,),
        in_specs=[tok(d), tok(HALF), tok(HALF), const((d, d)), const((1, d)),
                  const((d, ROUTE_LANES)), const((d, ROUTE_LANES))],
        out_specs=[tok(d), tok(d), tok(ROUTE_LANES), const((8, ROUTE_LANES))],
        out_shape=[jax.ShapeDtypeStruct((n, d), F32), jax.ShapeDtypeStruct((n, d), F32),
                   jax.ShapeDtypeStruct((n, ROUTE_LANES), F32), jax.ShapeDtypeStruct((8, ROUTE_LANES), F32)],
        scratch_shapes=[pltpu.VMEM((8, ROUTE_LANES), F32)],
        compiler_params=_cparams("arbitrary"),
        name="mix_cd_router",
    )(x, c_out, d_out, wout_bf16, g.reshape(1, d), rw_hi, rw_lo)


def _dispatch_kernel(dest_ref, h_ref, xb_in_ref, xb_ref, sem, *, tm):
    del xb_in_ref

    def row_copy(r, slot):
        return pltpu.make_async_copy(h_ref.at[pl.ds(r, 1), :], xb_ref.at[pl.ds(slot, 1), :], sem)

    def issue(r, carry):
        row_copy(r, dest_ref[0, 0, 2 * r]).start()
        row_copy(r, dest_ref[0, 0, 2 * r + 1]).start()
        return carry

    lax.fori_loop(0, tm, issue, 0)

    def drain(r, carry):
        row_copy(0, 0).wait()
        row_copy(0, 0).wait()
        return carry

    lax.fori_loop(0, tm, drain, 0)


def moe_dispatch(h, dest, xb_init, tm=TOK_TILE):
    n, d = h.shape
    return pl.pallas_call(
        functools.partial(_dispatch_kernel, tm=tm),
        grid=(n // tm,),
        in_specs=[pl.BlockSpec((1, 1, 2 * tm), lambda i: (i, 0, 0), memory_space=pltpu.SMEM),
                  pl.BlockSpec((tm, d), lambda i: (i, 0)),
                  pl.BlockSpec(memory_space=pl.ANY)],
        out_specs=pl.BlockSpec(memory_space=pl.ANY),
        out_shape=jax.ShapeDtypeStruct(xb_init.shape, xb_init.dtype),
        scratch_shapes=[pltpu.SemaphoreType.DMA(())],
        input_output_aliases={2: 0},
        compiler_params=_cparams("arbitrary"),
        name="moe_dispatch",
    )(dest.reshape(n // tm, 1, 2 * tm), h, xb_init)


def _moe_ffn_kernel(be_ref, nu_ref, x_ref, w1_ref, w3_ref, w2_ref, o_ref):
    del be_ref
    b = pl.program_id(0)
    f = pl.program_id(1)

    @pl.when(b < nu_ref[0])
    def _():
        x = x_ref[...].astype(BF16)
        act = (_silu(_dot(x, w1_ref[0])) * _dot(x, w3_ref[0])).astype(BF16)
        y = _dot(act, w2_ref[0])

        @pl.when(f == 0)
        def _():
            o_ref[...] = y

        @pl.when(f > 0)
        def _():
            o_ref[...] += y

    @pl.when((b >= nu_ref[0]) & (f == 0))
    def _():
        o_ref[...] = jnp.zeros_like(o_ref)


def moe_ffn(xb, block_e, n_used, w1, w3, w2, tmb=MOE_BLOCK, tf=MOE_FTILE):
    n_slots, d = xb.shape
    nb = n_slots // tmb
    nf = w1.shape[2] // tf

    def bclamp(b, nu):
        return jnp.minimum(b, nu[0] - 1)

    def fclamp(b, f, nu):
        return jnp.where(b < nu[0], f, nf - 1)

    grid_spec = pltpu.PrefetchScalarGridSpec(
        num_scalar_prefetch=2,
        grid=(nb, nf),
        in_specs=[pl.BlockSpec((tmb, d), lambda b, f, be, nu: (bclamp(b, nu), 0)),
                  pl.BlockSpec((1, d, tf), lambda b, f, be, nu: (be[bclamp(b, nu)], 0, fclamp(b, f, nu))),
                  pl.BlockSpec((1, d, tf), lambda b, f, be, nu: (be[bclamp(b, nu)], 0, fclamp(b, f, nu))),
                  pl.BlockSpec((1, tf, d), lambda b, f, be, nu: (be[bclamp(b, nu)], fclamp(b, f, nu), 0))],
        out_specs=pl.BlockSpec((tmb, d), lambda b, f, be, nu: (b, 0)),
    )
    return pl.pallas_call(
        _moe_ffn_kernel,
        grid_spec=grid_spec,
        out_shape=jax.ShapeDtypeStruct((n_slots, d), F32),
        compiler_params=_cparams("arbitrary", "arbitrary"),
        name="moe_ffn",
    )(block_e, n_used, xb, w1, w3, w2)


def _combine_kernel(dest_ref, x_ref, route_ref, g_ref, yb_ref, o_ref, y0_ref, y1_ref, sem, *, tm):
    def row_copy(slot, buf, r):
        return pltpu.make_async_copy(yb_ref.at[pl.ds(slot, 1), :], buf.at[pl.ds(r, 1), :], sem)

    def issue(r, carry):
        row_copy(dest_ref[0, 0, 2 * r], y0_ref, r).start()
        row_copy(dest_ref[0, 0, 2 * r + 1], y1_ref, r).start()
        return carry

    lax.fori_loop(0, tm, issue, 0)

    def drain(r, carry):
        row_copy(0, y0_ref, 0).wait()
        row_copy(0, y1_ref, 0).wait()
        return carry

    lax.fori_loop(0, tm, drain, 0)

    route = route_ref[...]
    y = x_ref[...] + route[:, 2:3] * y0_ref[...] + route[:, 3:4] * y1_ref[...]
    o_ref[...] = _rms(y, g_ref[...])


def moe_combine(x, route, dest, yb, g, tm=TOK_TILE):
    n, d = x.shape
    return pl.pallas_call(
        functools.partial(_combine_kernel, tm=tm),
        grid=(n // tm,),
        in_specs=[pl.BlockSpec((1, 1, 2 * tm), lambda i: (i, 0, 0), memory_space=pltpu.SMEM),
                  pl.BlockSpec((tm, d), lambda i: (i, 0)),
                  pl.BlockSpec((tm, ROUTE_LANES), lambda i: (i, 0)),
                  pl.BlockSpec((1, d), lambda i: (0, 0)),
                  pl.BlockSpec(memory_space=pl.ANY)],
        out_specs=pl.BlockSpec((tm, d), lambda i: (i, 0)),
        out_shape=jax.ShapeDtypeStruct((n, d), F32),
        scratch_shapes=[pltpu.VMEM((tm, d), F32), pltpu.VMEM((tm, d), F32), pltpu.SemaphoreType.DMA(())],
        compiler_params=_cparams("arbitrary"),
        name="moe_combine",
    )(dest.reshape(n // tm, 1, 2 * tm), x, route, g.reshape(1, d), yb)


def _tile_flags(seq_lens, tile):
    firsts, lasts = [], []
    for length in seq_lens:
        k = length // tile
        firsts += [1] + [0] * (k - 1)
        lasts += [0] * (k - 1) + [1]
    return np.asarray(firsts, np.int32), np.asarray(lasts, np.int32)


def _dft_tables(t):
    j = lax.broadcasted_iota(jnp.int32, (t, t), 0)
    k = lax.broadcasted_iota(jnp.int32, (t, t), 1)
    ang = ((j * k) % t).astype(F32) * (2.0 * np.pi / t)
    return jnp.cos(ang).astype(BF16), jnp.sin(ang).astype(BF16)


def kernel(x_prompt, x_sample, norm_mix, norm_ffn, norm_final, w_in_ab, hgrn_lb_logits, hgrn_out_norm, sgu_norm,
           sgu_w, sgu_b, w_out_ab, w_in_cd, conv_w, conv_b, conv_ln_g, conv_ln_b, w_out_cd, ffn_w1, ffn_w3, ffn_w2,
           router_w, moe_w1, moe_w3, moe_w2):
    bp, tp, d = x_prompt.shape
    bs, ts_, _ = x_sample.shape
    n_p, n_s = bp * tp, bs * ts_
    n = n_p + n_s
    seq_lens = [tp] * bp + [ts_] * bs
    tile = TOK_TILE
    firsts, lasts = _tile_flags(seq_lens, tile)
    depth = norm_mix.shape[0]

    x = jnp.concatenate([x_prompt.reshape(n_p, d), x_sample.reshape(n_s, d)], axis=0)

    for layer in range(depth):
        j = layer // 2
        if layer % 2 == 0:
            z = norm_proj(x, norm_mix[layer], w_in_ab[j].astype(BF16))
            o_f = hgrn_scan(z, hgrn_lb_logits, jnp.asarray(firsts), reverse=False, layer=layer, f_col=1)
            o_b = hgrn_scan(z, hgrn_lb_logits, jnp.asarray(lasts[::-1].copy()), reverse=True, layer=layer, f_col=2)
            x = mix_ab(x, o_f, o_b, z, hgrn_out_norm[j], sgu_norm[j], sgu_w[j].astype(BF16), sgu_b[j].T,
                       w_out_ab[j].astype(BF16))
            x = ffn(x, norm_ffn[layer], ffn_w1[j].astype(BF16), ffn_w3[j].astype(BF16), ffn_w2[j].astype(BF16))
        else:
            ang = (lax.broadcasted_iota(jnp.int32, (HEAD, HEAD), 0) * lax.broadcasted_iota(jnp.int32, (HEAD, HEAD), 1)
                   % HEAD).astype(F32) * (2.0 * np.pi / HEAD)
            dft_c = jnp.concatenate([jnp.cos(ang), jnp.sin(ang)], axis=1).astype(BF16)
            zc, wv = norm_proj_cd(x, norm_mix[layer], w_in_cd[j].astype(BF16), dft_c)
            c_out = conv_module(zc, jnp.asarray(firsts), jnp.asarray(lasts), conv_w[j], conv_b[j], conv_ln_g[j],
                                conv_ln_b[j])
            cos_p, sin_p = _dft_tables(tp)
            cos_s, sin_s = _dft_tables(ts_)
            d_out = jnp.concatenate([seq_dft(wv, cos_p, sin_p, 0, bp, tp),
                                     seq_dft(wv, cos_s, sin_s, n_p, bs, ts_)], axis=0)

            rw = jnp.zeros((d, ROUTE_LANES), F32).at[:, :N_EXPERTS].set(router_w[j])
            rw_hi = rw.astype(BF16)
            rw_lo = (rw - rw_hi.astype(F32)).astype(BF16)
            x3, h, route, counts = mix_cd_router(x, c_out, d_out, w_out_cd[j].astype(BF16), norm_ffn[layer],
                                                 rw_hi, rw_lo)

            tmb = MOE_BLOCK
            n_blocks = (2 * n + tmb - 1) // tmb + N_EXPERTS
            cnt = counts[0, :N_EXPERTS].astype(jnp.int32)
            padded = (cnt + tmb - 1) // tmb * tmb
            pend = jnp.cumsum(padded)
            pstart = pend - padded
            e_idx = route[:, 0:2].astype(jnp.int32)
            dest = (pstart[e_idx] + route[:, 4:6].astype(jnp.int32)).reshape(2 * n)
            block_e = jnp.minimum(
                jnp.searchsorted(pend, jnp.arange(n_blocks, dtype=jnp.int32) * tmb, side='right'),
                N_EXPERTS - 1).astype(jnp.int32)
            n_used = (pend[-1] // tmb).astype(jnp.int32).reshape(1)

            xb = moe_dispatch(h, dest, jnp.zeros((n_blocks * tmb, d), F32))
            yb = moe_ffn(xb, block_e, n_used, moe_w1[j].astype(BF16), moe_w3[j].astype(BF16),
                         moe_w2[j].astype(BF16))
            if layer == depth - 1:
                x = moe_combine(x3, route, dest, yb, norm_final)
            else:
                raise NotImplementedError("an odd layer that is not the last one")

    return x[:n_p].reshape(bp, tp, d), x[n_p:].reshape(bs, ts_, d)
```

```python
import functools

import numpy as np
import jax
import jax.numpy as jnp
from jax import lax
from jax.experimental import pallas as pl
from jax.experimental.pallas import tpu as pltpu

F32 = jnp.float32
BF16 = jnp.bfloat16
EPS = 1e-6

D_MODEL = 1024
HALF = 512
HEAD = 128
N_HEADS = 4
HGRN_CHUNK = 64
HGRN_SUB = 16
HGRN_CHUNKS_PER_ITER = 2
CONV_K = 31
CONV_PAD = 15
CONV_HALO = 16
N_EXPERTS = 8
ROUTE_LANES = 128
MAX_EXP = 80.0

VMEM_LIMIT_BYTES = 56 * 1024 * 1024

TOK_TILE = 512
MOE_BLOCK = 1024
MOE_FTILE = 512
FFN_FTILE = 256
ROW_DMA_UNROLL = 8
DFT_GEN_ROWS = 128


def _dot(a, b):
    return jnp.dot(a, b, preferred_element_type=F32)


def _dot_nt(a, b):
    return lax.dot_general(a, b, (((1,), (1,)), ((), ())), preferred_element_type=F32)


def _cparams(*sem):
    return pltpu.CompilerParams(dimension_semantics=sem, vmem_limit_bytes=VMEM_LIMIT_BYTES)


def _rms(x, g):
    ms = jnp.mean(x * x, axis=-1, keepdims=True)
    return x * lax.rsqrt(ms + EPS) * g


def _silu(x):
    return x * jax.nn.sigmoid(x)


def _gelu_tanh(x):
    return 0.5 * x * (1.0 + jnp.tanh(0.7978845608028654 * (x + 0.044715 * (x * x * x))))


def _two_part_specs(tm, d, tiles_a):
    return (pl.BlockSpec((tm, d), lambda i: (jnp.minimum(i, tiles_a - 1), 0)),
            pl.BlockSpec((tm, d), lambda i: (jnp.maximum(i - tiles_a, 0), 0)))


def _two_part_read(xa_ref, xb_ref, tiles_a):
    return jnp.where(pl.program_id(0) < tiles_a, xa_ref[...], xb_ref[...])


def _norm_proj_kernel(xa_ref, xb_ref, g_ref, w_ref, o_ref, *, tn, tiles_a):
    h = _rms(_two_part_read(xa_ref, xb_ref, tiles_a), g_ref[...]).astype(BF16)
    for c in range(o_ref.shape[1] // tn):
        o_ref[:, c * tn:(c + 1) * tn] = _dot(h, w_ref[:, c * tn:(c + 1) * tn])


def norm_proj(xa, xb, g, w_bf16, tm=TOK_TILE, tn=512):
    d = xa.shape[1]
    n = xa.shape[0] + xb.shape[0]
    tiles_a = xa.shape[0] // tm
    nout = w_bf16.shape[1]
    return pl.pallas_call(
        functools.partial(_norm_proj_kernel, tn=tn, tiles_a=tiles_a),
        grid=(n // tm,),
        in_specs=[*_two_part_specs(tm, d, tiles_a),
                  pl.BlockSpec((1, d), lambda i: (0, 0)),
                  pl.BlockSpec((d, nout), lambda i: (0, 0))],
        out_specs=pl.BlockSpec((tm, nout), lambda i: (i, 0)),
        out_shape=jax.ShapeDtypeStruct((n, nout), F32),
        compiler_params=_cparams("parallel"),
        name="norm_proj_ab",
    )(xa, xb, g.reshape(1, d), w_bf16)


def _norm_proj_cd_kernel(x_ref, g_ref, w_ref, dft_ref, zc_ref, wv_ref):
    h = _rms(x_ref[...], g_ref[...]).astype(BF16)
    for c in range(2):
        zc_ref[:, c * HALF:(c + 1) * HALF] = _dot(h, w_ref[:, c * HALF:(c + 1) * HALF])
    d = _dot(h, w_ref[:, 2 * HALF:3 * HALF]).astype(BF16)
    for g in range(N_HEADS):
        r = _dot(d[:, g * HEAD:(g + 1) * HEAD], dft_ref[...])
        wv_ref[:, g * HEAD:(g + 1) * HEAD] = r[:, :HEAD].astype(BF16)
        wv_ref[:, HALF + g * HEAD:HALF + (g + 1) * HEAD] = r[:, HEAD:].astype(BF16)


def norm_proj_cd(x, g, w_bf16, dft_c, tm=TOK_TILE):
    n, d = x.shape
    return pl.pallas_call(
        _norm_proj_cd_kernel,
        grid=(n // tm,),
        in_specs=[pl.BlockSpec((tm, d), lambda i: (i, 0)),
                  pl.BlockSpec((1, d), lambda i: (0, 0)),
                  pl.BlockSpec((d, 3 * HALF), lambda i: (0, 0)),
                  pl.BlockSpec((HEAD, 2 * HEAD), lambda i: (0, 0))],
        out_specs=[pl.BlockSpec((tm, 2 * HALF), lambda i: (i, 0)),
                   pl.BlockSpec((tm, 2 * HALF), lambda i: (i, 0))],
        out_shape=[jax.ShapeDtypeStruct((n, 2 * HALF), F32),
                   jax.ShapeDtypeStruct((n, 2 * HALF), BF16)],
        compiler_params=_cparams("parallel"),
        name="norm_proj_cd",
    )(x, g.reshape(1, d), w_bf16, dft_c)


def _hgrn_kernel(reset_ref, q_ref, f_ref, v_ref, lbl_ref, o_ref, st0, st1, st2, st3, oi_ref, qe_ref, u_ref, d_ref,
                 *, reverse, layer, tt):
    c, sub = HGRN_CHUNK, HGRN_SUB
    n_sub = c // sub
    n_chunks = tt // c
    states = (st0, st1, st2, st3)

    @pl.when(reset_ref[pl.program_id(0)] == 1)
    def _():
        for st in states:
            st[...] = jnp.zeros_like(st)

    lg = lbl_ref[...]
    e = jnp.exp(lg - jnp.max(lg, axis=0, keepdims=True))
    p = e / jnp.sum(e, axis=0, keepdims=True)
    lb_all = p[0:1, :]
    for r in range(1, layer + 1):
        lb_all = lb_all + p[r:r + 1, :]

    row = lax.broadcasted_iota(jnp.int32, (c, c), 0)
    col = lax.broadcasted_iota(jnp.int32, (c, c), 1)
    tri = jnp.where((col >= row) if reverse else (col <= row), 1.0, 0.0).astype(BF16)
    spans, keeps = [], []
    for i in range(n_sub):
        span = slice(i * sub, c) if reverse else slice(0, (i + 1) * sub)
        n_span = span.stop - span.start
        tr = lax.broadcasted_iota(jnp.int32, (sub, n_span), 0)
        sc = lax.broadcasted_iota(jnp.int32, (sub, n_span), 1)
        spans.append(span)
        keeps.append((sc >= tr) if reverse else (sc <= tr + i * sub))

    units = [(cj, h) for cj in range(HGRN_CHUNKS_PER_ITER) for h in range(N_HEADS)]

    def local_body(it, carry):
        ci = [it * HGRN_CHUNKS_PER_ITER + cj for cj in range(HGRN_CHUNKS_PER_ITER)]
        r0 = [pl.multiple_of(x * c, c) for x in ci]
        vals = {}
        for cj, h in units:
            cols = slice(h * HEAD, (h + 1) * HEAD)
            lb = lb_all[:, cols]
            q = _silu(q_ref[pl.ds(r0[cj], c), cols])
            v = v_ref[pl.ds(r0[cj], c), cols]
            f = lb + (1.0 - lb) * jax.nn.sigmoid(f_ref[pl.ds(r0[cj], c), cols])
            lf = jnp.log(f)
            lf_hi = lf.astype(BF16)
            lf_lo = (lf - lf_hi.astype(F32)).astype(BF16)
            vals[cj, h] = (q, v, 1.0 - f, _dot(tri, lf_hi) + _dot(tri, lf_lo))
        scores = {}
        for cj, h in units:
            cols = slice(h * HEAD, (h + 1) * HEAD)
            q, v, k, b = vals[cj, h]
            qe_ref[pl.ds(r0[cj], c), cols] = (q * jnp.exp(b)).astype(BF16)
            edge = b[0:1, :] if reverse else b[c - 1:c, :]
            ks = (k * jnp.exp(edge - b)).astype(BF16)
            u_ref[ci[cj] * N_HEADS + h] = _dot(v.T.astype(BF16), ks)
            d_ref[ci[cj] * N_HEADS + h] = jnp.broadcast_to(jnp.exp(edge), (8, HEAD))
            for i in range(n_sub):
                rows = slice(i * sub, (i + 1) * sub)
                if reverse:
                    anchor = b[(i + 1) * sub:(i + 1) * sub + 1, :] if i + 1 < n_sub else jnp.zeros((1, HEAD), F32)
                else:
                    anchor = b[i * sub - 1:i * sub, :] if i > 0 else jnp.zeros((1, HEAD), F32)
                qi = (q[rows] * jnp.exp(b[rows] - anchor)).astype(BF16)
                ki = (k[spans[i]] * jnp.exp(jnp.minimum(anchor - b[spans[i]], MAX_EXP))).astype(BF16)
                scores[cj, h, i] = _dot_nt(qi, ki)
        for cj, h in units:
            cols = slice(h * HEAD, (h + 1) * HEAD)
            vb = vals[cj, h][1].astype(BF16)
            for i in range(n_sub):
                a = jnp.where(keeps[i], scores[cj, h, i], 0.0).astype(BF16)
                oi_ref[pl.ds(r0[cj] + i * sub, sub), cols] = _dot(a, vb[spans[i]])
        return carry

    lax.fori_loop(0, n_chunks // HGRN_CHUNKS_PER_ITER, local_body, 0)

    cur = [st[...] for st in states]
    for ci in range(n_chunks):
        cc = (n_chunks - 1 - ci) if reverse else ci
        for h in range(N_HEADS):
            cols = slice(h * HEAD, (h + 1) * HEAD)
            o_ref[cc * c:(cc + 1) * c, cols] = (oi_ref[cc * c:(cc + 1) * c, cols]
                                                + _dot_nt(qe_ref[cc * c:(cc + 1) * c, cols], cur[h].astype(BF16)))
        cur = [cur[h] * d_ref[cc * N_HEADS + h][0:1, :] + u_ref[cc * N_HEADS + h] for h in range(N_HEADS)]
    for h in range(N_HEADS):
        states[h][...] = cur[h]


def hgrn_scan(z, lb_logits, resets, *, reverse, layer, f_col, tt=TOK_TILE):
    n = z.shape[0]
    nt = n // tt
    n_units = tt // HGRN_CHUNK * N_HEADS
    order = (lambda i, r: (nt - 1 - i, 0)) if reverse else (lambda i, r: (i, 0))
    blk = lambda cb: (lambda i, r: (order(i, r)[0], cb))
    grid_spec = pltpu.PrefetchScalarGridSpec(
        num_scalar_prefetch=1,
        grid=(nt,),
        in_specs=[pl.BlockSpec((tt, HALF), blk(0)),
                  pl.BlockSpec((tt, HALF), blk(f_col)),
                  pl.BlockSpec((tt, HALF), blk(3)),
                  pl.BlockSpec(lb_logits.shape, lambda i, r: (0, 0))],
        out_specs=pl.BlockSpec((tt, HALF), order),
        scratch_shapes=[pltpu.VMEM((HEAD, HEAD), F32)] * N_HEADS + [
            pltpu.VMEM((tt, HALF), F32),
            pltpu.VMEM((tt, HALF), BF16),
            pltpu.VMEM((n_units, HEAD, HEAD), F32),
            pltpu.VMEM((n_units, 8, HEAD), F32)],
    )
    return pl.pallas_call(
        functools.partial(_hgrn_kernel, reverse=reverse, layer=layer, tt=tt),
        grid_spec=grid_spec,
        out_shape=jax.ShapeDtypeStruct((n, HALF), F32),
        compiler_params=_cparams("arbitrary"),
        name="hgrn_bwd" if reverse else "hgrn_fwd",
    )(resets, z, z, z, lb_logits)


def _mix_ab_kernel(xa_ref, xb_ref, of_ref, ob_ref, g_ref, u_ref, v_ref, onorm_ref, snorm_ref, wsp_ref, bsp_ref,
                   wout_ref, o_ref, *, tm, tiles_a):
    o = of_ref[...] + ob_ref[...]
    g = g_ref[...]
    parts = []
    for h in range(N_HEADS):
        cols = slice(h * HEAD, (h + 1) * HEAD)
        parts.append(_rms(o[:, cols], onorm_ref[:, cols]) * _silu(g[:, cols]))
    a = jnp.concatenate(parts, axis=1).astype(BF16)
    acc = _two_part_read(xa_ref, xb_ref, tiles_a) + _dot(a, wout_ref[0:HALF, :])

    u = _gelu_tanh(u_ref[...])
    vb = _rms(_gelu_tanh(v_ref[...]), snorm_ref[...]).astype(BF16)
    rows = []
    for c in range(tm // HEAD):
        mixed = []
        for gi in range(N_HEADS):
            vg = vb[c * HEAD:(c + 1) * HEAD, gi * HEAD:(gi + 1) * HEAD]
            mixed.append(_dot(wsp_ref[gi], vg) + bsp_ref[:, gi:gi + 1])
        rows.append(jnp.concatenate(mixed, axis=1))
    b = (u * jnp.concatenate(rows, axis=0)).astype(BF16)
    o_ref[...] = acc + _dot(b, wout_ref[HALF:2 * HALF, :])


def mix_ab(xa, xb, o_f, o_b, z, onorm, snorm, wsp_bf16, bsp_t, wout_bf16, tm=TOK_TILE):
    d = xa.shape[1]
    n = xa.shape[0] + xb.shape[0]
    tiles_a = xa.shape[0] // tm
    zblk = lambda cb: pl.BlockSpec((tm, HALF), lambda i: (i, cb))
    const = lambda shape: pl.BlockSpec(shape, lambda i: (0,) * len(shape))
    return pl.pallas_call(
        functools.partial(_mix_ab_kernel, tm=tm, tiles_a=tiles_a),
        grid=(n // tm,),
        in_specs=[*_two_part_specs(tm, d, tiles_a),
                  pl.BlockSpec((tm, HALF), lambda i: (i, 0)),
                  pl.BlockSpec((tm, HALF), lambda i: (i, 0)),
                  zblk(4), zblk(5), zblk(6),
                  const((1, HALF)), const((1, HALF)),
                  const((N_HEADS, HEAD, HEAD)), const((HEAD, N_HEADS)),
                  const((d, d))],
        out_specs=pl.BlockSpec((tm, d), lambda i: (i, 0)),
        out_shape=jax.ShapeDtypeStruct((n, d), F32),
        compiler_params=_cparams("parallel"),
        name="mix_ab",
    )(xa, xb, o_f, o_b, z, z, z, onorm.reshape(1, HALF), snorm.reshape(1, HALF), wsp_bf16, bsp_t, wout_bf16)


def _ffn_kernel(x_ref, g_ref, w1_ref, w3_ref, w2_ref, o_ref, *, tf):
    x = x_ref[...]
    h = _rms(x, g_ref[...]).astype(BF16)
    acc = x
    for c in range(w1_ref.shape[1] // tf):
        cols = slice(c * tf, (c + 1) * tf)
        act = (_silu(_dot(h, w1_ref[:, cols])) * _dot(h, w3_ref[:, cols])).astype(BF16)
        acc = acc + _dot(act, w2_ref[cols, :])
    o_ref[...] = acc


def ffn(x, g, w1, w3, w2, tm=TOK_TILE, tf=FFN_FTILE):
    n, d = x.shape
    f = w1.shape[1]
    const = lambda shape: pl.BlockSpec(shape, lambda i: (0,) * len(shape))
    return pl.pallas_call(
        functools.partial(_ffn_kernel, tf=tf),
        grid=(n // tm,),
        in_specs=[pl.BlockSpec((tm, d), lambda i: (i, 0)), const((1, d)),
                  const((d, f)), const((d, f)), const((f, d))],
        out_specs=pl.BlockSpec((tm, d), lambda i: (i, 0)),
        out_shape=jax.ShapeDtypeStruct((n, d), F32),
        compiler_params=_cparams("parallel"),
        name="ffn_dense",
    )(x, g.reshape(1, d), w1, w3, w2)


def _conv_kernel(first_ref, last_ref, a_ref, gt_ref, ap_ref, gp_ref, an_ref, gn_ref, w_ref, b_ref, lng_ref,
                 lnb_ref, o_ref, ypad_ref, *, ts, rb):
    i = pl.program_id(0)
    halo = CONV_HALO
    keep_prev = jnp.where(first_ref[i] == 1, 0.0, 1.0)
    keep_next = jnp.where(last_ref[i] == 1, 0.0, 1.0)
    ypad_ref[0:halo, :] = ap_ref[...] * jax.nn.sigmoid(gp_ref[...]) * keep_prev
    ypad_ref[halo:halo + ts, :] = a_ref[...] * jax.nn.sigmoid(gt_ref[...])
    ypad_ref[halo + ts:2 * halo + ts, :] = an_ref[...] * jax.nn.sigmoid(gn_ref[...]) * keep_next

    n_win = rb + 2 * halo

    def body(bi, carry):
        r0 = pl.multiple_of(bi * rb, rb)
        strips = []
        for s in range(HALF // HEAD):
            cols = slice(s * HEAD, (s + 1) * HEAD)
            win = ypad_ref[pl.ds(r0, n_win), cols]
            acc = jnp.zeros((rb, HEAD), F32) + b_ref[:, cols]
            for res in range(8):
                sh = win if res == 0 else pltpu.roll(win, n_win - res, axis=0)
                for j in range(CONV_K):
                    off = halo - CONV_PAD + j
                    if off % 8 == res:
                        base = off - res
                        acc = acc + w_ref[j:j + 1, cols] * sh[base:base + rb, :]
            strips.append(acc)
        acc = jnp.concatenate(strips, axis=1)
        mu = jnp.mean(acc, axis=-1, keepdims=True)
        xc = acc - mu
        var = jnp.mean(xc * xc, axis=-1, keepdims=True)
        y = xc * lax.rsqrt(var + EPS) * lng_ref[...] + lnb_ref[...]
        o_ref[pl.ds(r0, rb), :] = _silu(y).astype(o_ref.dtype)
        return carry

    lax.fori_loop(0, ts // rb, body, 0)


def conv_module(zc, firsts, lasts, w, b, ln_g, ln_b, ts=TOK_TILE, rb=64):
    n = zc.shape[0]
    nt = n // ts
    hb = ts // CONV_HALO
    n_hb = n // CONV_HALO
    main = lambda cb: pl.BlockSpec((ts, HALF), lambda i, f, l: (i, cb))
    prev = lambda cb: pl.BlockSpec((CONV_HALO, HALF), lambda i, f, l: (jnp.maximum(i * hb - 1, 0), cb))
    nxt = lambda cb: pl.BlockSpec((CONV_HALO, HALF), lambda i, f, l: (jnp.minimum((i + 1) * hb, n_hb - 1), cb))
    const = lambda shape: pl.BlockSpec(shape, lambda i, f, l: (0,) * len(shape))
    grid_spec = pltpu.PrefetchScalarGridSpec(
        num_scalar_prefetch=2,
        grid=(nt,),
        in_specs=[main(0), main(1), prev(0), prev(1), nxt(0), nxt(1),
                  const((CONV_K, HALF)), const((1, HALF)), const((1, HALF)), const((1, HALF))],
        out_specs=pl.BlockSpec((ts, HALF), lambda i, f, l: (i, 0)),
        scratch_shapes=[pltpu.VMEM((ts + 2 * CONV_HALO, HALF), F32)],
    )
    return pl.pallas_call(
        functools.partial(_conv_kernel, ts=ts, rb=rb),
        grid_spec=grid_spec,
        out_shape=jax.ShapeDtypeStruct((n, HALF), BF16),
        compiler_params=_cparams("parallel"),
        name="conv_module",
    )(firsts, lasts, zc, zc, zc, zc, zc, zc, w, b.reshape(1, HALF), ln_g.reshape(1, HALF), ln_b.reshape(1, HALF))


def _seq_dft_kernel(c_ref, s_ref, wv_ref, o_ref, *, scale):
    acc = _dot(c_ref[...], wv_ref[:, 0:HALF]) - _dot(s_ref[...], wv_ref[:, HALF:2 * HALF])
    o_ref[...] = (acc * scale).astype(o_ref.dtype)


def seq_dft(wv, cos_t, sin_t, row0, n_seq, seq_len, tmo=256):
    tmo = min(tmo, seq_len)
    nt = seq_len // tmo
    b0 = row0 // seq_len
    scale = 1.0 / float(np.sqrt(seq_len * HEAD))
    return pl.pallas_call(
        functools.partial(_seq_dft_kernel, scale=scale),
        grid=(n_seq, nt),
        in_specs=[pl.BlockSpec((tmo, seq_len), lambda b, i: (i, 0)),
                  pl.BlockSpec((tmo, seq_len), lambda b, i: (i, 0)),
                  pl.BlockSpec((seq_len, 2 * HALF), lambda b, i: (b0 + b, 0))],
        out_specs=pl.BlockSpec((tmo, HALF), lambda b, i: (b * nt + i, 0)),
        out_shape=jax.ShapeDtypeStruct((n_seq * seq_len, HALF), BF16),
        compiler_params=_cparams("parallel", "parallel"),
        name="seq_dft_%d" % seq_len,
    )(cos_t, sin_t, wv)


def _mix_cd_router_kernel(x_ref, c_ref, d_ref, wout_ref, g_ref, rwh_ref, rwl_ref, x3_ref, h_ref, route_ref,
                          cnt_ref, carry_ref, *, tm):
    @pl.when(pl.program_id(0) == 0)
    def _():
        carry_ref[...] = jnp.zeros_like(carry_ref)

    x3 = x_ref[...] + _dot(c_ref[...], wout_ref[0:HALF, :]) + _dot(d_ref[...], wout_ref[HALF:2 * HALF, :])
    x3_ref[...] = x3
    h = _rms(x3, g_ref[...])
    h_ref[...] = h

    h_hi = h.astype(BF16)
    h_lo = (h - h_hi.astype(F32)).astype(BF16)
    logits = _dot(h_hi, rwh_ref[...]) + (_dot(h_lo, rwh_ref[...]) + _dot(h_hi, rwl_ref[...]))
    lane = lax.broadcasted_iota(jnp.int32, (tm, ROUTE_LANES), 1)
    neg = jnp.float32(-jnp.inf)
    logits = jnp.where(lane < N_EXPERTS, logits, neg)
    m1 = jnp.max(logits, axis=-1, keepdims=True)
    i1 = jnp.min(jnp.where(logits == m1, lane, ROUTE_LANES), axis=-1, keepdims=True)
    rest = jnp.where(lane == i1, neg, logits)
    m2 = jnp.max(rest, axis=-1, keepdims=True)
    i2 = jnp.min(jnp.where(rest == m2, lane, ROUTE_LANES), axis=-1, keepdims=True)
    e2 = jnp.exp(m2 - m1)
    g1 = 1.0 / (1.0 + e2)
    g2 = e2 / (1.0 + e2)

    onehot = jnp.where((lane == i1) | (lane == i2), 1.0, 0.0)
    row = lax.broadcasted_iota(jnp.int32, (tm, tm), 0)
    col = lax.broadcasted_iota(jnp.int32, (tm, tm), 1)
    strict = jnp.where(col < row, 1.0, 0.0).astype(BF16)
    before = _dot(strict, onehot.astype(BF16)) + carry_ref[0:1, :]
    r1 = jnp.sum(jnp.where(lane == i1, before, 0.0), axis=-1, keepdims=True)
    r2 = jnp.sum(jnp.where(lane == i2, before, 0.0), axis=-1, keepdims=True)
    total = carry_ref[0:1, :] + jnp.sum(onehot, axis=0, keepdims=True)
    carry_ref[...] = jnp.broadcast_to(total, carry_ref.shape)
    cnt_ref[...] = jnp.broadcast_to(total, cnt_ref.shape)

    route = jnp.where(lane == 0, i1.astype(F32), 0.0)
    route = jnp.where(lane == 1, i2.astype(F32), route)
    route = jnp.where(lane == 2, g1, route)
    route = jnp.where(lane == 3, g2, route)
    route = jnp.where(lane == 4, r1, route)
    route = jnp.where(lane == 5, r2, route)
    route_ref[...] = route


def mix_cd_router(x, c_out, d_out, wout_bf16, g, rw_hi, rw_lo, tm=TOK_TILE):
    n, d = x.shape
    const = lambda shape: pl.BlockSpec(shape, lambda i: (0,) * len(shape))
    tok = lambda w: pl.BlockSpec((tm, w), lambda i: (i, 0))
    return pl.pallas_call(
        functools.partial(_mix_cd_router_kernel, tm=tm),
        grid=(n // tm,),
        in_specs=[tok(d), tok(HALF), tok(HALF), const((d, d)), const((1, d)),
                  const((d, ROUTE_LANES)), const((d, ROUTE_LANES))],
        out_specs=[tok(d), tok(d), tok(ROUTE_LANES), const((8, ROUTE_LANES))],
        out_shape=[jax.ShapeDtypeStruct((n, d), F32), jax.ShapeDtypeStruct((n, d), F32),
                   jax.ShapeDtypeStruct((n, ROUTE_LANES), F32), jax.ShapeDtypeStruct((8, ROUTE_LANES), F32)],
        scratch_shapes=[pltpu.VMEM((8, ROUTE_LANES), F32)],
        compiler_params=_cparams("arbitrary"),
        name="mix_cd_router",
    )(x, c_out, d_out, wout_bf16, g.reshape(1, d), rw_hi, rw_lo)


def _dispatch_kernel(dest_ref, h_ref, xb_in_ref, xb_ref, sem, *, tm):
    del xb_in_ref

    def row_copy(r, slot):
        return pltpu.make_async_copy(h_ref.at[pl.ds(r, 1), :], xb_ref.at[pl.ds(slot, 1), :], sem)

    def issue(r, carry):
        row_copy(r, dest_ref[0, 0, 2 * r]).start()
        row_copy(r, dest_ref[0, 0, 2 * r + 1]).start()
        return carry

    lax.fori_loop(0, tm, issue, 0, unroll=ROW_DMA_UNROLL)

    for _ in range(2):
        pltpu.make_async_copy(h_ref, xb_ref.at[pl.ds(0, tm), :], sem).wait()


def moe_dispatch(h, dest, xb_init, tm=TOK_TILE):
    n, d = h.shape
    return pl.pallas_call(
        functools.partial(_dispatch_kernel, tm=tm),
        grid=(n // tm,),
        in_specs=[pl.BlockSpec((1, 1, 2 * tm), lambda i: (i, 0, 0), memory_space=pltpu.SMEM),
                  pl.BlockSpec((tm, d), lambda i: (i, 0)),
                  pl.BlockSpec(memory_space=pl.ANY)],
        out_specs=pl.BlockSpec(memory_space=pl.ANY),
        out_shape=jax.ShapeDtypeStruct(xb_init.shape, xb_init.dtype),
        scratch_shapes=[pltpu.SemaphoreType.DMA(())],
        input_output_aliases={2: 0},
        compiler_params=_cparams("arbitrary"),
        name="moe_dispatch",
    )(dest.reshape(n // tm, 1, 2 * tm), h, xb_init)


def _moe_ffn_kernel(be_ref, nu_ref, x_ref, w1_ref, w3_ref, w2_ref, o_ref):
    del be_ref
    b = pl.program_id(0)
    f = pl.program_id(1)

    @pl.when(b < nu_ref[0])
    def _():
        x = x_ref[...].astype(BF16)
        act = (_silu(_dot(x, w1_ref[0])) * _dot(x, w3_ref[0])).astype(BF16)
        y = _dot(act, w2_ref[0])

        @pl.when(f == 0)
        def _():
            o_ref[...] = y

        @pl.when(f > 0)
        def _():
            o_ref[...] += y

    @pl.when((b >= nu_ref[0]) & (f == 0))
    def _():
        o_ref[...] = jnp.zeros_like(o_ref)


def moe_ffn(xb, block_e, n_used, w1, w3, w2, tmb=MOE_BLOCK, tf=MOE_FTILE):
    n_slots, d = xb.shape
    nb = n_slots // tmb
    nf = w1.shape[2] // tf

    def bclamp(b, nu):
        return jnp.minimum(b, nu[0] - 1)

    def fclamp(b, f, nu):
        return jnp.where(b < nu[0], f, nf - 1)

    grid_spec = pltpu.PrefetchScalarGridSpec(
        num_scalar_prefetch=2,
        grid=(nb, nf),
        in_specs=[pl.BlockSpec((tmb, d), lambda b, f, be, nu: (bclamp(b, nu), 0)),
                  pl.BlockSpec((1, d, tf), lambda b, f, be, nu: (be[bclamp(b, nu)], 0, fclamp(b, f, nu))),
                  pl.BlockSpec((1, d, tf), lambda b, f, be, nu: (be[bclamp(b, nu)], 0, fclamp(b, f, nu))),
                  pl.BlockSpec((1, tf, d), lambda b, f, be, nu: (be[bclamp(b, nu)], fclamp(b, f, nu), 0))],
        out_specs=pl.BlockSpec((tmb, d), lambda b, f, be, nu: (b, 0)),
    )
    return pl.pallas_call(
        _moe_ffn_kernel,
        grid_spec=grid_spec,
        out_shape=jax.ShapeDtypeStruct((n_slots, d), F32),
        compiler_params=_cparams("arbitrary", "arbitrary"),
        name="moe_ffn",
    )(block_e, n_used, xb, w1, w3, w2)


def _combine_kernel(dest_ref, x_ref, route_ref, g_ref, yb_ref, oa_ref, ob_ref, y0_ref, y1_ref, sem, *, tm, tiles_a):
    def row_copy(slot, buf, r):
        return pltpu.make_async_copy(yb_ref.at[pl.ds(slot, 1), :], buf.at[pl.ds(r, 1), :], sem)

    def issue(r, carry):
        row_copy(dest_ref[0, 0, 2 * r], y0_ref, r).start()
        row_copy(dest_ref[0, 0, 2 * r + 1], y1_ref, r).start()
        return carry

    lax.fori_loop(0, tm, issue, 0, unroll=ROW_DMA_UNROLL)

    for buf in (y0_ref, y1_ref):
        pltpu.make_async_copy(yb_ref.at[pl.ds(0, tm), :], buf, sem).wait()

    route = route_ref[...]
    y = x_ref[...] + route[:, 2:3] * y0_ref[...] + route[:, 3:4] * y1_ref[...]
    res = _rms(y, g_ref[...])

    @pl.when(pl.program_id(0) < tiles_a)
    def _():
        oa_ref[...] = res

    @pl.when(pl.program_id(0) >= tiles_a)
    def _():
        ob_ref[...] = res


def moe_combine(x, route, dest, yb, g, n_a, tm=TOK_TILE):
    n, d = x.shape
    tiles_a = n_a // tm
    return pl.pallas_call(
        functools.partial(_combine_kernel, tm=tm, tiles_a=tiles_a),
        grid=(n // tm,),
        in_specs=[pl.BlockSpec((1, 1, 2 * tm), lambda i: (i, 0, 0), memory_space=pltpu.SMEM),
                  pl.BlockSpec((tm, d), lambda i: (i, 0)),
                  pl.BlockSpec((tm, ROUTE_LANES), lambda i: (i, 0)),
                  pl.BlockSpec((1, d), lambda i: (0, 0)),
                  pl.BlockSpec(memory_space=pl.ANY)],
        out_specs=list(_two_part_specs(tm, d, tiles_a)),
        out_shape=[jax.ShapeDtypeStruct((n_a, d), F32), jax.ShapeDtypeStruct((n - n_a, d), F32)],
        scratch_shapes=[pltpu.VMEM((tm, d), F32), pltpu.VMEM((tm, d), F32), pltpu.SemaphoreType.DMA(())],
        compiler_params=_cparams("arbitrary"),
        name="moe_combine",
    )(dest.reshape(n // tm, 1, 2 * tm), x, route, g.reshape(1, d), yb)


def _tile_flags(seq_lens, tile):
    firsts, lasts = [], []
    for length in seq_lens:
        k = length // tile
        firsts += [1] + [0] * (k - 1)
        lasts += [0] * (k - 1) + [1]
    return np.asarray(firsts, np.int32), np.asarray(lasts, np.int32)


def _dft_table_kernel(cphi_ref, sphi_ref, cth_ref, sth_ref, c_ref, s_ref):
    cphi, sphi = cphi_ref[...], sphi_ref[...]
    cth, sth = cth_ref[0], sth_ref[0]
    c_ref[...] = (cphi * cth - sphi * sth).astype(BF16)
    s_ref[...] = (sphi * cth + cphi * sth).astype(BF16)


def _angles(rows, t):
    k = lax.broadcasted_iota(jnp.int32, (rows.shape[0], t), 1)
    ang = ((rows[:, None] * k) % t).astype(F32) * (2.0 * np.pi / t)
    return jnp.cos(ang), jnp.sin(ang)


def _dft_tables(t):
    rows = min(DFT_GEN_ROWS, t)
    nt = t // rows
    cphi, sphi = _angles(jnp.arange(rows, dtype=jnp.int32), t)
    cth, sth = _angles(jnp.arange(nt, dtype=jnp.int32) * rows, t)
    tile = pl.BlockSpec((rows, t), lambda i: (0, 0))
    vec = pl.BlockSpec((1, 1, t), lambda i: (i, 0, 0))
    out = pl.BlockSpec((rows, t), lambda i: (i, 0))
    return pl.pallas_call(
        _dft_table_kernel,
        grid=(nt,),
        in_specs=[tile, tile, vec, vec],
        out_specs=[out, out],
        out_shape=[jax.ShapeDtypeStruct((t, t), BF16)] * 2,
        compiler_params=_cparams("parallel"),
        name="dft_table_%d" % t,
    )(cphi, sphi, cth.reshape(nt, 1, t), sth.reshape(nt, 1, t))


def kernel(x_prompt, x_sample, norm_mix, norm_ffn, norm_final, w_in_ab, hgrn_lb_logits, hgrn_out_norm, sgu_norm,
           sgu_w, sgu_b, w_out_ab, w_in_cd, conv_w, conv_b, conv_ln_g, conv_ln_b, w_out_cd, ffn_w1, ffn_w3, ffn_w2,
           router_w, moe_w1, moe_w3, moe_w2):
    bp, tp, d = x_prompt.shape
    bs, ts_, _ = x_sample.shape
    n_p, n_s = bp * tp, bs * ts_
    n = n_p + n_s
    seq_lens = [tp] * bp + [ts_] * bs
    tile = TOK_TILE
    firsts, lasts = _tile_flags(seq_lens, tile)
    depth = norm_mix.shape[0]
    assert depth == 2, "the layer schedule below is the two-layer trunk"
    xa, xb_in = x_prompt.reshape(n_p, d), x_sample.reshape(n_s, d)

    for layer in range(depth):
        j = layer // 2
        if layer % 2 == 0:
            z = norm_proj(xa, xb_in, norm_mix[layer], w_in_ab[j].astype(BF16))
            o_f = hgrn_scan(z, hgrn_lb_logits, jnp.asarray(firsts), reverse=False, layer=layer, f_col=1)
            o_b = hgrn_scan(z, hgrn_lb_logits, jnp.asarray(lasts[::-1].copy()), reverse=True, layer=layer, f_col=2)
            x = mix_ab(xa, xb_in, o_f, o_b, z, hgrn_out_norm[j], sgu_norm[j], sgu_w[j].astype(BF16), sgu_b[j].T,
                       w_out_ab[j].astype(BF16))
            x = ffn(x, norm_ffn[layer], ffn_w1[j].astype(BF16), ffn_w3[j].astype(BF16), ffn_w2[j].astype(BF16))
        else:
            ang = (lax.broadcasted_iota(jnp.int32, (HEAD, HEAD), 0) * lax.broadcasted_iota(jnp.int32, (HEAD, HEAD), 1)
                   % HEAD).astype(F32) * (2.0 * np.pi / HEAD)
            dft_c = jnp.concatenate([jnp.cos(ang), jnp.sin(ang)], axis=1).astype(BF16)
            zc, wv = norm_proj_cd(x, norm_mix[layer], w_in_cd[j].astype(BF16), dft_c)
            c_out = conv_module(zc, jnp.asarray(firsts), jnp.asarray(lasts), conv_w[j], conv_b[j], conv_ln_g[j],
                                conv_ln_b[j])
            cos_p, sin_p = _dft_tables(tp)
            cos_s, sin_s = _dft_tables(ts_)
            d_out = jnp.concatenate([seq_dft(wv, cos_p, sin_p, 0, bp, tp),
                                     seq_dft(wv, cos_s, sin_s, n_p, bs, ts_)], axis=0)

            rw = jnp.zeros((d, ROUTE_LANES), F32).at[:, :N_EXPERTS].set(router_w[j])
            rw_hi = rw.astype(BF16)
            rw_lo = (rw - rw_hi.astype(F32)).astype(BF16)
            x3, h, route, counts = mix_cd_router(x, c_out, d_out, w_out_cd[j].astype(BF16), norm_ffn[layer],
                                                 rw_hi, rw_lo)

            tmb = MOE_BLOCK
            n_blocks = (2 * n + tmb - 1) // tmb + N_EXPERTS
            cnt = counts[0, :N_EXPERTS].astype(jnp.int32)
            padded = (cnt + tmb - 1) // tmb * tmb
            pend = jnp.cumsum(padded)
            pstart = pend - padded
            e_idx = route[:, 0:2].astype(jnp.int32)
            dest = (pstart[e_idx] + route[:, 4:6].astype(jnp.int32)).reshape(2 * n)
            block_e = jnp.minimum(
                jnp.searchsorted(pend, jnp.arange(n_blocks, dtype=jnp.int32) * tmb, side='right'),
                N_EXPERTS - 1).astype(jnp.int32)
            n_used = (pend[-1] // tmb).astype(jnp.int32).reshape(1)

            xb = moe_dispatch(h, dest, jnp.zeros((n_blocks * tmb, d), F32))
            yb = moe_ffn(xb, block_e, n_used, moe_w1[j].astype(BF16), moe_w3[j].astype(BF16),
                         moe_w2[j].astype(BF16))
            out_p, out_s = moe_combine(x3, route, dest, yb, norm_final, n_p)

    return out_p.reshape(bp, tp, d), out_s.reshape(bs, ts_, d)
```

```python
import functools

import numpy as np
import jax
import jax.numpy as jnp
from jax import lax
from jax.experimental import pallas as pl
from jax.experimental.pallas import tpu as pltpu

F32 = jnp.float32
BF16 = jnp.bfloat16
EPS = 1e-6

D_MODEL = 1024
HALF = 512
HEAD = 128
N_HEADS = 4
HGRN_CHUNK = 64
HGRN_SUB = 16
HGRN_CHUNKS_PER_ITER = 4
CONV_K = 31
CONV_PAD = 15
CONV_HALO = 16
N_EXPERTS = 8
ROUTE_LANES = 128
MAX_EXP = 80.0

VMEM_LIMIT_BYTES = 56 * 1024 * 1024

TOK_TILE = 512
MOE_BLOCK = 1024
MOE_FTILE = 1792
MOE_ROWS_SUB = 512
MOE_COLS_SUB = 256
FFN_FTILE = 256
ROW_DMA_UNROLL = 8
SEQ_DFT_TILE_BYTES = 4 * 1024 * 1024
DFT_GEN_ROWS = 128


def _dot(a, b):
    return jnp.dot(a, b, preferred_element_type=F32)


def _dot_nt(a, b):
    return lax.dot_general(a, b, (((1,), (1,)), ((), ())), preferred_element_type=F32)


def _cparams(*sem):
    return pltpu.CompilerParams(dimension_semantics=sem, vmem_limit_bytes=VMEM_LIMIT_BYTES)


def _rms(x, g):
    ms = jnp.mean(x * x, axis=-1, keepdims=True)
    return x * lax.rsqrt(ms + EPS) * g


def _silu(x):
    return x * jax.nn.sigmoid(x)


def _gelu_tanh(x):
    return 0.5 * x * (1.0 + jnp.tanh(0.7978845608028654 * (x + 0.044715 * (x * x * x))))


def _two_part_specs(tm, d, tiles_a):
    return (pl.BlockSpec((tm, d), lambda i: (jnp.minimum(i, tiles_a - 1), 0)),
            pl.BlockSpec((tm, d), lambda i: (jnp.maximum(i - tiles_a, 0), 0)))


def _two_part_read(xa_ref, xb_ref, tiles_a):
    return jnp.where(pl.program_id(0) < tiles_a, xa_ref[...], xb_ref[...])


def _norm_proj_kernel(xa_ref, xb_ref, g_ref, w_ref, o_ref, *, tn, tiles_a):
    h = _rms(_two_part_read(xa_ref, xb_ref, tiles_a), g_ref[...]).astype(BF16)
    for c in range(o_ref.shape[1] // tn):
        o_ref[:, c * tn:(c + 1) * tn] = _dot(h, w_ref[:, c * tn:(c + 1) * tn])


def norm_proj(xa, xb, g, w_bf16, tm=TOK_TILE, tn=512):
    d = xa.shape[1]
    n = xa.shape[0] + xb.shape[0]
    tiles_a = xa.shape[0] // tm
    nout = w_bf16.shape[1]
    return pl.pallas_call(
        functools.partial(_norm_proj_kernel, tn=tn, tiles_a=tiles_a),
        grid=(n // tm,),
        in_specs=[*_two_part_specs(tm, d, tiles_a),
                  pl.BlockSpec((1, d), lambda i: (0, 0)),
                  pl.BlockSpec((d, nout), lambda i: (0, 0))],
        out_specs=pl.BlockSpec((tm, nout), lambda i: (i, 0)),
        out_shape=jax.ShapeDtypeStruct((n, nout), F32),
        compiler_params=_cparams("parallel"),
        name="norm_proj_ab",
    )(xa, xb, g.reshape(1, d), w_bf16)


def _norm_proj_cd_kernel(x_ref, g_ref, w_ref, dft_ref, zc_ref, wv_ref):
    h = _rms(x_ref[...], g_ref[...]).astype(BF16)
    for c in range(2):
        zc_ref[:, c * HALF:(c + 1) * HALF] = _dot(h, w_ref[:, c * HALF:(c + 1) * HALF])
    d = _dot(h, w_ref[:, 2 * HALF:3 * HALF]).astype(BF16)
    for g in range(N_HEADS):
        r = _dot(d[:, g * HEAD:(g + 1) * HEAD], dft_ref[...])
        wv_ref[:, g * HEAD:(g + 1) * HEAD] = r[:, :HEAD].astype(BF16)
        wv_ref[:, HALF + g * HEAD:HALF + (g + 1) * HEAD] = r[:, HEAD:].astype(BF16)


def norm_proj_cd(x, g, w_bf16, dft_c, tm=TOK_TILE):
    n, d = x.shape
    return pl.pallas_call(
        _norm_proj_cd_kernel,
        grid=(n // tm,),
        in_specs=[pl.BlockSpec((tm, d), lambda i: (i, 0)),
                  pl.BlockSpec((1, d), lambda i: (0, 0)),
                  pl.BlockSpec((d, 3 * HALF), lambda i: (0, 0)),
                  pl.BlockSpec((HEAD, 2 * HEAD), lambda i: (0, 0))],
        out_specs=[pl.BlockSpec((tm, 2 * HALF), lambda i: (i, 0)),
                   pl.BlockSpec((tm, 2 * HALF), lambda i: (i, 0))],
        out_shape=[jax.ShapeDtypeStruct((n, 2 * HALF), F32),
                   jax.ShapeDtypeStruct((n, 2 * HALF), BF16)],
        compiler_params=_cparams("parallel"),
        name="norm_proj_cd",
    )(x, g.reshape(1, d), w_bf16, dft_c)


def _hgrn_kernel(reset_ref, q_ref, f_ref, v_ref, lbl_ref, o_ref, st0, st1, st2, st3, oi_ref, qe_ref, u_ref, d_ref,
                 *, reverse, layer, tt):
    c, sub = HGRN_CHUNK, HGRN_SUB
    n_sub = c // sub
    n_chunks = tt // c
    states = (st0, st1, st2, st3)

    @pl.when(reset_ref[pl.program_id(0)] == 1)
    def _():
        for st in states:
            st[...] = jnp.zeros_like(st)

    lg = lbl_ref[...]
    e = jnp.exp(lg - jnp.max(lg, axis=0, keepdims=True))
    p = e / jnp.sum(e, axis=0, keepdims=True)
    lb_all = p[0:1, :]
    for r in range(1, layer + 1):
        lb_all = lb_all + p[r:r + 1, :]

    row = lax.broadcasted_iota(jnp.int32, (c, c), 0)
    col = lax.broadcasted_iota(jnp.int32, (c, c), 1)
    tri = jnp.where((col >= row) if reverse else (col <= row), 1.0, 0.0).astype(BF16)
    spans, keeps = [], []
    for i in range(n_sub):
        span = slice(i * sub, c) if reverse else slice(0, (i + 1) * sub)
        n_span = span.stop - span.start
        tr = lax.broadcasted_iota(jnp.int32, (sub, n_span), 0)
        sc = lax.broadcasted_iota(jnp.int32, (sub, n_span), 1)
        spans.append(span)
        keeps.append((sc >= tr) if reverse else (sc <= tr + i * sub))

    units = [(cj, h) for cj in range(HGRN_CHUNKS_PER_ITER) for h in range(N_HEADS)]

    def local_body(it, carry):
        ci = [it * HGRN_CHUNKS_PER_ITER + cj for cj in range(HGRN_CHUNKS_PER_ITER)]
        r0 = [pl.multiple_of(x * c, c) for x in ci]
        vals = {}
        for cj, h in units:
            cols = slice(h * HEAD, (h + 1) * HEAD)
            lb = lb_all[:, cols]
            q = _silu(q_ref[pl.ds(r0[cj], c), cols])
            v = v_ref[pl.ds(r0[cj], c), cols]
            f = lb + (1.0 - lb) * jax.nn.sigmoid(f_ref[pl.ds(r0[cj], c), cols])
            lf = jnp.log(f)
            lf_hi = lf.astype(BF16)
            lf_lo = (lf - lf_hi.astype(F32)).astype(BF16)
            vals[cj, h] = (q, v, 1.0 - f, _dot(tri, lf_hi) + _dot(tri, lf_lo))
        scores = {}
        for cj, h in units:
            cols = slice(h * HEAD, (h + 1) * HEAD)
            q, v, k, b = vals[cj, h]
            qe_ref[pl.ds(r0[cj], c), cols] = (q * jnp.exp(b)).astype(BF16)
            edge = b[0:1, :] if reverse else b[c - 1:c, :]
            ks = (k * jnp.exp(edge - b)).astype(BF16)
            u_ref[ci[cj] * N_HEADS + h] = _dot(v.T.astype(BF16), ks)
            d_ref[ci[cj] * N_HEADS + h] = jnp.broadcast_to(jnp.exp(edge), (8, HEAD))
            for i in range(n_sub):
                rows = slice(i * sub, (i + 1) * sub)
                if reverse:
                    anchor = b[(i + 1) * sub:(i + 1) * sub + 1, :] if i + 1 < n_sub else jnp.zeros((1, HEAD), F32)
                else:
                    anchor = b[i * sub - 1:i * sub, :] if i > 0 else jnp.zeros((1, HEAD), F32)
                qi = (q[rows] * jnp.exp(b[rows] - anchor)).astype(BF16)
                ki = (k[spans[i]] * jnp.exp(jnp.minimum(anchor - b[spans[i]], MAX_EXP))).astype(BF16)
                scores[cj, h, i] = _dot_nt(qi, ki)
        for cj, h in units:
            cols = slice(h * HEAD, (h + 1) * HEAD)
            vb = vals[cj, h][1].astype(BF16)
            for i in range(n_sub):
                a = jnp.where(keeps[i], scores[cj, h, i], 0.0).astype(BF16)
                oi_ref[pl.ds(r0[cj] + i * sub, sub), cols] = _dot(a, vb[spans[i]])
        return carry

    lax.fori_loop(0, n_chunks // HGRN_CHUNKS_PER_ITER, local_body, 0)

    cur = [st[...] for st in states]
    for ci in range(n_chunks):
        cc = (n_chunks - 1 - ci) if reverse else ci
        for h in range(N_HEADS):
            cols = slice(h * HEAD, (h + 1) * HEAD)
            o_ref[cc * c:(cc + 1) * c, cols] = (oi_ref[cc * c:(cc + 1) * c, cols]
                                                + _dot_nt(qe_ref[cc * c:(cc + 1) * c, cols], cur[h].astype(BF16)))
        cur = [cur[h] * d_ref[cc * N_HEADS + h][0:1, :] + u_ref[cc * N_HEADS + h] for h in range(N_HEADS)]
    for h in range(N_HEADS):
        states[h][...] = cur[h]


def hgrn_scan(z, lb_logits, resets, *, reverse, layer, f_col, tt=TOK_TILE):
    n = z.shape[0]
    nt = n // tt
    n_units = tt // HGRN_CHUNK * N_HEADS
    order = (lambda i, r: (nt - 1 - i, 0)) if reverse else (lambda i, r: (i, 0))
    blk = lambda cb: (lambda i, r: (order(i, r)[0], cb))
    grid_spec = pltpu.PrefetchScalarGridSpec(
        num_scalar_prefetch=1,
        grid=(nt,),
        in_specs=[pl.BlockSpec((tt, HALF), blk(0)),
                  pl.BlockSpec((tt, HALF), blk(f_col)),
                  pl.BlockSpec((tt, HALF), blk(3)),
                  pl.BlockSpec(lb_logits.shape, lambda i, r: (0, 0))],
        out_specs=pl.BlockSpec((tt, HALF), order),
        scratch_shapes=[pltpu.VMEM((HEAD, HEAD), F32)] * N_HEADS + [
            pltpu.VMEM((tt, HALF), F32),
            pltpu.VMEM((tt, HALF), BF16),
            pltpu.VMEM((n_units, HEAD, HEAD), F32),
            pltpu.VMEM((n_units, 8, HEAD), F32)],
    )
    return pl.pallas_call(
        functools.partial(_hgrn_kernel, reverse=reverse, layer=layer, tt=tt),
        grid_spec=grid_spec,
        out_shape=jax.ShapeDtypeStruct((n, HALF), F32),
        compiler_params=_cparams("arbitrary"),
        name="hgrn_bwd" if reverse else "hgrn_fwd",
    )(resets, z, z, z, lb_logits)


def _mix_ab_kernel(xa_ref, xb_ref, of_ref, ob_ref, g_ref, u_ref, v_ref, onorm_ref, snorm_ref, wsp_ref, bsp_ref,
                   wout_ref, o_ref, *, tm, tiles_a):
    o = of_ref[...] + ob_ref[...]
    g = g_ref[...]
    parts = []
    for h in range(N_HEADS):
        cols = slice(h * HEAD, (h + 1) * HEAD)
        parts.append(_rms(o[:, cols], onorm_ref[:, cols]) * _silu(g[:, cols]))
    a = jnp.concatenate(parts, axis=1).astype(BF16)
    acc = _two_part_read(xa_ref, xb_ref, tiles_a) + _dot(a, wout_ref[0:HALF, :])

    u = _gelu_tanh(u_ref[...])
    vb = _rms(_gelu_tanh(v_ref[...]), snorm_ref[...]).astype(BF16)
    rows = []
    for c in range(tm // HEAD):
        mixed = []
        for gi in range(N_HEADS):
            vg = vb[c * HEAD:(c + 1) * HEAD, gi * HEAD:(gi + 1) * HEAD]
            mixed.append(_dot(wsp_ref[gi], vg) + bsp_ref[:, gi:gi + 1])
        rows.append(jnp.concatenate(mixed, axis=1))
    b = (u * jnp.concatenate(rows, axis=0)).astype(BF16)
    o_ref[...] = acc + _dot(b, wout_ref[HALF:2 * HALF, :])


def mix_ab(xa, xb, o_f, o_b, z, onorm, snorm, wsp_bf16, bsp_t, wout_bf16, tm=TOK_TILE):
    d = xa.shape[1]
    n = xa.shape[0] + xb.shape[0]
    tiles_a = xa.shape[0] // tm
    zblk = lambda cb: pl.BlockSpec((tm, HALF), lambda i: (i, cb))
    const = lambda shape: pl.BlockSpec(shape, lambda i: (0,) * len(shape))
    return pl.pallas_call(
        functools.partial(_mix_ab_kernel, tm=tm, tiles_a=tiles_a),
        grid=(n // tm,),
        in_specs=[*_two_part_specs(tm, d, tiles_a),
                  pl.BlockSpec((tm, HALF), lambda i: (i, 0)),
                  pl.BlockSpec((tm, HALF), lambda i: (i, 0)),
                  zblk(4), zblk(5), zblk(6),
                  const((1, HALF)), const((1, HALF)),
                  const((N_HEADS, HEAD, HEAD)), const((HEAD, N_HEADS)),
                  const((d, d))],
        out_specs=pl.BlockSpec((tm, d), lambda i: (i, 0)),
        out_shape=jax.ShapeDtypeStruct((n, d), F32),
        compiler_params=_cparams("parallel"),
        name="mix_ab",
    )(xa, xb, o_f, o_b, z, z, z, onorm.reshape(1, HALF), snorm.reshape(1, HALF), wsp_bf16, bsp_t, wout_bf16)


def _ffn_kernel(x_ref, g_ref, w1_ref, w3_ref, w2_ref, o_ref, *, tf):
    x = x_ref[...]
    h = _rms(x, g_ref[...]).astype(BF16)
    acc = x
    for c in range(w1_ref.shape[1] // tf):
        cols = slice(c * tf, (c + 1) * tf)
        act = (_silu(_dot(h, w1_ref[:, cols])) * _dot(h, w3_ref[:, cols])).astype(BF16)
        acc = acc + _dot(act, w2_ref[cols, :])
    o_ref[...] = acc


def ffn(x, g, w1, w3, w2, tm=TOK_TILE, tf=FFN_FTILE):
    n, d = x.shape
    f = w1.shape[1]
    const = lambda shape: pl.BlockSpec(shape, lambda i: (0,) * len(shape))
    return pl.pallas_call(
        functools.partial(_ffn_kernel, tf=tf),
        grid=(n // tm,),
        in_specs=[pl.BlockSpec((tm, d), lambda i: (i, 0)), const((1, d)),
                  const((d, f)), const((d, f)), const((f, d))],
        out_specs=pl.BlockSpec((tm, d), lambda i: (i, 0)),
        out_shape=jax.ShapeDtypeStruct((n, d), F32),
        compiler_params=_cparams("parallel"),
        name="ffn_dense",
    )(x, g.reshape(1, d), w1, w3, w2)


def _conv_kernel(first_ref, last_ref, a_ref, gt_ref, ap_ref, gp_ref, an_ref, gn_ref, w_ref, b_ref, lng_ref,
                 lnb_ref, o_ref, ypad_ref, *, ts, rb):
    i = pl.program_id(0)
    halo = CONV_HALO
    keep_prev = jnp.where(first_ref[i] == 1, 0.0, 1.0)
    keep_next = jnp.where(last_ref[i] == 1, 0.0, 1.0)
    ypad_ref[0:halo, :] = ap_ref[...] * jax.nn.sigmoid(gp_ref[...]) * keep_prev
    ypad_ref[halo:halo + ts, :] = a_ref[...] * jax.nn.sigmoid(gt_ref[...])
    ypad_ref[halo + ts:2 * halo + ts, :] = an_ref[...] * jax.nn.sigmoid(gn_ref[...]) * keep_next

    n_win = rb + 2 * halo

    def body(bi, carry):
        r0 = pl.multiple_of(bi * rb, rb)
        strips = []
        for s in range(HALF // HEAD):
            cols = slice(s * HEAD, (s + 1) * HEAD)
            win = ypad_ref[pl.ds(r0, n_win), cols]
            acc = jnp.zeros((rb, HEAD), F32) + b_ref[:, cols]
            for res in range(8):
                sh = win if res == 0 else pltpu.roll(win, n_win - res, axis=0)
                for j in range(CONV_K):
                    off = halo - CONV_PAD + j
                    if off % 8 == res:
                        base = off - res
                        acc = acc + w_ref[j:j + 1, cols] * sh[base:base + rb, :]
            strips.append(acc)
        acc = jnp.concatenate(strips, axis=1)
        mu = jnp.mean(acc, axis=-1, keepdims=True)
        xc = acc - mu
        var = jnp.mean(xc * xc, axis=-1, keepdims=True)
        y = xc * lax.rsqrt(var + EPS) * lng_ref[...] + lnb_ref[...]
        o_ref[pl.ds(r0, rb), :] = _silu(y).astype(o_ref.dtype)
        return carry

    lax.fori_loop(0, ts // rb, body, 0)


def conv_module(zc, firsts, lasts, w, b, ln_g, ln_b, ts=TOK_TILE, rb=64):
    n = zc.shape[0]
    nt = n // ts
    hb = ts // CONV_HALO
    n_hb = n // CONV_HALO
    main = lambda cb: pl.BlockSpec((ts, HALF), lambda i, f, l: (i, cb))
    prev = lambda cb: pl.BlockSpec((CONV_HALO, HALF), lambda i, f, l: (jnp.maximum(i * hb - 1, 0), cb))
    nxt = lambda cb: pl.BlockSpec((CONV_HALO, HALF), lambda i, f, l: (jnp.minimum((i + 1) * hb, n_hb - 1), cb))
    const = lambda shape: pl.BlockSpec(shape, lambda i, f, l: (0,) * len(shape))
    grid_spec = pltpu.PrefetchScalarGridSpec(
        num_scalar_prefetch=2,
        grid=(nt,),
        in_specs=[main(0), main(1), prev(0), prev(1), nxt(0), nxt(1),
                  const((CONV_K, HALF)), const((1, HALF)), const((1, HALF)), const((1, HALF))],
        out_specs=pl.BlockSpec((ts, HALF), lambda i, f, l: (i, 0)),
        scratch_shapes=[pltpu.VMEM((ts + 2 * CONV_HALO, HALF), F32)],
    )
    return pl.pallas_call(
        functools.partial(_conv_kernel, ts=ts, rb=rb),
        grid_spec=grid_spec,
        out_shape=jax.ShapeDtypeStruct((n, HALF), BF16),
        compiler_params=_cparams("parallel"),
        name="conv_module",
    )(firsts, lasts, zc, zc, zc, zc, zc, zc, w, b.reshape(1, HALF), ln_g.reshape(1, HALF), ln_b.reshape(1, HALF))


def _seq_dft_kernel(c_ref, s_ref, wv_ref, o_ref, *, scale):
    acc = _dot(c_ref[...], wv_ref[:, 0:HALF]) - _dot(s_ref[...], wv_ref[:, HALF:2 * HALF])
    o_ref[...] = (acc * scale).astype(o_ref.dtype)


def seq_dft(wv, cos_t, sin_t, row0, n_seq, seq_len):
    tmo = min(seq_len, max(256, SEQ_DFT_TILE_BYTES // (2 * seq_len)))
    nt = seq_len // tmo
    b0 = row0 // seq_len
    scale = 1.0 / float(np.sqrt(seq_len * HEAD))
    return pl.pallas_call(
        functools.partial(_seq_dft_kernel, scale=scale),
        grid=(n_seq, nt),
        in_specs=[pl.BlockSpec((tmo, seq_len), lambda b, i: (i, 0)),
                  pl.BlockSpec((tmo, seq_len), lambda b, i: (i, 0)),
                  pl.BlockSpec((seq_len, 2 * HALF), lambda b, i: (b0 + b, 0))],
        out_specs=pl.BlockSpec((tmo, HALF), lambda b, i: (b * nt + i, 0)),
        out_shape=jax.ShapeDtypeStruct((n_seq * seq_len, HALF), BF16),
        compiler_params=_cparams("parallel", "parallel"),
        name="seq_dft_%d" % seq_len,
    )(cos_t, sin_t, wv)


def _mix_cd_router_kernel(x_ref, c_ref, d_ref, wout_ref, g_ref, rwh_ref, rwl_ref, strict_ref, x3_ref, h_ref,
                          route_ref, cnt_ref, carry_ref, *, tm):
    @pl.when(pl.program_id(0) == 0)
    def _():
        carry_ref[...] = jnp.zeros_like(carry_ref)

    x3 = x_ref[...] + _dot(c_ref[...], wout_ref[0:HALF, :]) + _dot(d_ref[...], wout_ref[HALF:2 * HALF, :])
    x3_ref[...] = x3
    h = _rms(x3, g_ref[...])
    h_ref[...] = h

    h_hi = h.astype(BF16)
    h_lo = (h - h_hi.astype(F32)).astype(BF16)
    logits = _dot(h_hi, rwh_ref[...]) + (_dot(h_lo, rwh_ref[...]) + _dot(h_hi, rwl_ref[...]))
    lane = lax.broadcasted_iota(jnp.int32, (tm, ROUTE_LANES), 1)
    neg = jnp.float32(-jnp.inf)
    logits = jnp.where(lane < N_EXPERTS, logits, neg)
    m1 = jnp.max(logits, axis=-1, keepdims=True)
    i1 = jnp.min(jnp.where(logits == m1, lane, ROUTE_LANES), axis=-1, keepdims=True)
    rest = jnp.where(lane == i1, neg, logits)
    m2 = jnp.max(rest, axis=-1, keepdims=True)
    i2 = jnp.min(jnp.where(rest == m2, lane, ROUTE_LANES), axis=-1, keepdims=True)
    e2 = jnp.exp(m2 - m1)
    g1 = 1.0 / (1.0 + e2)
    g2 = e2 / (1.0 + e2)

    onehot = jnp.where((lane == i1) | (lane == i2), 1.0, 0.0)
    before = _dot(strict_ref[...], onehot.astype(BF16)) + carry_ref[0:1, :]
    r1 = jnp.sum(jnp.where(lane == i1, before, 0.0), axis=-1, keepdims=True)
    r2 = jnp.sum(jnp.where(lane == i2, before, 0.0), axis=-1, keepdims=True)
    total = carry_ref[0:1, :] + jnp.sum(onehot, axis=0, keepdims=True)
    carry_ref[...] = jnp.broadcast_to(total, carry_ref.shape)
    cnt_ref[...] = jnp.broadcast_to(total, cnt_ref.shape)

    route = jnp.where(lane == 0, i1.astype(F32), 0.0)
    route = jnp.where(lane == 1, i2.astype(F32), route)
    route = jnp.where(lane == 2, g1, route)
    route = jnp.where(lane == 3, g2, route)
    route = jnp.where(lane == 4, r1, route)
    route = jnp.where(lane == 5, r2, route)
    route_ref[...] = route


def mix_cd_router(x, c_out, d_out, wout_bf16, g, rw_hi, rw_lo, tm=TOK_TILE):
    n, d = x.shape
    const = lambda shape: pl.BlockSpec(shape, lambda i: (0,) * len(shape))
    tok = lambda w: pl.BlockSpec((tm, w), lambda i: (i, 0))
    strict = jnp.tril(jnp.ones((tm, tm), BF16), -1)
    return pl.pallas_call(
        functools.partial(_mix_cd_router_kernel, tm=tm),
        grid=(n // tm,),
        in_specs=[tok(d), tok(HALF), tok(HALF), const((d, d)), const((1, d)),
                  const((d, ROUTE_LANES)), const((d, ROUTE_LANES)), const((tm, tm))],
        out_specs=[tok(d), tok(d), tok(ROUTE_LANES), const((8, ROUTE_LANES))],
        out_shape=[jax.ShapeDtypeStruct((n, d), F32), jax.ShapeDtypeStruct((n, d), F32),
                   jax.ShapeDtypeStruct((n, ROUTE_LANES), F32), jax.ShapeDtypeStruct((8, ROUTE_LANES), F32)],
        scratch_shapes=[pltpu.VMEM((8, ROUTE_LANES), F32)],
        compiler_params=_cparams("arbitrary"),
        name="mix_cd_router",
    )(x, c_out, d_out, wout_bf16, g.reshape(1, d), rw_hi, rw_lo, strict)


def _dispatch_kernel(dest_ref, h_ref, xb_in_ref, xb_ref, sem, *, tm):
    del xb_in_ref

    def row_copy(r, slot):
        return pltpu.make_async_copy(h_ref.at[pl.ds(r, 1), :], xb_ref.at[pl.ds(slot, 1), :], sem)

    def issue(r, carry):
        row_copy(r, dest_ref[0, 0, 2 * r]).start()
        row_copy(r, dest_ref[0, 0, 2 * r + 1]).start()
        return carry

    lax.fori_loop(0, tm, issue, 0, unroll=ROW_DMA_UNROLL)

    for _ in range(2):
        pltpu.make_async_copy(h_ref, xb_ref.at[pl.ds(0, tm), :], sem).wait()


def moe_dispatch(h, dest, xb_init, tm=TOK_TILE):
    n, d = h.shape
    return pl.pallas_call(
        functools.partial(_dispatch_kernel, tm=tm),
        grid=(n // tm,),
        in_specs=[pl.BlockSpec((1, 1, 2 * tm), lambda i: (i, 0, 0), memory_space=pltpu.SMEM),
                  pl.BlockSpec((tm, d), lambda i: (i, 0)),
                  pl.BlockSpec(memory_space=pl.ANY)],
        out_specs=pl.BlockSpec(memory_space=pl.ANY),
        out_shape=jax.ShapeDtypeStruct(xb_init.shape, xb_init.dtype),
        scratch_shapes=[pltpu.SemaphoreType.DMA(())],
        input_output_aliases={2: 0},
        compiler_params=_cparams("arbitrary"),
        name="moe_dispatch",
    )(dest.reshape(n // tm, 1, 2 * tm), h, xb_init)


def _moe_ffn_kernel(be_ref, nu_ref, x_ref, w1_ref, w3_ref, w2_ref, o_ref, xbf_ref, *, rows_sub, cols_sub):
    del be_ref
    b = pl.program_id(0)
    f = pl.program_id(1)
    active = b < nu_ref[0]
    tmb = x_ref.shape[0]
    tf = w1_ref.shape[2]

    @pl.when(active & (f == 0))
    def _():
        xbf_ref[...] = x_ref[...].astype(BF16)

    def partial_sums(first):
        for r in range(tmb // rows_sub):
            rows = slice(r * rows_sub, (r + 1) * rows_sub)
            xr = xbf_ref[rows, :]
            acc = jnp.zeros((rows_sub, o_ref.shape[1]), F32) if first else o_ref[rows, :]
            for c in range(tf // cols_sub):
                cols = slice(c * cols_sub, (c + 1) * cols_sub)
                act = (_silu(_dot(xr, w1_ref[0, :, cols])) * _dot(xr, w3_ref[0, :, cols])).astype(BF16)
                acc = acc + _dot(act, w2_ref[0, cols, :])
            o_ref[rows, :] = acc

    @pl.when(active & (f == 0))
    def _():
        partial_sums(True)

    @pl.when(active & (f > 0))
    def _():
        partial_sums(False)

    @pl.when(jnp.logical_not(active) & (f == 0))
    def _():
        o_ref[...] = jnp.zeros_like(o_ref)


def moe_ffn(xb, block_e, n_used, w1, w3, w2, tmb=MOE_BLOCK, tf=MOE_FTILE):
    n_slots, d = xb.shape
    nb = n_slots // tmb
    nf = w1.shape[2] // tf

    def bclamp(b, nu):
        return jnp.minimum(b, nu[0] - 1)

    def fclamp(b, f, nu):
        return jnp.where(b < nu[0], f, nf - 1)

    grid_spec = pltpu.PrefetchScalarGridSpec(
        num_scalar_prefetch=2,
        grid=(nb, nf),
        in_specs=[pl.BlockSpec((tmb, d), lambda b, f, be, nu: (bclamp(b, nu), 0)),
                  pl.BlockSpec((1, d, tf), lambda b, f, be, nu: (be[bclamp(b, nu)], 0, fclamp(b, f, nu))),
                  pl.BlockSpec((1, d, tf), lambda b, f, be, nu: (be[bclamp(b, nu)], 0, fclamp(b, f, nu))),
                  pl.BlockSpec((1, tf, d), lambda b, f, be, nu: (be[bclamp(b, nu)], fclamp(b, f, nu), 0))],
        out_specs=pl.BlockSpec((tmb, d), lambda b, f, be, nu: (b, 0)),
        scratch_shapes=[pltpu.VMEM((tmb, d), BF16)],
    )
    return pl.pallas_call(
        functools.partial(_moe_ffn_kernel, rows_sub=MOE_ROWS_SUB, cols_sub=MOE_COLS_SUB),
        grid_spec=grid_spec,
        out_shape=jax.ShapeDtypeStruct((n_slots, d), F32),
        compiler_params=_cparams("arbitrary", "arbitrary"),
        name="moe_ffn",
    )(block_e, n_used, xb, w1, w3, w2)


def _combine_kernel(dest_ref, x_ref, route_ref, g_ref, yb_ref, oa_ref, ob_ref, y0_ref, y1_ref, sem, *, tm, tiles_a):
    def row_copy(slot, buf, r):
        return pltpu.make_async_copy(yb_ref.at[pl.ds(slot, 1), :], buf.at[pl.ds(r, 1), :], sem)

    def issue(r, carry):
        row_copy(dest_ref[0, 0, 2 * r], y0_ref, r).start()
        row_copy(dest_ref[0, 0, 2 * r + 1], y1_ref, r).start()
        return carry

    lax.fori_loop(0, tm, issue, 0, unroll=ROW_DMA_UNROLL)

    for buf in (y0_ref, y1_ref):
        pltpu.make_async_copy(yb_ref.at[pl.ds(0, tm), :], buf, sem).wait()

    route = route_ref[...]
    y = x_ref[...] + route[:, 2:3] * y0_ref[...] + route[:, 3:4] * y1_ref[...]
    res = _rms(y, g_ref[...])

    @pl.when(pl.program_id(0) < tiles_a)
    def _():
        oa_ref[...] = res

    @pl.when(pl.program_id(0) >= tiles_a)
    def _():
        ob_ref[...] = res


def moe_combine(x, route, dest, yb, g, n_a, tm=TOK_TILE):
    n, d = x.shape
    tiles_a = n_a // tm
    return pl.pallas_call(
        functools.partial(_combine_kernel, tm=tm, tiles_a=tiles_a),
        grid=(n // tm,),
        in_specs=[pl.BlockSpec((1, 1, 2 * tm), lambda i: (i, 0, 0), memory_space=pltpu.SMEM),
                  pl.BlockSpec((tm, d), lambda i: (i, 0)),
                  pl.BlockSpec((tm, ROUTE_LANES), lambda i: (i, 0)),
                  pl.BlockSpec((1, d), lambda i: (0, 0)),
                  pl.BlockSpec(memory_space=pl.ANY)],
        out_specs=list(_two_part_specs(tm, d, tiles_a)),
        out_shape=[jax.ShapeDtypeStruct((n_a, d), F32), jax.ShapeDtypeStruct((n - n_a, d), F32)],
        scratch_shapes=[pltpu.VMEM((tm, d), F32), pltpu.VMEM((tm, d), F32), pltpu.SemaphoreType.DMA(())],
        compiler_params=_cparams("arbitrary"),
        name="moe_combine",
    )(dest.reshape(n // tm, 1, 2 * tm), x, route, g.reshape(1, d), yb)


def _tile_flags(seq_lens, tile):
    firsts, lasts = [], []
    for length in seq_lens:
        k = length // tile
        firsts += [1] + [0] * (k - 1)
        lasts += [0] * (k - 1) + [1]
    return np.asarray(firsts, np.int32), np.asarray(lasts, np.int32)


def _dft_table_kernel(cphi_ref, sphi_ref, cth_ref, sth_ref, c_ref, s_ref):
    cphi, sphi = cphi_ref[...], sphi_ref[...]
    cth, sth = cth_ref[0], sth_ref[0]
    c_ref[...] = (cphi * cth - sphi * sth).astype(BF16)
    s_ref[...] = (sphi * cth + cphi * sth).astype(BF16)


def _angles(rows, t):
    k = lax.broadcasted_iota(jnp.int32, (rows.shape[0], t), 1)
    ang = ((rows[:, None] * k) % t).astype(F32) * (2.0 * np.pi / t)
    return jnp.cos(ang), jnp.sin(ang)


def _dft_tables(t):
    rows = min(DFT_GEN_ROWS, t)
    nt = t // rows
    cphi, sphi = _angles(jnp.arange(rows, dtype=jnp.int32), t)
    cth, sth = _angles(jnp.arange(nt, dtype=jnp.int32) * rows, t)
    tile = pl.BlockSpec((rows, t), lambda i: (0, 0))
    vec = pl.BlockSpec((1, 1, t), lambda i: (i, 0, 0))
    out = pl.BlockSpec((rows, t), lambda i: (i, 0))
    return pl.pallas_call(
        _dft_table_kernel,
        grid=(nt,),
        in_specs=[tile, tile, vec, vec],
        out_specs=[out, out],
        out_shape=[jax.ShapeDtypeStruct((t, t), BF16)] * 2,
        compiler_params=_cparams("parallel"),
        name="dft_table_%d" % t,
    )(cphi, sphi, cth.reshape(nt, 1, t), sth.reshape(nt, 1, t))


def kernel(x_prompt, x_sample, norm_mix, norm_ffn, norm_final, w_in_ab, hgrn_lb_logits, hgrn_out_norm, sgu_norm,
           sgu_w, sgu_b, w_out_ab, w_in_cd, conv_w, conv_b, conv_ln_g, conv_ln_b, w_out_cd, ffn_w1, ffn_w3, ffn_w2,
           router_w, moe_w1, moe_w3, moe_w2):
    bp, tp, d = x_prompt.shape
    bs, ts_, _ = x_sample.shape
    n_p, n_s = bp * tp, bs * ts_
    n = n_p + n_s
    seq_lens = [tp] * bp + [ts_] * bs
    tile = TOK_TILE
    firsts, lasts = _tile_flags(seq_lens, tile)
    depth = norm_mix.shape[0]
    assert depth == 2, "the layer schedule below is the two-layer trunk"
    xa, xb_in = x_prompt.reshape(n_p, d), x_sample.reshape(n_s, d)

    for layer in range(depth):
        j = layer // 2
        if layer % 2 == 0:
            z = norm_proj(xa, xb_in, norm_mix[layer], w_in_ab[j].astype(BF16))
            o_f = hgrn_scan(z, hgrn_lb_logits, jnp.asarray(firsts), reverse=False, layer=layer, f_col=1)
            o_b = hgrn_scan(z, hgrn_lb_logits, jnp.asarray(lasts[::-1].copy()), reverse=True, layer=layer, f_col=2)
            x = mix_ab(xa, xb_in, o_f, o_b, z, hgrn_out_norm[j], sgu_norm[j], sgu_w[j].astype(BF16), sgu_b[j].T,
                       w_out_ab[j].astype(BF16))
            x = ffn(x, norm_ffn[layer], ffn_w1[j].astype(BF16), ffn_w3[j].astype(BF16), ffn_w2[j].astype(BF16))
        else:
            ang = (lax.broadcasted_iota(jnp.int32, (HEAD, HEAD), 0) * lax.broadcasted_iota(jnp.int32, (HEAD, HEAD), 1)
                   % HEAD).astype(F32) * (2.0 * np.pi / HEAD)
            dft_c = jnp.concatenate([jnp.cos(ang), jnp.sin(ang)], axis=1).astype(BF16)
            zc, wv = norm_proj_cd(x, norm_mix[layer], w_in_cd[j].astype(BF16), dft_c)
            c_out = conv_module(zc, jnp.asarray(firsts), jnp.asarray(lasts), conv_w[j], conv_b[j], conv_ln_g[j],
                                conv_ln_b[j])
            cos_p, sin_p = _dft_tables(tp)
            cos_s, sin_s = _dft_tables(ts_)
            d_out = jnp.concatenate([seq_dft(wv, cos_p, sin_p, 0, bp, tp),
                                     seq_dft(wv, cos_s, sin_s, n_p, bs, ts_)], axis=0)

            rw = jnp.zeros((d, ROUTE_LANES), F32).at[:, :N_EXPERTS].set(router_w[j])
            rw_hi = rw.astype(BF16)
            rw_lo = (rw - rw_hi.astype(F32)).astype(BF16)
            x3, h, route, counts = mix_cd_router(x, c_out, d_out, w_out_cd[j].astype(BF16), norm_ffn[layer],
                                                 rw_hi, rw_lo)

            tmb = MOE_BLOCK
            n_blocks = (2 * n + tmb - 1) // tmb + N_EXPERTS
            cnt = counts[0, :N_EXPERTS].astype(jnp.int32)
            padded = (cnt + tmb - 1) // tmb * tmb
            pend = jnp.cumsum(padded)
            pstart = pend - padded
            e_idx = route[:, 0:2].astype(jnp.int32)
            dest = (pstart[e_idx] + route[:, 4:6].astype(jnp.int32)).reshape(2 * n)
            block_e = jnp.minimum(
                jnp.searchsorted(pend, jnp.arange(n_blocks, dtype=jnp.int32) * tmb, side='right'),
                N_EXPERTS - 1).astype(jnp.int32)
            n_used = (pend[-1] // tmb).astype(jnp.int32).reshape(1)

            xb = moe_dispatch(h, dest, jnp.zeros((n_blocks * tmb, d), F32))
            yb = moe_ffn(xb, block_e, n_used, moe_w1[j].astype(BF16), moe_w3[j].astype(BF16),
                         moe_w2[j].astype(BF16))
            out_p, out_s = moe_combine(x3, route, dest, yb, norm_final, n_p)

    return out_p.reshape(bp, tp, d), out_s.reshape(bs, ts_, d)
```

```python
import functools

import numpy as np
import jax
import jax.numpy as jnp
from jax import lax
from jax.experimental import pallas as pl
from jax.experimental.pallas import tpu as pltpu
from jax.experimental.pallas import tpu_sc as plsc

F32 = jnp.float32
BF16 = jnp.bfloat16
EPS = 1e-6

D_MODEL = 1024
HALF = 512
HEAD = 128
N_HEADS = 4
HGRN_CHUNK = 64
HGRN_SUB = 16
HGRN_CHUNKS_PER_ITER = 4
CONV_K = 31
CONV_PAD = 15
CONV_HALO = 16
N_EXPERTS = 8
ROUTE_LANES = 128
MAX_EXP = 80.0

VMEM_LIMIT_BYTES = 56 * 1024 * 1024

TOK_TILE = 512
MOE_BLOCK = 1024
MOE_FTILE = 1792
MOE_ROWS_SUB = 512
MOE_COLS_SUB = 256
FFN_FTILE = 256
ROW_DMA_UNROLL = 8
SC_CORES = 2
SC_SUBCORES = 16
SC_GATHER_CHUNK = 64
SEQ_DFT_TILE_BYTES = 4 * 1024 * 1024
DFT_GEN_ROWS = 128


def _dot(a, b):
    return jnp.dot(a, b, preferred_element_type=F32)


def _dot_nt(a, b):
    return lax.dot_general(a, b, (((1,), (1,)), ((), ())), preferred_element_type=F32)


def _cparams(*sem):
    return pltpu.CompilerParams(dimension_semantics=sem, vmem_limit_bytes=VMEM_LIMIT_BYTES)


def _rms(x, g):
    ms = jnp.mean(x * x, axis=-1, keepdims=True)
    return x * lax.rsqrt(ms + EPS) * g


def _silu(x):
    return x * jax.nn.sigmoid(x)


def _gelu_tanh(x):
    return 0.5 * x * (1.0 + jnp.tanh(0.7978845608028654 * (x + 0.044715 * (x * x * x))))


def _two_part_specs(tm, d, tiles_a):
    return (pl.BlockSpec((tm, d), lambda i: (jnp.minimum(i, tiles_a - 1), 0)),
            pl.BlockSpec((tm, d), lambda i: (jnp.maximum(i - tiles_a, 0), 0)))


def _two_part_read(xa_ref, xb_ref, tiles_a):
    return jnp.where(pl.program_id(0) < tiles_a, xa_ref[...], xb_ref[...])


def _norm_proj_kernel(xa_ref, xb_ref, g_ref, w_ref, o_ref, *, tn, tiles_a):
    h = _rms(_two_part_read(xa_ref, xb_ref, tiles_a), g_ref[...]).astype(BF16)
    for c in range(o_ref.shape[1] // tn):
        o_ref[:, c * tn:(c + 1) * tn] = _dot(h, w_ref[:, c * tn:(c + 1) * tn])


def norm_proj(xa, xb, g, w_bf16, tm=TOK_TILE, tn=512):
    d = xa.shape[1]
    n = xa.shape[0] + xb.shape[0]
    tiles_a = xa.shape[0] // tm
    nout = w_bf16.shape[1]
    return pl.pallas_call(
        functools.partial(_norm_proj_kernel, tn=tn, tiles_a=tiles_a),
        grid=(n // tm,),
        in_specs=[*_two_part_specs(tm, d, tiles_a),
                  pl.BlockSpec((1, d), lambda i: (0, 0)),
                  pl.BlockSpec((d, nout), lambda i: (0, 0))],
        out_specs=pl.BlockSpec((tm, nout), lambda i: (i, 0)),
        out_shape=jax.ShapeDtypeStruct((n, nout), F32),
        compiler_params=_cparams("parallel"),
        name="norm_proj_ab",
    )(xa, xb, g.reshape(1, d), w_bf16)


def _norm_proj_cd_kernel(x_ref, g_ref, w_ref, dft_ref, zc_ref, wv_ref):
    h = _rms(x_ref[...], g_ref[...]).astype(BF16)
    for c in range(2):
        zc_ref[:, c * HALF:(c + 1) * HALF] = _dot(h, w_ref[:, c * HALF:(c + 1) * HALF])
    d = _dot(h, w_ref[:, 2 * HALF:3 * HALF]).astype(BF16)
    for g in range(N_HEADS):
        r = _dot(d[:, g * HEAD:(g + 1) * HEAD], dft_ref[...])
        wv_ref[:, g * HEAD:(g + 1) * HEAD] = r[:, :HEAD].astype(BF16)
        wv_ref[:, HALF + g * HEAD:HALF + (g + 1) * HEAD] = r[:, HEAD:].astype(BF16)


def norm_proj_cd(x, g, w_bf16, dft_c, tm=TOK_TILE):
    n, d = x.shape
    return pl.pallas_call(
        _norm_proj_cd_kernel,
        grid=(n // tm,),
        in_specs=[pl.BlockSpec((tm, d), lambda i: (i, 0)),
                  pl.BlockSpec((1, d), lambda i: (0, 0)),
                  pl.BlockSpec((d, 3 * HALF), lambda i: (0, 0)),
                  pl.BlockSpec((HEAD, 2 * HEAD), lambda i: (0, 0))],
        out_specs=[pl.BlockSpec((tm, 2 * HALF), lambda i: (i, 0)),
                   pl.BlockSpec((tm, 2 * HALF), lambda i: (i, 0))],
        out_shape=[jax.ShapeDtypeStruct((n, 2 * HALF), F32),
                   jax.ShapeDtypeStruct((n, 2 * HALF), BF16)],
        compiler_params=_cparams("parallel"),
        name="norm_proj_cd",
    )(x, g.reshape(1, d), w_bf16, dft_c)


def _hgrn_kernel(reset_ref, q_ref, f_ref, v_ref, lbl_ref, o_ref, st0, st1, st2, st3, oi_ref, qe_ref, u_ref, d_ref,
                 *, reverse, layer, tt):
    c, sub = HGRN_CHUNK, HGRN_SUB
    n_sub = c // sub
    n_chunks = tt // c
    states = (st0, st1, st2, st3)

    @pl.when(reset_ref[pl.program_id(0)] == 1)
    def _():
        for st in states:
            st[...] = jnp.zeros_like(st)

    lg = lbl_ref[...]
    e = jnp.exp(lg - jnp.max(lg, axis=0, keepdims=True))
    p = e / jnp.sum(e, axis=0, keepdims=True)
    lb_all = p[0:1, :]
    for r in range(1, layer + 1):
        lb_all = lb_all + p[r:r + 1, :]

    row = lax.broadcasted_iota(jnp.int32, (c, c), 0)
    col = lax.broadcasted_iota(jnp.int32, (c, c), 1)
    tri = jnp.where((col >= row) if reverse else (col <= row), 1.0, 0.0).astype(BF16)
    spans, keeps = [], []
    for i in range(n_sub):
        span = slice(i * sub, c) if reverse else slice(0, (i + 1) * sub)
        n_span = span.stop - span.start
        tr = lax.broadcasted_iota(jnp.int32, (sub, n_span), 0)
        sc = lax.broadcasted_iota(jnp.int32, (sub, n_span), 1)
        spans.append(span)
        keeps.append((sc >= tr) if reverse else (sc <= tr + i * sub))

    units = [(cj, h) for cj in range(HGRN_CHUNKS_PER_ITER) for h in range(N_HEADS)]

    def local_body(it, carry):
        ci = [it * HGRN_CHUNKS_PER_ITER + cj for cj in range(HGRN_CHUNKS_PER_ITER)]
        r0 = [pl.multiple_of(x * c, c) for x in ci]
        vals = {}
        for cj, h in units:
            cols = slice(h * HEAD, (h + 1) * HEAD)
            lb = lb_all[:, cols]
            q = _silu(q_ref[pl.ds(r0[cj], c), cols])
            v = v_ref[pl.ds(r0[cj], c), cols]
            f = lb + (1.0 - lb) * jax.nn.sigmoid(f_ref[pl.ds(r0[cj], c), cols])
            lf = jnp.log(f)
            lf_hi = lf.astype(BF16)
            lf_lo = (lf - lf_hi.astype(F32)).astype(BF16)
            vals[cj, h] = (q, v, 1.0 - f, _dot(tri, lf_hi) + _dot(tri, lf_lo))
        scores = {}
        for cj, h in units:
            cols = slice(h * HEAD, (h + 1) * HEAD)
            q, v, k, b = vals[cj, h]
            qe_ref[pl.ds(r0[cj], c), cols] = (q * jnp.exp(b)).astype(BF16)
            edge = b[0:1, :] if reverse else b[c - 1:c, :]
            ks = (k * jnp.exp(edge - b)).astype(BF16)
            u_ref[ci[cj] * N_HEADS + h] = _dot(v.T.astype(BF16), ks)
            d_ref[ci[cj] * N_HEADS + h] = jnp.broadcast_to(jnp.exp(edge), (8, HEAD))
            for i in range(n_sub):
                rows = slice(i * sub, (i + 1) * sub)
                if reverse:
                    anchor = b[(i + 1) * sub:(i + 1) * sub + 1, :] if i + 1 < n_sub else jnp.zeros((1, HEAD), F32)
                else:
                    anchor = b[i * sub - 1:i * sub, :] if i > 0 else jnp.zeros((1, HEAD), F32)
                qi = (q[rows] * jnp.exp(b[rows] - anchor)).astype(BF16)
                ki = (k[spans[i]] * jnp.exp(jnp.minimum(anchor - b[spans[i]], MAX_EXP))).astype(BF16)
                scores[cj, h, i] = _dot_nt(qi, ki)
        for cj, h in units:
            cols = slice(h * HEAD, (h + 1) * HEAD)
            vb = vals[cj, h][1].astype(BF16)
            for i in range(n_sub):
                a = jnp.where(keeps[i], scores[cj, h, i], 0.0).astype(BF16)
                oi_ref[pl.ds(r0[cj] + i * sub, sub), cols] = _dot(a, vb[spans[i]])
        return carry

    lax.fori_loop(0, n_chunks // HGRN_CHUNKS_PER_ITER, local_body, 0)

    cur = [st[...] for st in states]
    for ci in range(n_chunks):
        cc = (n_chunks - 1 - ci) if reverse else ci
        for h in range(N_HEADS):
            cols = slice(h * HEAD, (h + 1) * HEAD)
            o_ref[cc * c:(cc + 1) * c, cols] = (oi_ref[cc * c:(cc + 1) * c, cols]
                                                + _dot_nt(qe_ref[cc * c:(cc + 1) * c, cols], cur[h].astype(BF16)))
        cur = [cur[h] * d_ref[cc * N_HEADS + h][0:1, :] + u_ref[cc * N_HEADS + h] for h in range(N_HEADS)]
    for h in range(N_HEADS):
        states[h][...] = cur[h]


def hgrn_scan(z, lb_logits, resets, *, reverse, layer, f_col, tt=TOK_TILE):
    n = z.shape[0]
    nt = n // tt
    n_units = tt // HGRN_CHUNK * N_HEADS
    order = (lambda i, r: (nt - 1 - i, 0)) if reverse else (lambda i, r: (i, 0))
    blk = lambda cb: (lambda i, r: (order(i, r)[0], cb))
    grid_spec = pltpu.PrefetchScalarGridSpec(
        num_scalar_prefetch=1,
        grid=(nt,),
        in_specs=[pl.BlockSpec((tt, HALF), blk(0)),
                  pl.BlockSpec((tt, HALF), blk(f_col)),
                  pl.BlockSpec((tt, HALF), blk(3)),
                  pl.BlockSpec(lb_logits.shape, lambda i, r: (0, 0))],
        out_specs=pl.BlockSpec((tt, HALF), order),
        scratch_shapes=[pltpu.VMEM((HEAD, HEAD), F32)] * N_HEADS + [
            pltpu.VMEM((tt, HALF), F32),
            pltpu.VMEM((tt, HALF), BF16),
            pltpu.VMEM((n_units, HEAD, HEAD), F32),
            pltpu.VMEM((n_units, 8, HEAD), F32)],
    )
    return pl.pallas_call(
        functools.partial(_hgrn_kernel, reverse=reverse, layer=layer, tt=tt),
        grid_spec=grid_spec,
        out_shape=jax.ShapeDtypeStruct((n, HALF), F32),
        compiler_params=_cparams("arbitrary"),
        name="hgrn_bwd" if reverse else "hgrn_fwd",
    )(resets, z, z, z, lb_logits)


def _mix_ab_kernel(xa_ref, xb_ref, of_ref, ob_ref, g_ref, u_ref, v_ref, onorm_ref, snorm_ref, wsp_ref, bsp_ref,
                   wout_ref, o_ref, *, tm, tiles_a):
    o = of_ref[...] + ob_ref[...]
    g = g_ref[...]
    parts = []
    for h in range(N_HEADS):
        cols = slice(h * HEAD, (h + 1) * HEAD)
        parts.append(_rms(o[:, cols], onorm_ref[:, cols]) * _silu(g[:, cols]))
    a = jnp.concatenate(parts, axis=1).astype(BF16)
    acc = _two_part_read(xa_ref, xb_ref, tiles_a) + _dot(a, wout_ref[0:HALF, :])

    u = _gelu_tanh(u_ref[...])
    vb = _rms(_gelu_tanh(v_ref[...]), snorm_ref[...]).astype(BF16)
    rows = []
    for c in range(tm // HEAD):
        mixed = []
        for gi in range(N_HEADS):
            vg = vb[c * HEAD:(c + 1) * HEAD, gi * HEAD:(gi + 1) * HEAD]
            mixed.append(_dot(wsp_ref[gi], vg) + bsp_ref[:, gi:gi + 1])
        rows.append(jnp.concatenate(mixed, axis=1))
    b = (u * jnp.concatenate(rows, axis=0)).astype(BF16)
    o_ref[...] = acc + _dot(b, wout_ref[HALF:2 * HALF, :])


def mix_ab(xa, xb, o_f, o_b, z, onorm, snorm, wsp_bf16, bsp_t, wout_bf16, tm=TOK_TILE):
    d = xa.shape[1]
    n = xa.shape[0] + xb.shape[0]
    tiles_a = xa.shape[0] // tm
    zblk = lambda cb: pl.BlockSpec((tm, HALF), lambda i: (i, cb))
    const = lambda shape: pl.BlockSpec(shape, lambda i: (0,) * len(shape))
    return pl.pallas_call(
        functools.partial(_mix_ab_kernel, tm=tm, tiles_a=tiles_a),
        grid=(n // tm,),
        in_specs=[*_two_part_specs(tm, d, tiles_a),
                  pl.BlockSpec((tm, HALF), lambda i: (i, 0)),
                  pl.BlockSpec((tm, HALF), lambda i: (i, 0)),
                  zblk(4), zblk(5), zblk(6),
                  const((1, HALF)), const((1, HALF)),
                  const((N_HEADS, HEAD, HEAD)), const((HEAD, N_HEADS)),
                  const((d, d))],
        out_specs=pl.BlockSpec((tm, d), lambda i: (i, 0)),
        out_shape=jax.ShapeDtypeStruct((n, d), F32),
        compiler_params=_cparams("parallel"),
        name="mix_ab",
    )(xa, xb, o_f, o_b, z, z, z, onorm.reshape(1, HALF), snorm.reshape(1, HALF), wsp_bf16, bsp_t, wout_bf16)


def _ffn_kernel(x_ref, g_ref, w1_ref, w3_ref, w2_ref, o_ref, *, tf):
    x = x_ref[...]
    h = _rms(x, g_ref[...]).astype(BF16)
    acc = x
    for c in range(w1_ref.shape[1] // tf):
        cols = slice(c * tf, (c + 1) * tf)
        act = (_silu(_dot(h, w1_ref[:, cols])) * _dot(h, w3_ref[:, cols])).astype(BF16)
        acc = acc + _dot(act, w2_ref[cols, :])
    o_ref[...] = acc


def ffn(x, g, w1, w3, w2, tm=TOK_TILE, tf=FFN_FTILE):
    n, d = x.shape
    f = w1.shape[1]
    const = lambda shape: pl.BlockSpec(shape, lambda i: (0,) * len(shape))
    return pl.pallas_call(
        functools.partial(_ffn_kernel, tf=tf),
        grid=(n // tm,),
        in_specs=[pl.BlockSpec((tm, d), lambda i: (i, 0)), const((1, d)),
                  const((d, f)), const((d, f)), const((f, d))],
        out_specs=pl.BlockSpec((tm, d), lambda i: (i, 0)),
        out_shape=jax.ShapeDtypeStruct((n, d), F32),
        compiler_params=_cparams("parallel"),
        name="ffn_dense",
    )(x, g.reshape(1, d), w1, w3, w2)


def _conv_kernel(first_ref, last_ref, a_ref, gt_ref, ap_ref, gp_ref, an_ref, gn_ref, w_ref, b_ref, lng_ref,
                 lnb_ref, o_ref, ypad_ref, *, ts, rb):
    i = pl.program_id(0)
    halo = CONV_HALO
    keep_prev = jnp.where(first_ref[i] == 1, 0.0, 1.0)
    keep_next = jnp.where(last_ref[i] == 1, 0.0, 1.0)
    ypad_ref[0:halo, :] = ap_ref[...] * jax.nn.sigmoid(gp_ref[...]) * keep_prev
    ypad_ref[halo:halo + ts, :] = a_ref[...] * jax.nn.sigmoid(gt_ref[...])
    ypad_ref[halo + ts:2 * halo + ts, :] = an_ref[...] * jax.nn.sigmoid(gn_ref[...]) * keep_next

    n_win = rb + 2 * halo

    def body(bi, carry):
        r0 = pl.multiple_of(bi * rb, rb)
        strips = []
        for s in range(HALF // HEAD):
            cols = slice(s * HEAD, (s + 1) * HEAD)
            win = ypad_ref[pl.ds(r0, n_win), cols]
            acc = jnp.zeros((rb, HEAD), F32) + b_ref[:, cols]
            for res in range(8):
                sh = win if res == 0 else pltpu.roll(win, n_win - res, axis=0)
                for j in range(CONV_K):
                    off = halo - CONV_PAD + j
                    if off % 8 == res:
                        base = off - res
                        acc = acc + w_ref[j:j + 1, cols] * sh[base:base + rb, :]
            strips.append(acc)
        acc = jnp.concatenate(strips, axis=1)
        mu = jnp.mean(acc, axis=-1, keepdims=True)
        xc = acc - mu
        var = jnp.mean(xc * xc, axis=-1, keepdims=True)
        y = xc * lax.rsqrt(var + EPS) * lng_ref[...] + lnb_ref[...]
        o_ref[pl.ds(r0, rb), :] = _silu(y).astype(o_ref.dtype)
        return carry

    lax.fori_loop(0, ts // rb, body, 0)


def conv_module(zc, firsts, lasts, w, b, ln_g, ln_b, ts=TOK_TILE, rb=64):
    n = zc.shape[0]
    nt = n // ts
    hb = ts // CONV_HALO
    n_hb = n // CONV_HALO
    main = lambda cb: pl.BlockSpec((ts, HALF), lambda i, f, l: (i, cb))
    prev = lambda cb: pl.BlockSpec((CONV_HALO, HALF), lambda i, f, l: (jnp.maximum(i * hb - 1, 0), cb))
    nxt = lambda cb: pl.BlockSpec((CONV_HALO, HALF), lambda i, f, l: (jnp.minimum((i + 1) * hb, n_hb - 1), cb))
    const = lambda shape: pl.BlockSpec(shape, lambda i, f, l: (0,) * len(shape))
    grid_spec = pltpu.PrefetchScalarGridSpec(
        num_scalar_prefetch=2,
        grid=(nt,),
        in_specs=[main(0), main(1), prev(0), prev(1), nxt(0), nxt(1),
                  const((CONV_K, HALF)), const((1, HALF)), const((1, HALF)), const((1, HALF))],
        out_specs=pl.BlockSpec((ts, HALF), lambda i, f, l: (i, 0)),
        scratch_shapes=[pltpu.VMEM((ts + 2 * CONV_HALO, HALF), F32)],
    )
    return pl.pallas_call(
        functools.partial(_conv_kernel, ts=ts, rb=rb),
        grid_spec=grid_spec,
        out_shape=jax.ShapeDtypeStruct((n, HALF), BF16),
        compiler_params=_cparams("parallel"),
        name="conv_module",
    )(firsts, lasts, zc, zc, zc, zc, zc, zc, w, b.reshape(1, HALF), ln_g.reshape(1, HALF), ln_b.reshape(1, HALF))


def _seq_dft_kernel(c_ref, s_ref, wv_ref, o_ref, *, scale):
    acc = _dot(c_ref[...], wv_ref[:, 0:HALF]) - _dot(s_ref[...], wv_ref[:, HALF:2 * HALF])
    o_ref[...] = (acc * scale).astype(o_ref.dtype)


def seq_dft(wv, cos_t, sin_t, row0, n_seq, seq_len):
    tmo = min(seq_len, max(256, SEQ_DFT_TILE_BYTES // (2 * seq_len)))
    nt = seq_len // tmo
    b0 = row0 // seq_len
    scale = 1.0 / float(np.sqrt(seq_len * HEAD))
    return pl.pallas_call(
        functools.partial(_seq_dft_kernel, scale=scale),
        grid=(n_seq, nt),
        in_specs=[pl.BlockSpec((tmo, seq_len), lambda b, i: (i, 0)),
                  pl.BlockSpec((tmo, seq_len), lambda b, i: (i, 0)),
                  pl.BlockSpec((seq_len, 2 * HALF), lambda b, i: (b0 + b, 0))],
        out_specs=pl.BlockSpec((tmo, HALF), lambda b, i: (b * nt + i, 0)),
        out_shape=jax.ShapeDtypeStruct((n_seq * seq_len, HALF), BF16),
        compiler_params=_cparams("parallel", "parallel"),
        name="seq_dft_%d" % seq_len,
    )(cos_t, sin_t, wv)


def _mix_cd_router_kernel(x_ref, c_ref, d_ref, wout_ref, g_ref, rwh_ref, rwl_ref, strict_ref, x3_ref, h_ref,
                          route_ref, cnt_ref, carry_ref, *, tm):
    @pl.when(pl.program_id(0) == 0)
    def _():
        carry_ref[...] = jnp.zeros_like(carry_ref)

    x3 = x_ref[...] + _dot(c_ref[...], wout_ref[0:HALF, :]) + _dot(d_ref[...], wout_ref[HALF:2 * HALF, :])
    x3_ref[...] = x3
    h = _rms(x3, g_ref[...])
    h_ref[...] = h

    h_hi = h.astype(BF16)
    h_lo = (h - h_hi.astype(F32)).astype(BF16)
    logits = _dot(h_hi, rwh_ref[...]) + (_dot(h_lo, rwh_ref[...]) + _dot(h_hi, rwl_ref[...]))
    lane = lax.broadcasted_iota(jnp.int32, (tm, ROUTE_LANES), 1)
    neg = jnp.float32(-jnp.inf)
    logits = jnp.where(lane < N_EXPERTS, logits, neg)
    m1 = jnp.max(logits, axis=-1, keepdims=True)
    i1 = jnp.min(jnp.where(logits == m1, lane, ROUTE_LANES), axis=-1, keepdims=True)
    rest = jnp.where(lane == i1, neg, logits)
    m2 = jnp.max(rest, axis=-1, keepdims=True)
    i2 = jnp.min(jnp.where(rest == m2, lane, ROUTE_LANES), axis=-1, keepdims=True)
    e2 = jnp.exp(m2 - m1)
    g1 = 1.0 / (1.0 + e2)
    g2 = e2 / (1.0 + e2)

    onehot = jnp.where((lane == i1) | (lane == i2), 1.0, 0.0)
    before = _dot(strict_ref[...], onehot.astype(BF16)) + carry_ref[0:1, :]
    r1 = jnp.sum(jnp.where(lane == i1, before, 0.0), axis=-1, keepdims=True)
    r2 = jnp.sum(jnp.where(lane == i2, before, 0.0), axis=-1, keepdims=True)
    total = carry_ref[0:1, :] + jnp.sum(onehot, axis=0, keepdims=True)
    carry_ref[...] = jnp.broadcast_to(total, carry_ref.shape)
    cnt_ref[...] = jnp.broadcast_to(total, cnt_ref.shape)

    route = jnp.where(lane == 0, i1.astype(F32), 0.0)
    route = jnp.where(lane == 1, i2.astype(F32), route)
    route = jnp.where(lane == 2, g1, route)
    route = jnp.where(lane == 3, g2, route)
    route = jnp.where(lane == 4, r1, route)
    route = jnp.where(lane == 5, r2, route)
    route_ref[...] = route


def mix_cd_router(x, c_out, d_out, wout_bf16, g, rw_hi, rw_lo, tm=TOK_TILE):
    n, d = x.shape
    const = lambda shape: pl.BlockSpec(shape, lambda i: (0,) * len(shape))
    tok = lambda w: pl.BlockSpec((tm, w), lambda i: (i, 0))
    strict = jnp.tril(jnp.ones((tm, tm), BF16), -1)
    return pl.pallas_call(
        functools.partial(_mix_cd_router_kernel, tm=tm),
        grid=(n // tm,),
        in_specs=[tok(d), tok(HALF), tok(HALF), const((d, d)), const((1, d)),
                  const((d, ROUTE_LANES)), const((d, ROUTE_LANES)), const((tm, tm))],
        out_specs=[tok(d), tok(d), tok(ROUTE_LANES), const((8, ROUTE_LANES))],
        out_shape=[jax.ShapeDtypeStruct((n, d), F32), jax.ShapeDtypeStruct((n, d), F32),
                   jax.ShapeDtypeStruct((n, ROUTE_LANES), F32), jax.ShapeDtypeStruct((8, ROUTE_LANES), F32)],
        scratch_shapes=[pltpu.VMEM((8, ROUTE_LANES), F32)],
        compiler_params=_cparams("arbitrary"),
        name="mix_cd_router",
    )(x, c_out, d_out, wout_bf16, g.reshape(1, d), rw_hi, rw_lo, strict)


def _dispatch_kernel(dest_ref, h_ref, xb_in_ref, xb_ref, sem, *, tm):
    del xb_in_ref

    def row_copy(r, slot):
        return pltpu.make_async_copy(h_ref.at[pl.ds(r, 1), :], xb_ref.at[pl.ds(slot, 1), :], sem)

    def issue(r, carry):
        row_copy(r, dest_ref[0, 0, 2 * r]).start()
        row_copy(r, dest_ref[0, 0, 2 * r + 1]).start()
        return carry

    lax.fori_loop(0, tm, issue, 0, unroll=ROW_DMA_UNROLL)

    for _ in range(2):
        pltpu.make_async_copy(h_ref, xb_ref.at[pl.ds(0, tm), :], sem).wait()


def moe_dispatch(h, dest, xb_init, tm=TOK_TILE):
    n, d = h.shape
    return pl.pallas_call(
        functools.partial(_dispatch_kernel, tm=tm),
        grid=(n // tm,),
        in_specs=[pl.BlockSpec((1, 1, 2 * tm), lambda i: (i, 0, 0), memory_space=pltpu.SMEM),
                  pl.BlockSpec((tm, d), lambda i: (i, 0)),
                  pl.BlockSpec(memory_space=pl.ANY)],
        out_specs=pl.BlockSpec(memory_space=pl.ANY),
        out_shape=jax.ShapeDtypeStruct(xb_init.shape, xb_init.dtype),
        scratch_shapes=[pltpu.SemaphoreType.DMA(())],
        input_output_aliases={2: 0},
        compiler_params=_cparams("arbitrary"),
        name="moe_dispatch",
    )(dest.reshape(n // tm, 1, 2 * tm), h, xb_init)


def _moe_ffn_kernel(be_ref, nu_ref, x_ref, w1_ref, w3_ref, w2_ref, o_ref, xbf_ref, *, rows_sub, cols_sub):
    del be_ref
    b = pl.program_id(0)
    f = pl.program_id(1)
    active = b < nu_ref[0]
    tmb = x_ref.shape[0]
    tf = w1_ref.shape[2]

    @pl.when(active & (f == 0))
    def _():
        xbf_ref[...] = x_ref[...].astype(BF16)

    def partial_sums(first):
        for r in range(tmb // rows_sub):
            rows = slice(r * rows_sub, (r + 1) * rows_sub)
            xr = xbf_ref[rows, :]
            acc = jnp.zeros((rows_sub, o_ref.shape[1]), F32) if first else o_ref[rows, :]
            for c in range(tf // cols_sub):
                cols = slice(c * cols_sub, (c + 1) * cols_sub)
                act = (_silu(_dot(xr, w1_ref[0, :, cols])) * _dot(xr, w3_ref[0, :, cols])).astype(BF16)
                acc = acc + _dot(act, w2_ref[0, cols, :])
            o_ref[rows, :] = acc

    @pl.when(active & (f == 0))
    def _():
        partial_sums(True)

    @pl.when(active & (f > 0))
    def _():
        partial_sums(False)

    @pl.when(jnp.logical_not(active) & (f == 0))
    def _():
        o_ref[...] = jnp.zeros_like(o_ref)


def moe_ffn(xb, block_e, n_used, w1, w3, w2, tmb=MOE_BLOCK, tf=MOE_FTILE):
    n_slots, d = xb.shape
    nb = n_slots // tmb
    nf = w1.shape[2] // tf

    def bclamp(b, nu):
        return jnp.minimum(b, nu[0] - 1)

    def fclamp(b, f, nu):
        return jnp.where(b < nu[0], f, nf - 1)

    grid_spec = pltpu.PrefetchScalarGridSpec(
        num_scalar_prefetch=2,
        grid=(nb, nf),
        in_specs=[pl.BlockSpec((tmb, d), lambda b, f, be, nu: (bclamp(b, nu), 0)),
                  pl.BlockSpec((1, d, tf), lambda b, f, be, nu: (be[bclamp(b, nu)], 0, fclamp(b, f, nu))),
                  pl.BlockSpec((1, d, tf), lambda b, f, be, nu: (be[bclamp(b, nu)], 0, fclamp(b, f, nu))),
                  pl.BlockSpec((1, tf, d), lambda b, f, be, nu: (be[bclamp(b, nu)], fclamp(b, f, nu), 0))],
        out_specs=pl.BlockSpec((tmb, d), lambda b, f, be, nu: (b, 0)),
        scratch_shapes=[pltpu.VMEM((tmb, d), BF16)],
    )
    return pl.pallas_call(
        functools.partial(_moe_ffn_kernel, rows_sub=MOE_ROWS_SUB, cols_sub=MOE_COLS_SUB),
        grid_spec=grid_spec,
        out_shape=jax.ShapeDtypeStruct((n_slots, d), F32),
        compiler_params=_cparams("arbitrary", "arbitrary"),
        name="moe_ffn",
    )(block_e, n_used, xb, w1, w3, w2)


def _combine_kernel(dest_ref, x_ref, route_ref, g_ref, yb_ref, oa_ref, ob_ref, y0_ref, y1_ref, sem, *, tm, tiles_a):
    def row_copy(slot, buf, r):
        return pltpu.make_async_copy(yb_ref.at[pl.ds(slot, 1), :], buf.at[pl.ds(r, 1), :], sem)

    def issue(r, carry):
        row_copy(dest_ref[0, 0, 2 * r], y0_ref, r).start()
        row_copy(dest_ref[0, 0, 2 * r + 1], y1_ref, r).start()
        return carry

    lax.fori_loop(0, tm, issue, 0, unroll=ROW_DMA_UNROLL)

    for buf in (y0_ref, y1_ref):
        pltpu.make_async_copy(yb_ref.at[pl.ds(0, tm), :], buf, sem).wait()

    route = route_ref[...]
    y = x_ref[...] + route[:, 2:3] * y0_ref[...] + route[:, 3:4] * y1_ref[...]
    res = _rms(y, g_ref[...])

    @pl.when(pl.program_id(0) < tiles_a)
    def _():
        oa_ref[...] = res

    @pl.when(pl.program_id(0) >= tiles_a)
    def _():
        ob_ref[...] = res


def moe_combine(x, route, dest, yb, g, n_a, tm=TOK_TILE):
    n, d = x.shape
    tiles_a = n_a // tm
    return pl.pallas_call(
        functools.partial(_combine_kernel, tm=tm, tiles_a=tiles_a),
        grid=(n // tm,),
        in_specs=[pl.BlockSpec((1, 1, 2 * tm), lambda i: (i, 0, 0), memory_space=pltpu.SMEM),
                  pl.BlockSpec((tm, d), lambda i: (i, 0)),
                  pl.BlockSpec((tm, ROUTE_LANES), lambda i: (i, 0)),
                  pl.BlockSpec((1, d), lambda i: (0, 0)),
                  pl.BlockSpec(memory_space=pl.ANY)],
        out_specs=list(_two_part_specs(tm, d, tiles_a)),
        out_shape=[jax.ShapeDtypeStruct((n_a, d), F32), jax.ShapeDtypeStruct((n - n_a, d), F32)],
        scratch_shapes=[pltpu.VMEM((tm, d), F32), pltpu.VMEM((tm, d), F32), pltpu.SemaphoreType.DMA(())],
        compiler_params=_cparams("arbitrary"),
        name="moe_combine",
    )(dest.reshape(n // tm, 1, 2 * tm), x, route, g.reshape(1, d), yb)


def sc_row_gather(table, idx, chunk=SC_GATHER_CHUNK):
    b, d = idx.shape[0], table.shape[1]
    n_workers = SC_CORES * SC_SUBCORES
    per_worker = b // n_workers
    assert per_worker * n_workers == b and per_worker % chunk == 0 and chunk % 8 == 0
    mesh = plsc.VectorSubcoreMesh(core_axis_name="c", subcore_axis_name="s")

    @functools.partial(
        pl.kernel, mesh=mesh,
        out_type=jax.ShapeDtypeStruct((b, d), table.dtype),
        scratch_types=[pltpu.VMEM((chunk,), jnp.int32), pltpu.VMEM((chunk, d), table.dtype),
                       pltpu.SemaphoreType.DMA],
    )
    def gather(table_hbm, idx_hbm, out_hbm, idx_v, rows_v, sem):
        worker = lax.axis_index("s") * SC_CORES + lax.axis_index("c")
        base = worker * per_worker

        @pl.loop(0, per_worker // chunk)
        def _(c):
            off = base + c * chunk
            pltpu.sync_copy(idx_hbm.at[pl.ds(off, chunk)], idx_v)
            pltpu.async_copy(table_hbm.at[idx_v], rows_v, sem).wait()
            pltpu.sync_copy(rows_v, out_hbm.at[pl.ds(off, chunk)])

    return gather(table, idx)


def _finish_kernel(x_ref, route_ref, g_ref, y_ref, oa_ref, ob_ref, *, tiles_a):
    d = x_ref.shape[1]
    route = route_ref[...]
    y = x_ref[...] + route[:, 2:3] * y_ref[:, 0:d] + route[:, 3:4] * y_ref[:, d:2 * d]
    res = _rms(y, g_ref[...])

    @pl.when(pl.program_id(0) < tiles_a)
    def _():
        oa_ref[...] = res

    @pl.when(pl.program_id(0) >= tiles_a)
    def _():
        ob_ref[...] = res


def moe_finish(x, route, y2, g, n_a, tm=TOK_TILE):
    n, d = x.shape
    tiles_a = n_a // tm
    return pl.pallas_call(
        functools.partial(_finish_kernel, tiles_a=tiles_a),
        grid=(n // tm,),
        in_specs=[pl.BlockSpec((tm, d), lambda i: (i, 0)),
                  pl.BlockSpec((tm, ROUTE_LANES), lambda i: (i, 0)),
                  pl.BlockSpec((1, d), lambda i: (0, 0)),
                  pl.BlockSpec((tm, 2 * d), lambda i: (i, 0))],
        out_specs=list(_two_part_specs(tm, d, tiles_a)),
        out_shape=[jax.ShapeDtypeStruct((n_a, d), F32), jax.ShapeDtypeStruct((n - n_a, d), F32)],
        compiler_params=_cparams("arbitrary"),
        name="moe_finish",
    )(x, route, g.reshape(1, d), y2)


def _tile_flags(seq_lens, tile):
    firsts, lasts = [], []
    for length in seq_lens:
        k = length // tile
        firsts += [1] + [0] * (k - 1)
        lasts += [0] * (k - 1) + [1]
    return np.asarray(firsts, np.int32), np.asarray(lasts, np.int32)


def _dft_table_kernel(cphi_ref, sphi_ref, cth_ref, sth_ref, c_ref, s_ref):
    cphi, sphi = cphi_ref[...], sphi_ref[...]
    cth, sth = cth_ref[0], sth_ref[0]
    c_ref[...] = (cphi * cth - sphi * sth).astype(BF16)
    s_ref[...] = (sphi * cth + cphi * sth).astype(BF16)


def _angles(rows, t):
    k = lax.broadcasted_iota(jnp.int32, (rows.shape[0], t), 1)
    ang = ((rows[:, None] * k) % t).astype(F32) * (2.0 * np.pi / t)
    return jnp.cos(ang), jnp.sin(ang)


def _dft_tables(t):
    rows = min(DFT_GEN_ROWS, t)
    nt = t // rows
    cphi, sphi = _angles(jnp.arange(rows, dtype=jnp.int32), t)
    cth, sth = _angles(jnp.arange(nt, dtype=jnp.int32) * rows, t)
    tile = pl.BlockSpec((rows, t), lambda i: (0, 0))
    vec = pl.BlockSpec((1, 1, t), lambda i: (i, 0, 0))
    out = pl.BlockSpec((rows, t), lambda i: (i, 0))
    return pl.pallas_call(
        _dft_table_kernel,
        grid=(nt,),
        in_specs=[tile, tile, vec, vec],
        out_specs=[out, out],
        out_shape=[jax.ShapeDtypeStruct((t, t), BF16)] * 2,
        compiler_params=_cparams("parallel"),
        name="dft_table_%d" % t,
    )(cphi, sphi, cth.reshape(nt, 1, t), sth.reshape(nt, 1, t))


def kernel(x_prompt, x_sample, norm_mix, norm_ffn, norm_final, w_in_ab, hgrn_lb_logits, hgrn_out_norm, sgu_norm,
           sgu_w, sgu_b, w_out_ab, w_in_cd, conv_w, conv_b, conv_ln_g, conv_ln_b, w_out_cd, ffn_w1, ffn_w3, ffn_w2,
           router_w, moe_w1, moe_w3, moe_w2):
    bp, tp, d = x_prompt.shape
    bs, ts_, _ = x_sample.shape
    n_p, n_s = bp * tp, bs * ts_
    n = n_p + n_s
    seq_lens = [tp] * bp + [ts_] * bs
    tile = TOK_TILE
    firsts, lasts = _tile_flags(seq_lens, tile)
    depth = norm_mix.shape[0]
    assert depth == 2, "the layer schedule below is the two-layer trunk"
    xa, xb_in = x_prompt.reshape(n_p, d), x_sample.reshape(n_s, d)

    for layer in range(depth):
        j = layer // 2
        if layer % 2 == 0:
            z = norm_proj(xa, xb_in, norm_mix[layer], w_in_ab[j].astype(BF16))
            o_f = hgrn_scan(z, hgrn_lb_logits, jnp.asarray(firsts), reverse=False, layer=layer, f_col=1)
            o_b = hgrn_scan(z, hgrn_lb_logits, jnp.asarray(lasts[::-1].copy()), reverse=True, layer=layer, f_col=2)
            x = mix_ab(xa, xb_in, o_f, o_b, z, hgrn_out_norm[j], sgu_norm[j], sgu_w[j].astype(BF16), sgu_b[j].T,
                       w_out_ab[j].astype(BF16))
            x = ffn(x, norm_ffn[layer], ffn_w1[j].astype(BF16), ffn_w3[j].astype(BF16), ffn_w2[j].astype(BF16))
        else:
            ang = (lax.broadcasted_iota(jnp.int32, (HEAD, HEAD), 0) * lax.broadcasted_iota(jnp.int32, (HEAD, HEAD), 1)
                   % HEAD).astype(F32) * (2.0 * np.pi / HEAD)
            dft_c = jnp.concatenate([jnp.cos(ang), jnp.sin(ang)], axis=1).astype(BF16)
            zc, wv = norm_proj_cd(x, norm_mix[layer], w_in_cd[j].astype(BF16), dft_c)
            c_out = conv_module(zc, jnp.asarray(firsts), jnp.asarray(lasts), conv_w[j], conv_b[j], conv_ln_g[j],
                                conv_ln_b[j])
            cos_p, sin_p = _dft_tables(tp)
            cos_s, sin_s = _dft_tables(ts_)
            d_out = jnp.concatenate([seq_dft(wv, cos_p, sin_p, 0, bp, tp),
                                     seq_dft(wv, cos_s, sin_s, n_p, bs, ts_)], axis=0)

            rw = jnp.zeros((d, ROUTE_LANES), F32).at[:, :N_EXPERTS].set(router_w[j])
            rw_hi = rw.astype(BF16)
            rw_lo = (rw - rw_hi.astype(F32)).astype(BF16)
            x3, h, route, counts = mix_cd_router(x, c_out, d_out, w_out_cd[j].astype(BF16), norm_ffn[layer],
                                                 rw_hi, rw_lo)

            tmb = MOE_BLOCK
            n_blocks = (2 * n + tmb - 1) // tmb + N_EXPERTS
            cnt = counts[0, :N_EXPERTS].astype(jnp.int32)
            padded = (cnt + tmb - 1) // tmb * tmb
            pend = jnp.cumsum(padded)
            pstart = pend - padded
            e_idx = route[:, 0:2].astype(jnp.int32)
            dest = (pstart[e_idx] + route[:, 4:6].astype(jnp.int32)).reshape(2 * n)
            block_e = jnp.minimum(
                jnp.searchsorted(pend, jnp.arange(n_blocks, dtype=jnp.int32) * tmb, side='right'),
                N_EXPERTS - 1).astype(jnp.int32)
            n_used = (pend[-1] // tmb).astype(jnp.int32).reshape(1)

            xb = moe_dispatch(h, dest, jnp.zeros((n_blocks * tmb, d), F32))
            yb = moe_ffn(xb, block_e, n_used, moe_w1[j].astype(BF16), moe_w3[j].astype(BF16),
                         moe_w2[j].astype(BF16))
            y2 = sc_row_gather(yb, dest).reshape(n, 2 * d)
            out_p, out_s = moe_finish(x3, route, y2, norm_final, n_p)

    return out_p.reshape(bp, tp, d), out_s.reshape(bs, ts_, d)
```

```python
import functools

import numpy as np
import jax
import jax.numpy as jnp
from jax import lax
from jax.experimental import pallas as pl
from jax.experimental.pallas import tpu as pltpu
from jax.experimental.pallas import tpu_sc as plsc

F32 = jnp.float32
BF16 = jnp.bfloat16
EPS = 1e-6

D_MODEL = 1024
HALF = 512
HEAD = 128
N_HEADS = 4
HGRN_CHUNK = 64
HGRN_SUB = 16
HGRN_CHUNKS_PER_ITER = 4
CONV_K = 31
CONV_PAD = 15
CONV_HALO = 16
N_EXPERTS = 8
ROUTE_LANES = 128
MAX_EXP = 80.0

VMEM_LIMIT_BYTES = 56 * 1024 * 1024

TOK_TILE = 512
MOE_BLOCK = 1024
MOE_FTILE = 1792
MOE_ROWS_SUB = 512
MOE_COLS_SUB = 256
FFN_FTILE = 256
SC_CORES = 2
SC_SUBCORES = 16
SC_ROW_CHUNK = 128
SEQ_DFT_TILE_BYTES = 4 * 1024 * 1024
DFT_GEN_ROWS = 128


def _dot(a, b):
    return jnp.dot(a, b, preferred_element_type=F32)


def _dot_nt(a, b):
    return lax.dot_general(a, b, (((1,), (1,)), ((), ())), preferred_element_type=F32)


def _cparams(*sem):
    return pltpu.CompilerParams(dimension_semantics=sem, vmem_limit_bytes=VMEM_LIMIT_BYTES)


def _rms(x, g):
    ms = jnp.mean(x * x, axis=-1, keepdims=True)
    return x * lax.rsqrt(ms + EPS) * g


def _silu(x):
    return x * jax.nn.sigmoid(x)


def _pack_bf16_pairs(x):
    w = x.shape[1] // 2
    bits = lax.bitcast_convert_type(x, jnp.uint32)
    return (bits[:, :w] & jnp.uint32(0xFFFF0000)) | (bits[:, w:] >> 16)


def _unpack_bf16_pairs(words):
    hi = lax.bitcast_convert_type(words & jnp.uint32(0xFFFF0000), F32)
    lo = lax.bitcast_convert_type(words << 16, F32)
    return hi, lo


def _gelu_tanh(x):
    return 0.5 * x * (1.0 + jnp.tanh(0.7978845608028654 * (x + 0.044715 * (x * x * x))))


def _two_part_specs(tm, d, tiles_a):
    return (pl.BlockSpec((tm, d), lambda i: (jnp.minimum(i, tiles_a - 1), 0)),
            pl.BlockSpec((tm, d), lambda i: (jnp.maximum(i - tiles_a, 0), 0)))


def _two_part_read(xa_ref, xb_ref, tiles_a):
    return jnp.where(pl.program_id(0) < tiles_a, xa_ref[...], xb_ref[...])


def _norm_proj_kernel(xa_ref, xb_ref, g_ref, w_ref, o_ref, *, tn, tiles_a):
    h = _rms(_two_part_read(xa_ref, xb_ref, tiles_a), g_ref[...]).astype(BF16)
    for c in range(o_ref.shape[1] // tn):
        o_ref[:, c * tn:(c + 1) * tn] = _dot(h, w_ref[:, c * tn:(c + 1) * tn])


def norm_proj(xa, xb, g, w_bf16, tm=TOK_TILE, tn=512):
    d = xa.shape[1]
    n = xa.shape[0] + xb.shape[0]
    tiles_a = xa.shape[0] // tm
    nout = w_bf16.shape[1]
    return pl.pallas_call(
        functools.partial(_norm_proj_kernel, tn=tn, tiles_a=tiles_a),
        grid=(n // tm,),
        in_specs=[*_two_part_specs(tm, d, tiles_a),
                  pl.BlockSpec((1, d), lambda i: (0, 0)),
                  pl.BlockSpec((d, nout), lambda i: (0, 0))],
        out_specs=pl.BlockSpec((tm, nout), lambda i: (i, 0)),
        out_shape=jax.ShapeDtypeStruct((n, nout), F32),
        compiler_params=_cparams("parallel"),
        name="norm_proj_ab",
    )(xa, xb, g.reshape(1, d), w_bf16)


def _norm_proj_cd_kernel(x_ref, g_ref, w_ref, dft_ref, zc_ref, wv_ref):
    h = _rms(x_ref[...], g_ref[...]).astype(BF16)
    for c in range(2):
        zc_ref[:, c * HALF:(c + 1) * HALF] = _dot(h, w_ref[:, c * HALF:(c + 1) * HALF])
    d = _dot(h, w_ref[:, 2 * HALF:3 * HALF]).astype(BF16)
    for g in range(N_HEADS):
        r = _dot(d[:, g * HEAD:(g + 1) * HEAD], dft_ref[...])
        wv_ref[:, g * HEAD:(g + 1) * HEAD] = r[:, :HEAD].astype(BF16)
        wv_ref[:, HALF + g * HEAD:HALF + (g + 1) * HEAD] = r[:, HEAD:].astype(BF16)


def norm_proj_cd(x, g, w_bf16, dft_c, tm=TOK_TILE):
    n, d = x.shape
    return pl.pallas_call(
        _norm_proj_cd_kernel,
        grid=(n // tm,),
        in_specs=[pl.BlockSpec((tm, d), lambda i: (i, 0)),
                  pl.BlockSpec((1, d), lambda i: (0, 0)),
                  pl.BlockSpec((d, 3 * HALF), lambda i: (0, 0)),
                  pl.BlockSpec((HEAD, 2 * HEAD), lambda i: (0, 0))],
        out_specs=[pl.BlockSpec((tm, 2 * HALF), lambda i: (i, 0)),
                   pl.BlockSpec((tm, 2 * HALF), lambda i: (i, 0))],
        out_shape=[jax.ShapeDtypeStruct((n, 2 * HALF), F32),
                   jax.ShapeDtypeStruct((n, 2 * HALF), BF16)],
        compiler_params=_cparams("parallel"),
        name="norm_proj_cd",
    )(x, g.reshape(1, d), w_bf16, dft_c)


def _hgrn_kernel(reset_ref, q_ref, f_ref, v_ref, lbl_ref, o_ref, st0, st1, st2, st3, oi_ref, qe_ref, u_ref, d_ref,
                 *, reverse, layer, tt):
    c, sub = HGRN_CHUNK, HGRN_SUB
    n_sub = c // sub
    n_chunks = tt // c
    states = (st0, st1, st2, st3)

    @pl.when(reset_ref[pl.program_id(0)] == 1)
    def _():
        for st in states:
            st[...] = jnp.zeros_like(st)

    lg = lbl_ref[...]
    e = jnp.exp(lg - jnp.max(lg, axis=0, keepdims=True))
    p = e / jnp.sum(e, axis=0, keepdims=True)
    lb_all = p[0:1, :]
    for r in range(1, layer + 1):
        lb_all = lb_all + p[r:r + 1, :]

    row = lax.broadcasted_iota(jnp.int32, (c, c), 0)
    col = lax.broadcasted_iota(jnp.int32, (c, c), 1)
    tri = jnp.where((col >= row) if reverse else (col <= row), 1.0, 0.0).astype(BF16)
    spans, keeps = [], []
    for i in range(n_sub):
        span = slice(i * sub, c) if reverse else slice(0, (i + 1) * sub)
        n_span = span.stop - span.start
        tr = lax.broadcasted_iota(jnp.int32, (sub, n_span), 0)
        sc = lax.broadcasted_iota(jnp.int32, (sub, n_span), 1)
        spans.append(span)
        keeps.append((sc >= tr) if reverse else (sc <= tr + i * sub))

    units = [(cj, h) for cj in range(HGRN_CHUNKS_PER_ITER) for h in range(N_HEADS)]

    def local_body(it, carry):
        ci = [it * HGRN_CHUNKS_PER_ITER + cj for cj in range(HGRN_CHUNKS_PER_ITER)]
        r0 = [pl.multiple_of(x * c, c) for x in ci]
        vals = {}
        for cj, h in units:
            cols = slice(h * HEAD, (h + 1) * HEAD)
            lb = lb_all[:, cols]
            q = _silu(q_ref[pl.ds(r0[cj], c), cols])
            v = v_ref[pl.ds(r0[cj], c), cols]
            f = lb + (1.0 - lb) * jax.nn.sigmoid(f_ref[pl.ds(r0[cj], c), cols])
            lf = jnp.log(f)
            lf_hi = lf.astype(BF16)
            lf_lo = (lf - lf_hi.astype(F32)).astype(BF16)
            vals[cj, h] = (q, v, 1.0 - f, _dot(tri, lf_hi) + _dot(tri, lf_lo))
        scores = {}
        for cj, h in units:
            cols = slice(h * HEAD, (h + 1) * HEAD)
            q, v, k, b = vals[cj, h]
            qe_ref[pl.ds(r0[cj], c), cols] = (q * jnp.exp(b)).astype(BF16)
            edge = b[0:1, :] if reverse else b[c - 1:c, :]
            ks = (k * jnp.exp(edge - b)).astype(BF16)
            u_ref[ci[cj] * N_HEADS + h] = _dot(v.T.astype(BF16), ks)
            d_ref[ci[cj] * N_HEADS + h] = jnp.broadcast_to(jnp.exp(edge), (8, HEAD))
            for i in range(n_sub):
                rows = slice(i * sub, (i + 1) * sub)
                if reverse:
                    anchor = b[(i + 1) * sub:(i + 1) * sub + 1, :] if i + 1 < n_sub else jnp.zeros((1, HEAD), F32)
                else:
                    anchor = b[i * sub - 1:i * sub, :] if i > 0 else jnp.zeros((1, HEAD), F32)
                qi = (q[rows] * jnp.exp(b[rows] - anchor)).astype(BF16)
                ki = (k[spans[i]] * jnp.exp(jnp.minimum(anchor - b[spans[i]], MAX_EXP))).astype(BF16)
                scores[cj, h, i] = _dot_nt(qi, ki)
        for cj, h in units:
            cols = slice(h * HEAD, (h + 1) * HEAD)
            vb = vals[cj, h][1].astype(BF16)
            for i in range(n_sub):
                a = jnp.where(keeps[i], scores[cj, h, i], 0.0).astype(BF16)
                oi_ref[pl.ds(r0[cj] + i * sub, sub), cols] = _dot(a, vb[spans[i]])
        return carry

    lax.fori_loop(0, n_chunks // HGRN_CHUNKS_PER_ITER, local_body, 0)

    cur = [st[...] for st in states]
    for ci in range(n_chunks):
        cc = (n_chunks - 1 - ci) if reverse else ci
        for h in range(N_HEADS):
            cols = slice(h * HEAD, (h + 1) * HEAD)
            o_ref[cc * c:(cc + 1) * c, cols] = (oi_ref[cc * c:(cc + 1) * c, cols]
                                                + _dot_nt(qe_ref[cc * c:(cc + 1) * c, cols], cur[h].astype(BF16)))
        cur = [cur[h] * d_ref[cc * N_HEADS + h][0:1, :] + u_ref[cc * N_HEADS + h] for h in range(N_HEADS)]
    for h in range(N_HEADS):
        states[h][...] = cur[h]


def hgrn_scan(z, lb_logits, resets, *, reverse, layer, f_col, tt=TOK_TILE):
    n = z.shape[0]
    nt = n // tt
    n_units = tt // HGRN_CHUNK * N_HEADS
    order = (lambda i, r: (nt - 1 - i, 0)) if reverse else (lambda i, r: (i, 0))
    blk = lambda cb: (lambda i, r: (order(i, r)[0], cb))
    grid_spec = pltpu.PrefetchScalarGridSpec(
        num_scalar_prefetch=1,
        grid=(nt,),
        in_specs=[pl.BlockSpec((tt, HALF), blk(0)),
                  pl.BlockSpec((tt, HALF), blk(f_col)),
                  pl.BlockSpec((tt, HALF), blk(3)),
                  pl.BlockSpec(lb_logits.shape, lambda i, r: (0, 0))],
        out_specs=pl.BlockSpec((tt, HALF), order),
        scratch_shapes=[pltpu.VMEM((HEAD, HEAD), F32)] * N_HEADS + [
            pltpu.VMEM((tt, HALF), F32),
            pltpu.VMEM((tt, HALF), BF16),
            pltpu.VMEM((n_units, HEAD, HEAD), F32),
            pltpu.VMEM((n_units, 8, HEAD), F32)],
    )
    return pl.pallas_call(
        functools.partial(_hgrn_kernel, reverse=reverse, layer=layer, tt=tt),
        grid_spec=grid_spec,
        out_shape=jax.ShapeDtypeStruct((n, HALF), F32),
        compiler_params=_cparams("arbitrary"),
        name="hgrn_bwd" if reverse else "hgrn_fwd",
    )(resets, z, z, z, lb_logits)


def _mix_ab_kernel(xa_ref, xb_ref, of_ref, ob_ref, g_ref, u_ref, v_ref, onorm_ref, snorm_ref, wsp_ref, bsp_ref,
                   wout_ref, o_ref, *, tm, tiles_a):
    o = of_ref[...] + ob_ref[...]
    g = g_ref[...]
    parts = []
    for h in range(N_HEADS):
        cols = slice(h * HEAD, (h + 1) * HEAD)
        parts.append(_rms(o[:, cols], onorm_ref[:, cols]) * _silu(g[:, cols]))
    a = jnp.concatenate(parts, axis=1).astype(BF16)
    acc = _two_part_read(xa_ref, xb_ref, tiles_a) + _dot(a, wout_ref[0:HALF, :])

    u = _gelu_tanh(u_ref[...])
    vb = _rms(_gelu_tanh(v_ref[...]), snorm_ref[...]).astype(BF16)
    rows = []
    for c in range(tm // HEAD):
        mixed = []
        for gi in range(N_HEADS):
            vg = vb[c * HEAD:(c + 1) * HEAD, gi * HEAD:(gi + 1) * HEAD]
            mixed.append(_dot(wsp_ref[gi], vg) + bsp_ref[:, gi:gi + 1])
        rows.append(jnp.concatenate(mixed, axis=1))
    b = (u * jnp.concatenate(rows, axis=0)).astype(BF16)
    o_ref[...] = acc + _dot(b, wout_ref[HALF:2 * HALF, :])


def mix_ab(xa, xb, o_f, o_b, z, onorm, snorm, wsp_bf16, bsp_t, wout_bf16, tm=TOK_TILE):
    d = xa.shape[1]
    n = xa.shape[0] + xb.shape[0]
    tiles_a = xa.shape[0] // tm
    zblk = lambda cb: pl.BlockSpec((tm, HALF), lambda i: (i, cb))
    const = lambda shape: pl.BlockSpec(shape, lambda i: (0,) * len(shape))
    return pl.pallas_call(
        functools.partial(_mix_ab_kernel, tm=tm, tiles_a=tiles_a),
        grid=(n // tm,),
        in_specs=[*_two_part_specs(tm, d, tiles_a),
                  pl.BlockSpec((tm, HALF), lambda i: (i, 0)),
                  pl.BlockSpec((tm, HALF), lambda i: (i, 0)),
                  zblk(4), zblk(5), zblk(6),
                  const((1, HALF)), const((1, HALF)),
                  const((N_HEADS, HEAD, HEAD)), const((HEAD, N_HEADS)),
                  const((d, d))],
        out_specs=pl.BlockSpec((tm, d), lambda i: (i, 0)),
        out_shape=jax.ShapeDtypeStruct((n, d), F32),
        compiler_params=_cparams("parallel"),
        name="mix_ab",
    )(xa, xb, o_f, o_b, z, z, z, onorm.reshape(1, HALF), snorm.reshape(1, HALF), wsp_bf16, bsp_t, wout_bf16)


def _ffn_kernel(x_ref, g_ref, w1_ref, w3_ref, w2_ref, o_ref, *, tf):
    x = x_ref[...]
    h = _rms(x, g_ref[...]).astype(BF16)
    acc = x
    for c in range(w1_ref.shape[1] // tf):
        cols = slice(c * tf, (c + 1) * tf)
        act = (_silu(_dot(h, w1_ref[:, cols])) * _dot(h, w3_ref[:, cols])).astype(BF16)
        acc = acc + _dot(act, w2_ref[cols, :])
    o_ref[...] = acc


def ffn(x, g, w1, w3, w2, tm=TOK_TILE, tf=FFN_FTILE):
    n, d = x.shape
    f = w1.shape[1]
    const = lambda shape: pl.BlockSpec(shape, lambda i: (0,) * len(shape))
    return pl.pallas_call(
        functools.partial(_ffn_kernel, tf=tf),
        grid=(n // tm,),
        in_specs=[pl.BlockSpec((tm, d), lambda i: (i, 0)), const((1, d)),
                  const((d, f)), const((d, f)), const((f, d))],
        out_specs=pl.BlockSpec((tm, d), lambda i: (i, 0)),
        out_shape=jax.ShapeDtypeStruct((n, d), F32),
        compiler_params=_cparams("parallel"),
        name="ffn_dense",
    )(x, g.reshape(1, d), w1, w3, w2)


def _conv_kernel(first_ref, last_ref, a_ref, gt_ref, ap_ref, gp_ref, an_ref, gn_ref, w_ref, b_ref, lng_ref,
                 lnb_ref, o_ref, ypad_ref, *, ts, rb):
    i = pl.program_id(0)
    halo = CONV_HALO
    keep_prev = jnp.where(first_ref[i] == 1, 0.0, 1.0)
    keep_next = jnp.where(last_ref[i] == 1, 0.0, 1.0)
    ypad_ref[0:halo, :] = ap_ref[...] * jax.nn.sigmoid(gp_ref[...]) * keep_prev
    ypad_ref[halo:halo + ts, :] = a_ref[...] * jax.nn.sigmoid(gt_ref[...])
    ypad_ref[halo + ts:2 * halo + ts, :] = an_ref[...] * jax.nn.sigmoid(gn_ref[...]) * keep_next

    n_win = rb + 2 * halo

    def body(bi, carry):
        r0 = pl.multiple_of(bi * rb, rb)
        strips = []
        for s in range(HALF // HEAD):
            cols = slice(s * HEAD, (s + 1) * HEAD)
            win = ypad_ref[pl.ds(r0, n_win), cols]
            acc = jnp.zeros((rb, HEAD), F32) + b_ref[:, cols]
            for res in range(8):
                sh = win if res == 0 else pltpu.roll(win, n_win - res, axis=0)
                for j in range(CONV_K):
                    off = halo - CONV_PAD + j
                    if off % 8 == res:
                        base = off - res
                        acc = acc + w_ref[j:j + 1, cols] * sh[base:base + rb, :]
            strips.append(acc)
        acc = jnp.concatenate(strips, axis=1)
        mu = jnp.mean(acc, axis=-1, keepdims=True)
        xc = acc - mu
        var = jnp.mean(xc * xc, axis=-1, keepdims=True)
        y = xc * lax.rsqrt(var + EPS) * lng_ref[...] + lnb_ref[...]
        o_ref[pl.ds(r0, rb), :] = _silu(y).astype(o_ref.dtype)
        return carry

    lax.fori_loop(0, ts // rb, body, 0)


def conv_module(zc, firsts, lasts, w, b, ln_g, ln_b, ts=TOK_TILE, rb=64):
    n = zc.shape[0]
    nt = n // ts
    hb = ts // CONV_HALO
    n_hb = n // CONV_HALO
    main = lambda cb: pl.BlockSpec((ts, HALF), lambda i, f, l: (i, cb))
    prev = lambda cb: pl.BlockSpec((CONV_HALO, HALF), lambda i, f, l: (jnp.maximum(i * hb - 1, 0), cb))
    nxt = lambda cb: pl.BlockSpec((CONV_HALO, HALF), lambda i, f, l: (jnp.minimum((i + 1) * hb, n_hb - 1), cb))
    const = lambda shape: pl.BlockSpec(shape, lambda i, f, l: (0,) * len(shape))
    grid_spec = pltpu.PrefetchScalarGridSpec(
        num_scalar_prefetch=2,
        grid=(nt,),
        in_specs=[main(0), main(1), prev(0), prev(1), nxt(0), nxt(1),
                  const((CONV_K, HALF)), const((1, HALF)), const((1, HALF)), const((1, HALF))],
        out_specs=pl.BlockSpec((ts, HALF), lambda i, f, l: (i, 0)),
        scratch_shapes=[pltpu.VMEM((ts + 2 * CONV_HALO, HALF), F32)],
    )
    return pl.pallas_call(
        functools.partial(_conv_kernel, ts=ts, rb=rb),
        grid_spec=grid_spec,
        out_shape=jax.ShapeDtypeStruct((n, HALF), BF16),
        compiler_params=_cparams("parallel"),
        name="conv_module",
    )(firsts, lasts, zc, zc, zc, zc, zc, zc, w, b.reshape(1, HALF), ln_g.reshape(1, HALF), ln_b.reshape(1, HALF))


def _seq_dft_kernel(c_ref, s_ref, wv_ref, o_ref, *, scale):
    acc = _dot(c_ref[...], wv_ref[:, 0:HALF]) - _dot(s_ref[...], wv_ref[:, HALF:2 * HALF])
    o_ref[...] = (acc * scale).astype(o_ref.dtype)


def seq_dft(wv, cos_t, sin_t, row0, n_seq, seq_len):
    tmo = min(seq_len, max(256, SEQ_DFT_TILE_BYTES // (2 * seq_len)))
    nt = seq_len // tmo
    b0 = row0 // seq_len
    scale = 1.0 / float(np.sqrt(seq_len * HEAD))
    return pl.pallas_call(
        functools.partial(_seq_dft_kernel, scale=scale),
        grid=(n_seq, nt),
        in_specs=[pl.BlockSpec((tmo, seq_len), lambda b, i: (i, 0)),
                  pl.BlockSpec((tmo, seq_len), lambda b, i: (i, 0)),
                  pl.BlockSpec((seq_len, 2 * HALF), lambda b, i: (b0 + b, 0))],
        out_specs=pl.BlockSpec((tmo, HALF), lambda b, i: (b * nt + i, 0)),
        out_shape=jax.ShapeDtypeStruct((n_seq * seq_len, HALF), BF16),
        compiler_params=_cparams("parallel", "parallel"),
        name="seq_dft_%d" % seq_len,
    )(cos_t, sin_t, wv)


def _mix_cd_router_kernel(x_ref, c_ref, d_ref, wout_ref, g_ref, rwh_ref, rwl_ref, strict_ref, x3_ref, h_ref,
                          route_ref, cnt_ref, carry_ref, *, tm):
    @pl.when(pl.program_id(0) == 0)
    def _():
        carry_ref[...] = jnp.zeros_like(carry_ref)

    x3 = x_ref[...] + _dot(c_ref[...], wout_ref[0:HALF, :]) + _dot(d_ref[...], wout_ref[HALF:2 * HALF, :])
    x3_ref[...] = x3
    h = _rms(x3, g_ref[...])

    h_hi = h.astype(BF16)
    h_lo = (h - h_hi.astype(F32)).astype(BF16)
    h_ref[...] = _pack_bf16_pairs(h_hi.astype(F32))
    logits = _dot(h_hi, rwh_ref[...]) + (_dot(h_lo, rwh_ref[...]) + _dot(h_hi, rwl_ref[...]))
    lane = lax.broadcasted_iota(jnp.int32, (tm, ROUTE_LANES), 1)
    neg = jnp.float32(-jnp.inf)
    logits = jnp.where(lane < N_EXPERTS, logits, neg)
    m1 = jnp.max(logits, axis=-1, keepdims=True)
    i1 = jnp.min(jnp.where(logits == m1, lane, ROUTE_LANES), axis=-1, keepdims=True)
    rest = jnp.where(lane == i1, neg, logits)
    m2 = jnp.max(rest, axis=-1, keepdims=True)
    i2 = jnp.min(jnp.where(rest == m2, lane, ROUTE_LANES), axis=-1, keepdims=True)
    e2 = jnp.exp(m2 - m1)
    g1 = 1.0 / (1.0 + e2)
    g2 = e2 / (1.0 + e2)

    onehot = jnp.where((lane == i1) | (lane == i2), 1.0, 0.0)
    before = _dot(strict_ref[...], onehot.astype(BF16)) + carry_ref[0:1, :]
    r1 = jnp.sum(jnp.where(lane == i1, before, 0.0), axis=-1, keepdims=True)
    r2 = jnp.sum(jnp.where(lane == i2, before, 0.0), axis=-1, keepdims=True)
    total = carry_ref[0:1, :] + jnp.sum(onehot, axis=0, keepdims=True)
    carry_ref[...] = jnp.broadcast_to(total, carry_ref.shape)
    cnt_ref[...] = jnp.broadcast_to(total, cnt_ref.shape)

    route = jnp.where(lane == 0, i1.astype(F32), 0.0)
    route = jnp.where(lane == 1, i2.astype(F32), route)
    route = jnp.where(lane == 2, g1, route)
    route = jnp.where(lane == 3, g2, route)
    route = jnp.where(lane == 4, r1, route)
    route = jnp.where(lane == 5, r2, route)
    route_ref[...] = route


def mix_cd_router(x, c_out, d_out, wout_bf16, g, rw_hi, rw_lo, tm=TOK_TILE):
    n, d = x.shape
    const = lambda shape: pl.BlockSpec(shape, lambda i: (0,) * len(shape))
    tok = lambda w: pl.BlockSpec((tm, w), lambda i: (i, 0))
    strict = jnp.tril(jnp.ones((tm, tm), BF16), -1)
    return pl.pallas_call(
        functools.partial(_mix_cd_router_kernel, tm=tm),
        grid=(n // tm,),
        in_specs=[tok(d), tok(HALF), tok(HALF), const((d, d)), const((1, d)),
                  const((d, ROUTE_LANES)), const((d, ROUTE_LANES)), const((tm, tm))],
        out_specs=[tok(d), tok(d // 2), tok(ROUTE_LANES), const((8, ROUTE_LANES))],
        out_shape=[jax.ShapeDtypeStruct((n, d), F32), jax.ShapeDtypeStruct((n, d // 2), jnp.uint32),
                   jax.ShapeDtypeStruct((n, ROUTE_LANES), F32), jax.ShapeDtypeStruct((8, ROUTE_LANES), F32)],
        scratch_shapes=[pltpu.VMEM((8, ROUTE_LANES), F32)],
        compiler_params=_cparams("arbitrary"),
        name="mix_cd_router",
    )(x, c_out, d_out, wout_bf16, g.reshape(1, d), rw_hi, rw_lo, strict)


_SC_WORKERS = SC_CORES * SC_SUBCORES


def _sc_mesh():
    return plsc.VectorSubcoreMesh(core_axis_name="c", subcore_axis_name="s")


def _sc_worker_id():
    return lax.axis_index("s") * SC_CORES + lax.axis_index("c")


def sc_dispatch(rows, dest0, dest1, pad_slots, n_slots, chunk=SC_ROW_CHUNK):
    n, w = rows.shape
    n_pad = pad_slots.shape[0]
    per_worker = n // _SC_WORKERS
    pad_per_worker = n_pad // _SC_WORKERS
    assert per_worker * _SC_WORKERS == n and per_worker % chunk == 0 and chunk % 8 == 0
    assert pad_per_worker * _SC_WORKERS == n_pad and pad_per_worker % chunk == 0

    @functools.partial(
        pl.kernel, mesh=_sc_mesh(),
        out_type=jax.ShapeDtypeStruct((n_slots, w), rows.dtype),
        scratch_types=[pltpu.VMEM((chunk,), jnp.int32), pltpu.VMEM((chunk, w), rows.dtype),
                       pltpu.SemaphoreType.DMA],
    )
    def scatter(rows_hbm, d0_hbm, d1_hbm, pad_hbm, zeros_hbm, out_hbm, idx_v, rows_v, sem):
        worker = _sc_worker_id()

        @pl.loop(0, per_worker // chunk)
        def _(c):
            off = worker * per_worker + c * chunk
            pltpu.sync_copy(rows_hbm.at[pl.ds(off, chunk)], rows_v)
            for d_hbm in (d0_hbm, d1_hbm):
                pltpu.sync_copy(d_hbm.at[pl.ds(off, chunk)], idx_v)
                pltpu.async_copy(rows_v, out_hbm.at[idx_v], sem).wait()

        pltpu.sync_copy(zeros_hbm, rows_v)

        @pl.loop(0, pad_per_worker // chunk)
        def _(c):
            off = worker * pad_per_worker + c * chunk
            pltpu.sync_copy(pad_hbm.at[pl.ds(off, chunk)], idx_v)
            pltpu.async_copy(rows_v, out_hbm.at[idx_v], sem).wait()

    return scatter(rows, dest0, dest1, pad_slots, jnp.zeros((chunk, w), rows.dtype))


def sc_row_gather(table, idx, chunk=SC_ROW_CHUNK):
    b, w = idx.shape[0], table.shape[1]
    per_worker = b // _SC_WORKERS
    assert per_worker * _SC_WORKERS == b and per_worker % chunk == 0 and chunk % 8 == 0

    @functools.partial(
        pl.kernel, mesh=_sc_mesh(),
        out_type=jax.ShapeDtypeStruct((b, w), table.dtype),
        scratch_types=[pltpu.VMEM((chunk,), jnp.int32), pltpu.VMEM((chunk, w), table.dtype),
                       pltpu.SemaphoreType.DMA],
    )
    def gather(table_hbm, idx_hbm, out_hbm, idx_v, rows_v, sem):
        worker = _sc_worker_id()

        @pl.loop(0, per_worker // chunk)
        def _(c):
            off = worker * per_worker + c * chunk
            pltpu.sync_copy(idx_hbm.at[pl.ds(off, chunk)], idx_v)
            pltpu.async_copy(table_hbm.at[idx_v], rows_v, sem).wait()
            pltpu.sync_copy(rows_v, out_hbm.at[pl.ds(off, chunk)])

    return gather(table, idx)


def _moe_ffn_kernel(be_ref, nu_ref, x_ref, w1_ref, w3_ref, w2_ref, o_ref, xbf_ref, acc_ref, *, rows_sub, cols_sub, nf):
    del be_ref
    b = pl.program_id(0)
    f = pl.program_id(1)
    active = b < nu_ref[0]
    tmb = x_ref.shape[0]
    tf = w1_ref.shape[2]
    half = x_ref.shape[1]

    @pl.when(active & (f == 0))
    def _():
        hi, lo = _unpack_bf16_pairs(x_ref[...])
        xbf_ref[:, 0:half] = hi.astype(BF16)
        xbf_ref[:, half:2 * half] = lo.astype(BF16)

    def partial_sums(first, last):
        for r in range(tmb // rows_sub):
            rows = slice(r * rows_sub, (r + 1) * rows_sub)
            xr = xbf_ref[rows, :]
            acc = jnp.zeros((rows_sub, 2 * half), F32) if first else acc_ref[rows, :]
            for c in range(tf // cols_sub):
                cols = slice(c * cols_sub, (c + 1) * cols_sub)
                act = (_silu(_dot(xr, w1_ref[0, :, cols])) * _dot(xr, w3_ref[0, :, cols])).astype(BF16)
                acc = acc + _dot(act, w2_ref[0, cols, :])
            if last:
                o_ref[rows, :] = _pack_bf16_pairs(acc.astype(BF16).astype(F32))
            else:
                acc_ref[rows, :] = acc

    @pl.when(active & (f == 0))
    def _():
        partial_sums(True, False)

    if nf > 2:
        @pl.when(active & (f > 0) & (f < nf - 1))
        def _():
            partial_sums(False, False)

    @pl.when(active & (f == nf - 1))
    def _():
        partial_sums(False, True)

    @pl.when(jnp.logical_not(active) & (f == 0))
    def _():
        o_ref[...] = jnp.zeros_like(o_ref)


def moe_ffn(xb, block_e, n_used, w1, w3, w2, tmb=MOE_BLOCK, tf=MOE_FTILE):
    n_slots, half = xb.shape
    d = 2 * half
    nb = n_slots // tmb
    nf = w1.shape[2] // tf
    assert nf >= 2, "the first hidden-column step initialises the sum and the last one writes the block"

    def bclamp(b, nu):
        return jnp.minimum(b, nu[0] - 1)

    def fclamp(b, f, nu):
        return jnp.where(b < nu[0], f, nf - 1)

    grid_spec = pltpu.PrefetchScalarGridSpec(
        num_scalar_prefetch=2,
        grid=(nb, nf),
        in_specs=[pl.BlockSpec((tmb, half), lambda b, f, be, nu: (bclamp(b, nu), 0)),
                  pl.BlockSpec((1, d, tf), lambda b, f, be, nu: (be[bclamp(b, nu)], 0, fclamp(b, f, nu))),
                  pl.BlockSpec((1, d, tf), lambda b, f, be, nu: (be[bclamp(b, nu)], 0, fclamp(b, f, nu))),
                  pl.BlockSpec((1, tf, d), lambda b, f, be, nu: (be[bclamp(b, nu)], fclamp(b, f, nu), 0))],
        out_specs=pl.BlockSpec((tmb, half), lambda b, f, be, nu: (b, 0)),
        scratch_shapes=[pltpu.VMEM((tmb, d), BF16), pltpu.VMEM((tmb, d), F32)],
    )
    return pl.pallas_call(
        functools.partial(_moe_ffn_kernel, rows_sub=MOE_ROWS_SUB, cols_sub=MOE_COLS_SUB, nf=nf),
        grid_spec=grid_spec,
        out_shape=jax.ShapeDtypeStruct((n_slots, half), jnp.uint32),
        compiler_params=_cparams("arbitrary", "arbitrary"),
        name="moe_ffn",
    )(block_e, n_used, xb, w1, w3, w2)


def _finish_kernel(x_ref, route_ref, g_ref, y0_ref, y1_ref, oa_ref, ob_ref, *, tiles_a):
    half = y0_ref.shape[1]
    route = route_ref[...]
    g0, g1 = route[:, 2:3], route[:, 3:4]
    hi0, lo0 = _unpack_bf16_pairs(y0_ref[...])
    hi1, lo1 = _unpack_bf16_pairs(y1_ref[...])
    ya = x_ref[:, 0:half] + g0 * hi0 + g1 * hi1
    yb = x_ref[:, half:2 * half] + g0 * lo0 + g1 * lo1
    ms = (jnp.sum(ya * ya, axis=-1, keepdims=True) + jnp.sum(yb * yb, axis=-1, keepdims=True)) / (2 * half)
    inv = lax.rsqrt(ms + EPS)

    for o_ref, mine in ((oa_ref, pl.program_id(0) < tiles_a), (ob_ref, pl.program_id(0) >= tiles_a)):
        @pl.when(mine)
        def _():
            o_ref[:, 0:half] = ya * inv * g_ref[:, 0:half]
            o_ref[:, half:2 * half] = yb * inv * g_ref[:, half:2 * half]


def moe_finish(x, route, y_rows, g, n_a, tm=TOK_TILE):
    n, d = x.shape
    tiles_a = n_a // tm
    n_tiles = n // tm
    return pl.pallas_call(
        functools.partial(_finish_kernel, tiles_a=tiles_a),
        grid=(n_tiles,),
        in_specs=[pl.BlockSpec((tm, d), lambda i: (i, 0)),
                  pl.BlockSpec((tm, ROUTE_LANES), lambda i: (i, 0)),
                  pl.BlockSpec((1, d), lambda i: (0, 0)),
                  pl.BlockSpec((tm, d // 2), lambda i: (i, 0)),
                  pl.BlockSpec((tm, d // 2), lambda i: (i + n_tiles, 0))],
        out_specs=list(_two_part_specs(tm, d, tiles_a)),
        out_shape=[jax.ShapeDtypeStruct((n_a, d), F32), jax.ShapeDtypeStruct((n - n_a, d), F32)],
        compiler_params=_cparams("arbitrary"),
        name="moe_finish",
    )(x, route, g.reshape(1, d), y_rows, y_rows)


def _tile_flags(seq_lens, tile):
    firsts, lasts = [], []
    for length in seq_lens:
        k = length // tile
        firsts += [1] + [0] * (k - 1)
        lasts += [0] * (k - 1) + [1]
    return np.asarray(firsts, np.int32), np.asarray(lasts, np.int32)


def _dft_table_kernel(cphi_ref, sphi_ref, cth_ref, sth_ref, c_ref, s_ref):
    cphi, sphi = cphi_ref[...], sphi_ref[...]
    cth, sth = cth_ref[0], sth_ref[0]
    c_ref[...] = (cphi * cth - sphi * sth).astype(BF16)
    s_ref[...] = (sphi * cth + cphi * sth).astype(BF16)


def _angles(rows, t):
    k = lax.broadcasted_iota(jnp.int32, (rows.shape[0], t), 1)
    ang = ((rows[:, None] * k) % t).astype(F32) * (2.0 * np.pi / t)
    return jnp.cos(ang), jnp.sin(ang)


def _dft_tables(t):
    rows = min(DFT_GEN_ROWS, t)
    nt = t // rows
    cphi, sphi = _angles(jnp.arange(rows, dtype=jnp.int32), t)
    cth, sth = _angles(jnp.arange(nt, dtype=jnp.int32) * rows, t)
    tile = pl.BlockSpec((rows, t), lambda i: (0, 0))
    vec = pl.BlockSpec((1, 1, t), lambda i: (i, 0, 0))
    out = pl.BlockSpec((rows, t), lambda i: (i, 0))
    return pl.pallas_call(
        _dft_table_kernel,
        grid=(nt,),
        in_specs=[tile, tile, vec, vec],
        out_specs=[out, out],
        out_shape=[jax.ShapeDtypeStruct((t, t), BF16)] * 2,
        compiler_params=_cparams("parallel"),
        name="dft_table_%d" % t,
    )(cphi, sphi, cth.reshape(nt, 1, t), sth.reshape(nt, 1, t))


def kernel(x_prompt, x_sample, norm_mix, norm_ffn, norm_final, w_in_ab, hgrn_lb_logits, hgrn_out_norm, sgu_norm,
           sgu_w, sgu_b, w_out_ab, w_in_cd, conv_w, conv_b, conv_ln_g, conv_ln_b, w_out_cd, ffn_w1, ffn_w3, ffn_w2,
           router_w, moe_w1, moe_w3, moe_w2):
    bp, tp, d = x_prompt.shape
    bs, ts_, _ = x_sample.shape
    n_p, n_s = bp * tp, bs * ts_
    n = n_p + n_s
    seq_lens = [tp] * bp + [ts_] * bs
    tile = TOK_TILE
    firsts, lasts = _tile_flags(seq_lens, tile)
    depth = norm_mix.shape[0]
    assert depth == 2, "the layer schedule below is the two-layer trunk"
    xa, xb_in = x_prompt.reshape(n_p, d), x_sample.reshape(n_s, d)

    for layer in range(depth):
        j = layer // 2
        if layer % 2 == 0:
            z = norm_proj(xa, xb_in, norm_mix[layer], w_in_ab[j].astype(BF16))
            o_f = hgrn_scan(z, hgrn_lb_logits, jnp.asarray(firsts), reverse=False, layer=layer, f_col=1)
            o_b = hgrn_scan(z, hgrn_lb_logits, jnp.asarray(lasts[::-1].copy()), reverse=True, layer=layer, f_col=2)
            x = mix_ab(xa, xb_in, o_f, o_b, z, hgrn_out_norm[j], sgu_norm[j], sgu_w[j].astype(BF16), sgu_b[j].T,
                       w_out_ab[j].astype(BF16))
            x = ffn(x, norm_ffn[layer], ffn_w1[j].astype(BF16), ffn_w3[j].astype(BF16), ffn_w2[j].astype(BF16))
        else:
            ang = (lax.broadcasted_iota(jnp.int32, (HEAD, HEAD), 0) * lax.broadcasted_iota(jnp.int32, (HEAD, HEAD), 1)
                   % HEAD).astype(F32) * (2.0 * np.pi / HEAD)
            dft_c = jnp.concatenate([jnp.cos(ang), jnp.sin(ang)], axis=1).astype(BF16)
            zc, wv = norm_proj_cd(x, norm_mix[layer], w_in_cd[j].astype(BF16), dft_c)
            c_out = conv_module(zc, jnp.asarray(firsts), jnp.asarray(lasts), conv_w[j], conv_b[j], conv_ln_g[j],
                                conv_ln_b[j])
            cos_p, sin_p = _dft_tables(tp)
            cos_s, sin_s = _dft_tables(ts_)
            d_out = jnp.concatenate([seq_dft(wv, cos_p, sin_p, 0, bp, tp),
                                     seq_dft(wv, cos_s, sin_s, n_p, bs, ts_)], axis=0)

            rw = jnp.zeros((d, ROUTE_LANES), F32).at[:, :N_EXPERTS].set(router_w[j])
            rw_hi = rw.astype(BF16)
            rw_lo = (rw - rw_hi.astype(F32)).astype(BF16)
            x3, h, route, counts = mix_cd_router(x, c_out, d_out, w_out_cd[j].astype(BF16), norm_ffn[layer],
                                                 rw_hi, rw_lo)

            tmb = MOE_BLOCK
            n_blocks = (2 * n + tmb - 1) // tmb + N_EXPERTS
            cnt = counts[0, :N_EXPERTS].astype(jnp.int32)
            padded = (cnt + tmb - 1) // tmb * tmb
            pend = jnp.cumsum(padded)
            pstart = pend - padded
            e_idx = route[:, 0:2].astype(jnp.int32)
            dest = pstart[e_idx] + route[:, 4:6].astype(jnp.int32)
            dest0, dest1 = dest[:, 0], dest[:, 1]
            block_e = jnp.minimum(
                jnp.searchsorted(pend, jnp.arange(n_blocks, dtype=jnp.int32) * tmb, side='right'),
                N_EXPERTS - 1).astype(jnp.int32)
            n_used = (pend[-1] // tmb).astype(jnp.int32).reshape(1)
            n_slots = n_blocks * tmb
            jpad = jnp.arange(tmb, dtype=jnp.int32)[None, :]
            pad_slots = jnp.where(jpad < (padded - cnt)[:, None], (pstart + cnt)[:, None] + jpad,
                                  n_slots - 1).reshape(N_EXPERTS * tmb)

            xb = sc_dispatch(h, dest0, dest1, pad_slots, n_slots)
            yb = moe_ffn(xb, block_e, n_used, moe_w1[j].astype(BF16), moe_w3[j].astype(BF16),
                         moe_w2[j].astype(BF16))
            y_rows = sc_row_gather(yb, jnp.concatenate([dest0, dest1]))
            out_p, out_s = moe_finish(x3, route, y_rows, norm_final, n_p)

    return out_p.reshape(bp, tp, d), out_s.reshape(bs, ts_, d)
```

```python
import functools

import numpy as np
import jax
import jax.numpy as jnp
from jax import lax
from jax.experimental import pallas as pl
from jax.experimental.pallas import tpu as pltpu
from jax.experimental.pallas import tpu_sc as plsc

F32 = jnp.float32
BF16 = jnp.bfloat16
EPS = 1e-6

D_MODEL = 1024
HALF = 512
HEAD = 128
N_HEADS = 4
HGRN_CHUNK = 64
HGRN_SUB = 16
HGRN_CHUNKS_PER_ITER = 4
CONV_K = 31
CONV_PAD = 15
CONV_HALO = 16
N_EXPERTS = 8
ROUTE_LANES = 128
MAX_EXP = 80.0

VMEM_LIMIT_BYTES = 56 * 1024 * 1024

TOK_TILE = 512
MOE_BLOCK = 1024
MOE_FTILE = 1792
MOE_ROWS_SUB = 512
MOE_COLS_SUB = 256
FFN_FTILE = 256
SC_CORES = 2
SC_SUBCORES = 16
SC_ROW_CHUNK = 128
SEQ_DFT_TILE_BYTES = 4 * 1024 * 1024
DFT_GEN_ROWS = 128


def _dot(a, b):
    return jnp.dot(a, b, preferred_element_type=F32)


def _dot_nt(a, b):
    return lax.dot_general(a, b, (((1,), (1,)), ((), ())), preferred_element_type=F32)


def _cparams(*sem):
    return pltpu.CompilerParams(dimension_semantics=sem, vmem_limit_bytes=VMEM_LIMIT_BYTES)


def _rms(x, g):
    ms = jnp.mean(x * x, axis=-1, keepdims=True)
    return x * lax.rsqrt(ms + EPS) * g


def _silu(x):
    return x * jax.nn.sigmoid(x)


def _pack_bf16_pairs(x):
    w = x.shape[1] // 2
    bits = lax.bitcast_convert_type(x, jnp.uint32)
    return (bits[:, :w] & jnp.uint32(0xFFFF0000)) | (bits[:, w:] >> 16)


def _unpack_bf16_pairs(words):
    hi = lax.bitcast_convert_type(words & jnp.uint32(0xFFFF0000), F32)
    lo = lax.bitcast_convert_type(words << 16, F32)
    return hi, lo


def _gelu_tanh(x):
    return 0.5 * x * (1.0 + jnp.tanh(0.7978845608028654 * (x + 0.044715 * (x * x * x))))


def _two_part_specs(tm, d, tiles_a):
    return (pl.BlockSpec((tm, d), lambda i: (jnp.minimum(i, tiles_a - 1), 0)),
            pl.BlockSpec((tm, d), lambda i: (jnp.maximum(i - tiles_a, 0), 0)))


def _two_part_read(xa_ref, xb_ref, tiles_a):
    return jnp.where(pl.program_id(0) < tiles_a, xa_ref[...], xb_ref[...])


def _norm_proj_kernel(xa_ref, xb_ref, g_ref, w_ref, o_ref, *, tn, tiles_a):
    h = _rms(_two_part_read(xa_ref, xb_ref, tiles_a), g_ref[...]).astype(BF16)
    for c in range(o_ref.shape[1] // tn):
        o_ref[:, c * tn:(c + 1) * tn] = _dot(h, w_ref[:, c * tn:(c + 1) * tn]).astype(o_ref.dtype)


def norm_proj(xa, xb, g, w_bf16, tm=TOK_TILE, tn=512):
    d = xa.shape[1]
    n = xa.shape[0] + xb.shape[0]
    tiles_a = xa.shape[0] // tm
    nout = w_bf16.shape[1]
    return pl.pallas_call(
        functools.partial(_norm_proj_kernel, tn=tn, tiles_a=tiles_a),
        grid=(n // tm,),
        in_specs=[*_two_part_specs(tm, d, tiles_a),
                  pl.BlockSpec((1, d), lambda i: (0, 0)),
                  pl.BlockSpec((d, nout), lambda i: (0, 0))],
        out_specs=pl.BlockSpec((tm, nout), lambda i: (i, 0)),
        out_shape=jax.ShapeDtypeStruct((n, nout), BF16),
        compiler_params=_cparams("parallel"),
        name="norm_proj_ab",
    )(xa, xb, g.reshape(1, d), w_bf16)


def _norm_proj_cd_kernel(x_ref, g_ref, w_ref, dft_ref, zc_ref, wv_ref):
    h = _rms(x_ref[...], g_ref[...]).astype(BF16)
    for c in range(2):
        zc_ref[:, c * HALF:(c + 1) * HALF] = _dot(h, w_ref[:, c * HALF:(c + 1) * HALF])
    d = _dot(h, w_ref[:, 2 * HALF:3 * HALF]).astype(BF16)
    for g in range(N_HEADS):
        r = _dot(d[:, g * HEAD:(g + 1) * HEAD], dft_ref[...])
        wv_ref[:, g * HEAD:(g + 1) * HEAD] = r[:, :HEAD].astype(BF16)
        wv_ref[:, HALF + g * HEAD:HALF + (g + 1) * HEAD] = r[:, HEAD:].astype(BF16)


def norm_proj_cd(x, g, w_bf16, dft_c, tm=TOK_TILE):
    n, d = x.shape
    return pl.pallas_call(
        _norm_proj_cd_kernel,
        grid=(n // tm,),
        in_specs=[pl.BlockSpec((tm, d), lambda i: (i, 0)),
                  pl.BlockSpec((1, d), lambda i: (0, 0)),
                  pl.BlockSpec((d, 3 * HALF), lambda i: (0, 0)),
                  pl.BlockSpec((HEAD, 2 * HEAD), lambda i: (0, 0))],
        out_specs=[pl.BlockSpec((tm, 2 * HALF), lambda i: (i, 0)),
                   pl.BlockSpec((tm, 2 * HALF), lambda i: (i, 0))],
        out_shape=[jax.ShapeDtypeStruct((n, 2 * HALF), F32),
                   jax.ShapeDtypeStruct((n, 2 * HALF), BF16)],
        compiler_params=_cparams("parallel"),
        name="norm_proj_cd",
    )(x, g.reshape(1, d), w_bf16, dft_c)


def _hgrn_kernel(reset_ref, q_ref, f_ref, v_ref, lbl_ref, o_ref, st0, st1, st2, st3, oi_ref, qe_ref, u_ref, d_ref,
                 *, reverse, layer, tt):
    c, sub = HGRN_CHUNK, HGRN_SUB
    n_sub = c // sub
    n_chunks = tt // c
    states = (st0, st1, st2, st3)

    @pl.when(reset_ref[pl.program_id(0)] == 1)
    def _():
        for st in states:
            st[...] = jnp.zeros_like(st)

    lg = lbl_ref[...]
    e = jnp.exp(lg - jnp.max(lg, axis=0, keepdims=True))
    p = e / jnp.sum(e, axis=0, keepdims=True)
    lb_all = p[0:1, :]
    for r in range(1, layer + 1):
        lb_all = lb_all + p[r:r + 1, :]

    row = lax.broadcasted_iota(jnp.int32, (c, c), 0)
    col = lax.broadcasted_iota(jnp.int32, (c, c), 1)
    tri = jnp.where((col >= row) if reverse else (col <= row), 1.0, 0.0).astype(BF16)
    spans, keeps = [], []
    for i in range(n_sub):
        span = slice(i * sub, c) if reverse else slice(0, (i + 1) * sub)
        n_span = span.stop - span.start
        tr = lax.broadcasted_iota(jnp.int32, (sub, n_span), 0)
        sc = lax.broadcasted_iota(jnp.int32, (sub, n_span), 1)
        spans.append(span)
        keeps.append((sc >= tr) if reverse else (sc <= tr + i * sub))

    units = [(cj, h) for cj in range(HGRN_CHUNKS_PER_ITER) for h in range(N_HEADS)]

    def local_body(it, carry):
        ci = [it * HGRN_CHUNKS_PER_ITER + cj for cj in range(HGRN_CHUNKS_PER_ITER)]
        r0 = [pl.multiple_of(x * c, c) for x in ci]
        vals = {}
        for cj, h in units:
            cols = slice(h * HEAD, (h + 1) * HEAD)
            lb = lb_all[:, cols]
            q = _silu(q_ref[pl.ds(r0[cj], c), cols].astype(F32))
            v = v_ref[pl.ds(r0[cj], c), cols].astype(F32)
            f = lb + (1.0 - lb) * jax.nn.sigmoid(f_ref[pl.ds(r0[cj], c), cols].astype(F32))
            lf = jnp.log(f)
            lf_hi = lf.astype(BF16)
            lf_lo = (lf - lf_hi.astype(F32)).astype(BF16)
            vals[cj, h] = (q, v, 1.0 - f, _dot(tri, lf_hi) + _dot(tri, lf_lo))
        scores = {}
        for cj, h in units:
            cols = slice(h * HEAD, (h + 1) * HEAD)
            q, v, k, b = vals[cj, h]
            qe_ref[pl.ds(r0[cj], c), cols] = (q * jnp.exp(b)).astype(BF16)
            edge = b[0:1, :] if reverse else b[c - 1:c, :]
            ks = (k * jnp.exp(edge - b)).astype(BF16)
            u_ref[ci[cj] * N_HEADS + h] = _dot(v.T.astype(BF16), ks)
            d_ref[ci[cj] * N_HEADS + h] = jnp.broadcast_to(jnp.exp(edge), (8, HEAD))
            for i in range(n_sub):
                rows = slice(i * sub, (i + 1) * sub)
                if reverse:
                    anchor = b[(i + 1) * sub:(i + 1) * sub + 1, :] if i + 1 < n_sub else jnp.zeros((1, HEAD), F32)
                else:
                    anchor = b[i * sub - 1:i * sub, :] if i > 0 else jnp.zeros((1, HEAD), F32)
                qi = (q[rows] * jnp.exp(b[rows] - anchor)).astype(BF16)
                ki = (k[spans[i]] * jnp.exp(jnp.minimum(anchor - b[spans[i]], MAX_EXP))).astype(BF16)
                scores[cj, h, i] = _dot_nt(qi, ki)
        for cj, h in units:
            cols = slice(h * HEAD, (h + 1) * HEAD)
            vb = vals[cj, h][1].astype(BF16)
            for i in range(n_sub):
                a = jnp.where(keeps[i], scores[cj, h, i], 0.0).astype(BF16)
                oi_ref[pl.ds(r0[cj] + i * sub, sub), cols] = _dot(a, vb[spans[i]])
        return carry

    lax.fori_loop(0, n_chunks // HGRN_CHUNKS_PER_ITER, local_body, 0)

    cur = [st[...] for st in states]
    for ci in range(n_chunks):
        cc = (n_chunks - 1 - ci) if reverse else ci
        for h in range(N_HEADS):
            cols = slice(h * HEAD, (h + 1) * HEAD)
            o_ref[cc * c:(cc + 1) * c, cols] = (
                oi_ref[cc * c:(cc + 1) * c, cols]
                + _dot_nt(qe_ref[cc * c:(cc + 1) * c, cols], cur[h].astype(BF16))).astype(o_ref.dtype)
        cur = [cur[h] * d_ref[cc * N_HEADS + h][0:1, :] + u_ref[cc * N_HEADS + h] for h in range(N_HEADS)]
    for h in range(N_HEADS):
        states[h][...] = cur[h]


def hgrn_scan(z, lb_logits, resets, *, reverse, layer, f_col, tt=TOK_TILE):
    n = z.shape[0]
    nt = n // tt
    n_units = tt // HGRN_CHUNK * N_HEADS
    order = (lambda i, r: (nt - 1 - i, 0)) if reverse else (lambda i, r: (i, 0))
    blk = lambda cb: (lambda i, r: (order(i, r)[0], cb))
    grid_spec = pltpu.PrefetchScalarGridSpec(
        num_scalar_prefetch=1,
        grid=(nt,),
        in_specs=[pl.BlockSpec((tt, HALF), blk(0)),
                  pl.BlockSpec((tt, HALF), blk(f_col)),
                  pl.BlockSpec((tt, HALF), blk(3)),
                  pl.BlockSpec(lb_logits.shape, lambda i, r: (0, 0))],
        out_specs=pl.BlockSpec((tt, HALF), order),
        scratch_shapes=[pltpu.VMEM((HEAD, HEAD), F32)] * N_HEADS + [
            pltpu.VMEM((tt, HALF), F32),
            pltpu.VMEM((tt, HALF), BF16),
            pltpu.VMEM((n_units, HEAD, HEAD), F32),
            pltpu.VMEM((n_units, 8, HEAD), F32)],
    )
    return pl.pallas_call(
        functools.partial(_hgrn_kernel, reverse=reverse, layer=layer, tt=tt),
        grid_spec=grid_spec,
        out_shape=jax.ShapeDtypeStruct((n, HALF), BF16),
        compiler_params=_cparams("arbitrary"),
        name="hgrn_bwd" if reverse else "hgrn_fwd",
    )(resets, z, z, z, lb_logits)


def _mix_ab_kernel(xa_ref, xb_ref, of_ref, ob_ref, g_ref, u_ref, v_ref, onorm_ref, snorm_ref, wsp_ref, bsp_ref,
                   wout_ref, o_ref, *, tm, tiles_a):
    o = of_ref[...].astype(F32) + ob_ref[...].astype(F32)
    g = g_ref[...].astype(F32)
    parts = []
    for h in range(N_HEADS):
        cols = slice(h * HEAD, (h + 1) * HEAD)
        parts.append(_rms(o[:, cols], onorm_ref[:, cols]) * _silu(g[:, cols]))
    a = jnp.concatenate(parts, axis=1).astype(BF16)
    acc = _two_part_read(xa_ref, xb_ref, tiles_a) + _dot(a, wout_ref[0:HALF, :])

    u = _gelu_tanh(u_ref[...].astype(F32))
    vb = _rms(_gelu_tanh(v_ref[...].astype(F32)), snorm_ref[...]).astype(BF16)
    rows = []
    for c in range(tm // HEAD):
        mixed = []
        for gi in range(N_HEADS):
            vg = vb[c * HEAD:(c + 1) * HEAD, gi * HEAD:(gi + 1) * HEAD]
            mixed.append(_dot(wsp_ref[gi], vg) + bsp_ref[:, gi:gi + 1])
        rows.append(jnp.concatenate(mixed, axis=1))
    b = (u * jnp.concatenate(rows, axis=0)).astype(BF16)
    o_ref[...] = acc + _dot(b, wout_ref[HALF:2 * HALF, :])


def mix_ab(xa, xb, o_f, o_b, z, onorm, snorm, wsp_bf16, bsp_t, wout_bf16, tm=TOK_TILE):
    d = xa.shape[1]
    n = xa.shape[0] + xb.shape[0]
    tiles_a = xa.shape[0] // tm
    zblk = lambda cb: pl.BlockSpec((tm, HALF), lambda i: (i, cb))
    const = lambda shape: pl.BlockSpec(shape, lambda i: (0,) * len(shape))
    return pl.pallas_call(
        functools.partial(_mix_ab_kernel, tm=tm, tiles_a=tiles_a),
        grid=(n // tm,),
        in_specs=[*_two_part_specs(tm, d, tiles_a),
                  pl.BlockSpec((tm, HALF), lambda i: (i, 0)),
                  pl.BlockSpec((tm, HALF), lambda i: (i, 0)),
                  zblk(4), zblk(5), zblk(6),
                  const((1, HALF)), const((1, HALF)),
                  const((N_HEADS, HEAD, HEAD)), const((HEAD, N_HEADS)),
                  const((d, d))],
        out_specs=pl.BlockSpec((tm, d), lambda i: (i, 0)),
        out_shape=jax.ShapeDtypeStruct((n, d), F32),
        compiler_params=_cparams("parallel"),
        name="mix_ab",
    )(xa, xb, o_f, o_b, z, z, z, onorm.reshape(1, HALF), snorm.reshape(1, HALF), wsp_bf16, bsp_t, wout_bf16)


def _ffn_kernel(x_ref, g_ref, w1_ref, w3_ref, w2_ref, o_ref, *, tf):
    x = x_ref[...]
    h = _rms(x, g_ref[...]).astype(BF16)
    acc = x
    for c in range(w1_ref.shape[1] // tf):
        cols = slice(c * tf, (c + 1) * tf)
        act = (_silu(_dot(h, w1_ref[:, cols])) * _dot(h, w3_ref[:, cols])).astype(BF16)
        acc = acc + _dot(act, w2_ref[cols, :])
    o_ref[...] = acc


def ffn(x, g, w1, w3, w2, tm=TOK_TILE, tf=FFN_FTILE):
    n, d = x.shape
    f = w1.shape[1]
    const = lambda shape: pl.BlockSpec(shape, lambda i: (0,) * len(shape))
    return pl.pallas_call(
        functools.partial(_ffn_kernel, tf=tf),
        grid=(n // tm,),
        in_specs=[pl.BlockSpec((tm, d), lambda i: (i, 0)), const((1, d)),
                  const((d, f)), const((d, f)), const((f, d))],
        out_specs=pl.BlockSpec((tm, d), lambda i: (i, 0)),
        out_shape=jax.ShapeDtypeStruct((n, d), F32),
        compiler_params=_cparams("parallel"),
        name="ffn_dense",
    )(x, g.reshape(1, d), w1, w3, w2)


def _conv_kernel(first_ref, last_ref, a_ref, gt_ref, ap_ref, gp_ref, an_ref, gn_ref, w_ref, b_ref, lng_ref,
                 lnb_ref, o_ref, ypad_ref, *, ts, rb):
    i = pl.program_id(0)
    halo = CONV_HALO
    keep_prev = jnp.where(first_ref[i] == 1, 0.0, 1.0)
    keep_next = jnp.where(last_ref[i] == 1, 0.0, 1.0)
    ypad_ref[0:halo, :] = ap_ref[...] * jax.nn.sigmoid(gp_ref[...]) * keep_prev
    ypad_ref[halo:halo + ts, :] = a_ref[...] * jax.nn.sigmoid(gt_ref[...])
    ypad_ref[halo + ts:2 * halo + ts, :] = an_ref[...] * jax.nn.sigmoid(gn_ref[...]) * keep_next

    n_win = rb + 2 * halo

    def body(bi, carry):
        r0 = pl.multiple_of(bi * rb, rb)
        strips = []
        for s in range(HALF // HEAD):
            cols = slice(s * HEAD, (s + 1) * HEAD)
            win = ypad_ref[pl.ds(r0, n_win), cols]
            acc = jnp.zeros((rb, HEAD), F32) + b_ref[:, cols]
            for res in range(8):
                sh = win if res == 0 else pltpu.roll(win, n_win - res, axis=0)
                for j in range(CONV_K):
                    off = halo - CONV_PAD + j
                    if off % 8 == res:
                        base = off - res
                        acc = acc + w_ref[j:j + 1, cols] * sh[base:base + rb, :]
            strips.append(acc)
        acc = jnp.concatenate(strips, axis=1)
        mu = jnp.mean(acc, axis=-1, keepdims=True)
        xc = acc - mu
        var = jnp.mean(xc * xc, axis=-1, keepdims=True)
        y = xc * lax.rsqrt(var + EPS) * lng_ref[...] + lnb_ref[...]
        o_ref[pl.ds(r0, rb), :] = _silu(y).astype(o_ref.dtype)
        return carry

    lax.fori_loop(0, ts // rb, body, 0)


def conv_module(zc, firsts, lasts, w, b, ln_g, ln_b, ts=TOK_TILE, rb=64):
    n = zc.shape[0]
    nt = n // ts
    hb = ts // CONV_HALO
    n_hb = n // CONV_HALO
    main = lambda cb: pl.BlockSpec((ts, HALF), lambda i, f, l: (i, cb))
    prev = lambda cb: pl.BlockSpec((CONV_HALO, HALF), lambda i, f, l: (jnp.maximum(i * hb - 1, 0), cb))
    nxt = lambda cb: pl.BlockSpec((CONV_HALO, HALF), lambda i, f, l: (jnp.minimum((i + 1) * hb, n_hb - 1), cb))
    const = lambda shape: pl.BlockSpec(shape, lambda i, f, l: (0,) * len(shape))
    grid_spec = pltpu.PrefetchScalarGridSpec(
        num_scalar_prefetch=2,
        grid=(nt,),
        in_specs=[main(0), main(1), prev(0), prev(1), nxt(0), nxt(1),
                  const((CONV_K, HALF)), const((1, HALF)), const((1, HALF)), const((1, HALF))],
        out_specs=pl.BlockSpec((ts, HALF), lambda i, f, l: (i, 0)),
        scratch_shapes=[pltpu.VMEM((ts + 2 * CONV_HALO, HALF), F32)],
    )
    return pl.pallas_call(
        functools.partial(_conv_kernel, ts=ts, rb=rb),
        grid_spec=grid_spec,
        out_shape=jax.ShapeDtypeStruct((n, HALF), BF16),
        compiler_params=_cparams("parallel"),
        name="conv_module",
    )(firsts, lasts, zc, zc, zc, zc, zc, zc, w, b.reshape(1, HALF), ln_g.reshape(1, HALF), ln_b.reshape(1, HALF))


def _seq_dft_kernel(c_ref, s_ref, wv_ref, o_ref, *, scale):
    acc = _dot(c_ref[...], wv_ref[:, 0:HALF]) - _dot(s_ref[...], wv_ref[:, HALF:2 * HALF])
    o_ref[...] = (acc * scale).astype(o_ref.dtype)


def seq_dft(wv, cos_t, sin_t, row0, n_seq, seq_len):
    tmo = min(seq_len, max(256, SEQ_DFT_TILE_BYTES // (2 * seq_len)))
    nt = seq_len // tmo
    b0 = row0 // seq_len
    scale = 1.0 / float(np.sqrt(seq_len * HEAD))
    return pl.pallas_call(
        functools.partial(_seq_dft_kernel, scale=scale),
        grid=(n_seq, nt),
        in_specs=[pl.BlockSpec((tmo, seq_len), lambda b, i: (i, 0)),
                  pl.BlockSpec((tmo, seq_len), lambda b, i: (i, 0)),
                  pl.BlockSpec((seq_len, 2 * HALF), lambda b, i: (b0 + b, 0))],
        out_specs=pl.BlockSpec((tmo, HALF), lambda b, i: (b * nt + i, 0)),
        out_shape=jax.ShapeDtypeStruct((n_seq * seq_len, HALF), BF16),
        compiler_params=_cparams("parallel", "parallel"),
        name="seq_dft_%d" % seq_len,
    )(cos_t, sin_t, wv)


def _mix_cd_router_kernel(x_ref, c_ref, da_ref, db_ref, wout_ref, g_ref, rwh_ref, rwl_ref, strict_ref, x3_ref, h_ref,
                          route_ref, cnt_ref, carry_ref, *, tm, tiles_a):
    @pl.when(pl.program_id(0) == 0)
    def _():
        carry_ref[...] = jnp.zeros_like(carry_ref)

    d_mix = _two_part_read(da_ref, db_ref, tiles_a)
    x3 = x_ref[...] + _dot(c_ref[...], wout_ref[0:HALF, :]) + _dot(d_mix, wout_ref[HALF:2 * HALF, :])
    x3_ref[...] = x3
    h = _rms(x3, g_ref[...])

    h_hi = h.astype(BF16)
    h_lo = (h - h_hi.astype(F32)).astype(BF16)
    h_ref[...] = _pack_bf16_pairs(h_hi.astype(F32))
    logits = _dot(h_hi, rwh_ref[...]) + (_dot(h_lo, rwh_ref[...]) + _dot(h_hi, rwl_ref[...]))
    lane = lax.broadcasted_iota(jnp.int32, (tm, ROUTE_LANES), 1)
    neg = jnp.float32(-jnp.inf)
    logits = jnp.where(lane < N_EXPERTS, logits, neg)
    m1 = jnp.max(logits, axis=-1, keepdims=True)
    i1 = jnp.min(jnp.where(logits == m1, lane, ROUTE_LANES), axis=-1, keepdims=True)
    rest = jnp.where(lane == i1, neg, logits)
    m2 = jnp.max(rest, axis=-1, keepdims=True)
    i2 = jnp.min(jnp.where(rest == m2, lane, ROUTE_LANES), axis=-1, keepdims=True)
    e2 = jnp.exp(m2 - m1)
    g1 = 1.0 / (1.0 + e2)
    g2 = e2 / (1.0 + e2)

    onehot = jnp.where((lane == i1) | (lane == i2), 1.0, 0.0)
    before = _dot(strict_ref[...], onehot.astype(BF16)) + carry_ref[0:1, :]
    r1 = jnp.sum(jnp.where(lane == i1, before, 0.0), axis=-1, keepdims=True)
    r2 = jnp.sum(jnp.where(lane == i2, before, 0.0), axis=-1, keepdims=True)
    total = carry_ref[0:1, :] + jnp.sum(onehot, axis=0, keepdims=True)
    carry_ref[...] = jnp.broadcast_to(total, carry_ref.shape)
    cnt_ref[...] = jnp.broadcast_to(total, cnt_ref.shape)

    route = jnp.where(lane == 0, i1.astype(F32), 0.0)
    route = jnp.where(lane == 1, i2.astype(F32), route)
    route = jnp.where(lane == 2, g1, route)
    route = jnp.where(lane == 3, g2, route)
    route = jnp.where(lane == 4, r1, route)
    route = jnp.where(lane == 5, r2, route)
    route_ref[...] = route


def mix_cd_router(x, c_out, d_a, d_b, wout_bf16, g, rw_hi, rw_lo, tm=TOK_TILE):
    n, d = x.shape
    tiles_a = d_a.shape[0] // tm
    const = lambda shape: pl.BlockSpec(shape, lambda i: (0,) * len(shape))
    tok = lambda w: pl.BlockSpec((tm, w), lambda i: (i, 0))
    strict = jnp.tril(jnp.ones((tm, tm), BF16), -1)
    return pl.pallas_call(
        functools.partial(_mix_cd_router_kernel, tm=tm, tiles_a=tiles_a),
        grid=(n // tm,),
        in_specs=[tok(d), tok(HALF), *_two_part_specs(tm, HALF, tiles_a), const((d, d)), const((1, d)),
                  const((d, ROUTE_LANES)), const((d, ROUTE_LANES)), const((tm, tm))],
        out_specs=[tok(d), tok(d // 2), tok(ROUTE_LANES), const((8, ROUTE_LANES))],
        out_shape=[jax.ShapeDtypeStruct((n, d), F32), jax.ShapeDtypeStruct((n, d // 2), jnp.uint32),
                   jax.ShapeDtypeStruct((n, ROUTE_LANES), F32), jax.ShapeDtypeStruct((8, ROUTE_LANES), F32)],
        scratch_shapes=[pltpu.VMEM((8, ROUTE_LANES), F32)],
        compiler_params=_cparams("arbitrary"),
        name="mix_cd_router",
    )(x, c_out, d_a, d_b, wout_bf16, g.reshape(1, d), rw_hi, rw_lo, strict)


_SC_WORKERS = SC_CORES * SC_SUBCORES


def _sc_mesh():
    return plsc.VectorSubcoreMesh(core_axis_name="c", subcore_axis_name="s")


def _sc_worker_id():
    return lax.axis_index("s") * SC_CORES + lax.axis_index("c")


def sc_dispatch(rows, dest0, dest1, pad_slots, n_slots, chunk=SC_ROW_CHUNK):
    n, w = rows.shape
    n_pad = pad_slots.shape[0]
    per_worker = n // _SC_WORKERS
    pad_per_worker = n_pad // _SC_WORKERS
    assert per_worker * _SC_WORKERS == n and per_worker % chunk == 0 and chunk % 8 == 0
    assert pad_per_worker * _SC_WORKERS == n_pad and pad_per_worker % chunk == 0

    @functools.partial(
        pl.kernel, mesh=_sc_mesh(),
        out_type=jax.ShapeDtypeStruct((n_slots, w), rows.dtype),
        scratch_types=[pltpu.VMEM((chunk,), jnp.int32), pltpu.VMEM((chunk, w), rows.dtype),
                       pltpu.SemaphoreType.DMA],
    )
    def scatter(rows_hbm, d0_hbm, d1_hbm, pad_hbm, zeros_hbm, out_hbm, idx_v, rows_v, sem):
        worker = _sc_worker_id()

        @pl.loop(0, per_worker // chunk)
        def _(c):
            off = worker * per_worker + c * chunk
            pltpu.sync_copy(rows_hbm.at[pl.ds(off, chunk)], rows_v)
            for d_hbm in (d0_hbm, d1_hbm):
                pltpu.sync_copy(d_hbm.at[pl.ds(off, chunk)], idx_v)
                pltpu.async_copy(rows_v, out_hbm.at[idx_v], sem).wait()

        pltpu.sync_copy(zeros_hbm, rows_v)

        @pl.loop(0, pad_per_worker // chunk)
        def _(c):
            off = worker * pad_per_worker + c * chunk
            pltpu.sync_copy(pad_hbm.at[pl.ds(off, chunk)], idx_v)
            pltpu.async_copy(rows_v, out_hbm.at[idx_v], sem).wait()

    return scatter(rows, dest0, dest1, pad_slots, jnp.zeros((chunk, w), rows.dtype))


def sc_row_gather(table, idx, chunk=SC_ROW_CHUNK):
    b, w = idx.shape[0], table.shape[1]
    per_worker = b // _SC_WORKERS
    assert per_worker * _SC_WORKERS == b and per_worker % chunk == 0 and chunk % 8 == 0

    @functools.partial(
        pl.kernel, mesh=_sc_mesh(),
        out_type=jax.ShapeDtypeStruct((b, w), table.dtype),
        scratch_types=[pltpu.VMEM((chunk,), jnp.int32), pltpu.VMEM((chunk, w), table.dtype),
                       pltpu.SemaphoreType.DMA],
    )
    def gather(table_hbm, idx_hbm, out_hbm, idx_v, rows_v, sem):
        worker = _sc_worker_id()

        @pl.loop(0, per_worker // chunk)
        def _(c):
            off = worker * per_worker + c * chunk
            pltpu.sync_copy(idx_hbm.at[pl.ds(off, chunk)], idx_v)
            pltpu.async_copy(table_hbm.at[idx_v], rows_v, sem).wait()
            pltpu.sync_copy(rows_v, out_hbm.at[pl.ds(off, chunk)])

    return gather(table, idx)


def _moe_ffn_kernel(be_ref, nu_ref, x_ref, w1_ref, w3_ref, w2_ref, o_ref, xbf_ref, acc_ref, *, rows_sub, cols_sub, nf):
    del be_ref
    b = pl.program_id(0)
    f = pl.program_id(1)
    active = b < nu_ref[0]
    tmb = x_ref.shape[0]
    tf = w1_ref.shape[2]
    half = x_ref.shape[1]

    @pl.when(active & (f == 0))
    def _():
        hi, lo = _unpack_bf16_pairs(x_ref[...])
        xbf_ref[:, 0:half] = hi.astype(BF16)
        xbf_ref[:, half:2 * half] = lo.astype(BF16)

    def partial_sums(first, last):
        for r in range(tmb // rows_sub):
            rows = slice(r * rows_sub, (r + 1) * rows_sub)
            xr = xbf_ref[rows, :]
            acc = jnp.zeros((rows_sub, 2 * half), F32) if first else acc_ref[rows, :]
            for c in range(tf // cols_sub):
                cols = slice(c * cols_sub, (c + 1) * cols_sub)
                act = (_silu(_dot(xr, w1_ref[0, :, cols])) * _dot(xr, w3_ref[0, :, cols])).astype(BF16)
                acc = acc + _dot(act, w2_ref[0, cols, :])
            if last:
                o_ref[rows, :] = _pack_bf16_pairs(acc.astype(BF16).astype(F32))
            else:
                acc_ref[rows, :] = acc

    @pl.when(active & (f == 0))
    def _():
        partial_sums(True, False)

    if nf > 2:
        @pl.when(active & (f > 0) & (f < nf - 1))
        def _():
            partial_sums(False, False)

    @pl.when(active & (f == nf - 1))
    def _():
        partial_sums(False, True)

    @pl.when(jnp.logical_not(active) & (f == 0))
    def _():
        o_ref[...] = jnp.zeros_like(o_ref)


def moe_ffn(xb, block_e, n_used, w1, w3, w2, tmb=MOE_BLOCK, tf=MOE_FTILE):
    n_slots, half = xb.shape
    d = 2 * half
    nb = n_slots // tmb
    nf = w1.shape[2] // tf
    assert nf >= 2, "the first hidden-column step initialises the sum and the last one writes the block"

    def bclamp(b, nu):
        return jnp.minimum(b, nu[0] - 1)

    def fclamp(b, f, nu):
        return jnp.where(b < nu[0], f, nf - 1)

    grid_spec = pltpu.PrefetchScalarGridSpec(
        num_scalar_prefetch=2,
        grid=(nb, nf),
        in_specs=[pl.BlockSpec((tmb, half), lambda b, f, be, nu: (bclamp(b, nu), 0)),
                  pl.BlockSpec((1, d, tf), lambda b, f, be, nu: (be[bclamp(b, nu)], 0, fclamp(b, f, nu))),
                  pl.BlockSpec((1, d, tf), lambda b, f, be, nu: (be[bclamp(b, nu)], 0, fclamp(b, f, nu))),
                  pl.BlockSpec((1, tf, d), lambda b, f, be, nu: (be[bclamp(b, nu)], fclamp(b, f, nu), 0))],
        out_specs=pl.BlockSpec((tmb, half), lambda b, f, be, nu: (b, 0)),
        scratch_shapes=[pltpu.VMEM((tmb, d), BF16), pltpu.VMEM((tmb, d), F32)],
    )
    return pl.pallas_call(
        functools.partial(_moe_ffn_kernel, rows_sub=MOE_ROWS_SUB, cols_sub=MOE_COLS_SUB, nf=nf),
        grid_spec=grid_spec,
        out_shape=jax.ShapeDtypeStruct((n_slots, half), jnp.uint32),
        compiler_params=_cparams("arbitrary", "arbitrary"),
        name="moe_ffn",
    )(block_e, n_used, xb, w1, w3, w2)


def _finish_kernel(x_ref, route_ref, g_ref, y0_ref, y1_ref, oa_ref, ob_ref, *, tiles_a):
    half = y0_ref.shape[1]
    route = route_ref[...]
    g0, g1 = route[:, 2:3], route[:, 3:4]
    hi0, lo0 = _unpack_bf16_pairs(y0_ref[...])
    hi1, lo1 = _unpack_bf16_pairs(y1_ref[...])
    ya = x_ref[:, 0:half] + g0 * hi0 + g1 * hi1
    yb = x_ref[:, half:2 * half] + g0 * lo0 + g1 * lo1
    ms = (jnp.sum(ya * ya, axis=-1, keepdims=True) + jnp.sum(yb * yb, axis=-1, keepdims=True)) / (2 * half)
    inv = lax.rsqrt(ms + EPS)

    for o_ref, mine in ((oa_ref, pl.program_id(0) < tiles_a), (ob_ref, pl.program_id(0) >= tiles_a)):
        @pl.when(mine)
        def _():
            o_ref[:, 0:half] = ya * inv * g_ref[:, 0:half]
            o_ref[:, half:2 * half] = yb * inv * g_ref[:, half:2 * half]


def moe_finish(x, route, y_rows, g, n_a, tm=TOK_TILE):
    n, d = x.shape
    tiles_a = n_a // tm
    n_tiles = n // tm
    return pl.pallas_call(
        functools.partial(_finish_kernel, tiles_a=tiles_a),
        grid=(n_tiles,),
        in_specs=[pl.BlockSpec((tm, d), lambda i: (i, 0)),
                  pl.BlockSpec((tm, ROUTE_LANES), lambda i: (i, 0)),
                  pl.BlockSpec((1, d), lambda i: (0, 0)),
                  pl.BlockSpec((tm, d // 2), lambda i: (i, 0)),
                  pl.BlockSpec((tm, d // 2), lambda i: (i + n_tiles, 0))],
        out_specs=list(_two_part_specs(tm, d, tiles_a)),
        out_shape=[jax.ShapeDtypeStruct((n_a, d), F32), jax.ShapeDtypeStruct((n - n_a, d), F32)],
        compiler_params=_cparams("arbitrary"),
        name="moe_finish",
    )(x, route, g.reshape(1, d), y_rows, y_rows)


def _tile_flags(seq_lens, tile):
    firsts, lasts = [], []
    for length in seq_lens:
        k = length // tile
        firsts += [1] + [0] * (k - 1)
        lasts += [0] * (k - 1) + [1]
    return np.asarray(firsts, np.int32), np.asarray(lasts, np.int32)


def _dft_table_kernel(cphi_ref, sphi_ref, cth_ref, sth_ref, c_ref, s_ref):
    cphi, sphi = cphi_ref[...], sphi_ref[...]
    cth, sth = cth_ref[0], sth_ref[0]
    c_ref[...] = (cphi * cth - sphi * sth).astype(BF16)
    s_ref[...] = (sphi * cth + cphi * sth).astype(BF16)


def _angles(rows, t):
    k = lax.broadcasted_iota(jnp.int32, (rows.shape[0], t), 1)
    ang = ((rows[:, None] * k) % t).astype(F32) * (2.0 * np.pi / t)
    return jnp.cos(ang), jnp.sin(ang)


def _dft_tables(t):
    rows = min(DFT_GEN_ROWS, t)
    nt = t // rows
    cphi, sphi = _angles(jnp.arange(rows, dtype=jnp.int32), t)
    cth, sth = _angles(jnp.arange(nt, dtype=jnp.int32) * rows, t)
    tile = pl.BlockSpec((rows, t), lambda i: (0, 0))
    vec = pl.BlockSpec((1, 1, t), lambda i: (i, 0, 0))
    out = pl.BlockSpec((rows, t), lambda i: (i, 0))
    return pl.pallas_call(
        _dft_table_kernel,
        grid=(nt,),
        in_specs=[tile, tile, vec, vec],
        out_specs=[out, out],
        out_shape=[jax.ShapeDtypeStruct((t, t), BF16)] * 2,
        compiler_params=_cparams("parallel"),
        name="dft_table_%d" % t,
    )(cphi, sphi, cth.reshape(nt, 1, t), sth.reshape(nt, 1, t))


def kernel(x_prompt, x_sample, norm_mix, norm_ffn, norm_final, w_in_ab, hgrn_lb_logits, hgrn_out_norm, sgu_norm,
           sgu_w, sgu_b, w_out_ab, w_in_cd, conv_w, conv_b, conv_ln_g, conv_ln_b, w_out_cd, ffn_w1, ffn_w3, ffn_w2,
           router_w, moe_w1, moe_w3, moe_w2):
    bp, tp, d = x_prompt.shape
    bs, ts_, _ = x_sample.shape
    n_p, n_s = bp * tp, bs * ts_
    n = n_p + n_s
    seq_lens = [tp] * bp + [ts_] * bs
    tile = TOK_TILE
    firsts, lasts = _tile_flags(seq_lens, tile)
    depth = norm_mix.shape[0]
    assert depth == 2, "the layer schedule below is the two-layer trunk"
    xa, xb_in = x_prompt.reshape(n_p, d), x_sample.reshape(n_s, d)

    for layer in range(depth):
        j = layer // 2
        if layer % 2 == 0:
            z = norm_proj(xa, xb_in, norm_mix[layer], w_in_ab[j].astype(BF16))
            o_f = hgrn_scan(z, hgrn_lb_logits, jnp.asarray(firsts), reverse=False, layer=layer, f_col=1)
            o_b = hgrn_scan(z, hgrn_lb_logits, jnp.asarray(lasts[::-1].copy()), reverse=True, layer=layer, f_col=2)
            x = mix_ab(xa, xb_in, o_f, o_b, z, hgrn_out_norm[j], sgu_norm[j], sgu_w[j].astype(BF16), sgu_b[j].T,
                       w_out_ab[j].astype(BF16))
            x = ffn(x, norm_ffn[layer], ffn_w1[j].astype(BF16), ffn_w3[j].astype(BF16), ffn_w2[j].astype(BF16))
        else:
            ang = (lax.broadcasted_iota(jnp.int32, (HEAD, HEAD), 0) * lax.broadcasted_iota(jnp.int32, (HEAD, HEAD), 1)
                   % HEAD).astype(F32) * (2.0 * np.pi / HEAD)
            dft_c = jnp.concatenate([jnp.cos(ang), jnp.sin(ang)], axis=1).astype(BF16)
            zc, wv = norm_proj_cd(x, norm_mix[layer], w_in_cd[j].astype(BF16), dft_c)
            c_out = conv_module(zc, jnp.asarray(firsts), jnp.asarray(lasts), conv_w[j], conv_b[j], conv_ln_g[j],
                                conv_ln_b[j])
            cos_p, sin_p = _dft_tables(tp)
            cos_s, sin_s = _dft_tables(ts_)
            d_p = seq_dft(wv, cos_p, sin_p, 0, bp, tp)
            d_s = seq_dft(wv, cos_s, sin_s, n_p, bs, ts_)

            rw = jnp.zeros((d, ROUTE_LANES), F32).at[:, :N_EXPERTS].set(router_w[j])
            rw_hi = rw.astype(BF16)
            rw_lo = (rw - rw_hi.astype(F32)).astype(BF16)
            x3, h, route, counts = mix_cd_router(x, c_out, d_p, d_s, w_out_cd[j].astype(BF16), norm_ffn[layer],
                                                 rw_hi, rw_lo)

            tmb = MOE_BLOCK
            n_blocks = (2 * n + tmb - 1) // tmb + N_EXPERTS
            cnt = counts[0, :N_EXPERTS].astype(jnp.int32)
            padded = (cnt + tmb - 1) // tmb * tmb
            pend = jnp.cumsum(padded)
            pstart = pend - padded
            e_idx = route[:, 0:2].astype(jnp.int32)
            dest = pstart[e_idx] + route[:, 4:6].astype(jnp.int32)
            dest0, dest1 = dest[:, 0], dest[:, 1]
            block_start = jnp.arange(n_blocks, dtype=jnp.int32) * tmb
            block_e = jnp.minimum(jnp.sum((pend[None, :] <= block_start[:, None]).astype(jnp.int32), axis=1),
                                  N_EXPERTS - 1)
            n_used = (pend[-1] // tmb).astype(jnp.int32).reshape(1)
            n_slots = n_blocks * tmb
            jpad = jnp.arange(tmb, dtype=jnp.int32)[None, :]
            pad_slots = jnp.where(jpad < (padded - cnt)[:, None], (pstart + cnt)[:, None] + jpad,
                                  n_slots - 1).reshape(N_EXPERTS * tmb)

            xb = sc_dispatch(h, dest0, dest1, pad_slots, n_slots)
            yb = moe_ffn(xb, block_e, n_used, moe_w1[j].astype(BF16), moe_w3[j].astype(BF16),
                         moe_w2[j].astype(BF16))
            y_rows = sc_row_gather(yb, jnp.concatenate([dest0, dest1]))
            out_p, out_s = moe_finish(x3, route, y_rows, norm_final, n_p)

    return out_p.reshape(bp, tp, d), out_s.reshape(bs, ts_, d)
```

```python
import functools

import numpy as np
import jax
import jax.numpy as jnp
from jax import lax
from jax.experimental import pallas as pl
from jax.experimental.pallas import tpu as pltpu
from jax.experimental.pallas import tpu_sc as plsc

F32 = jnp.float32
BF16 = jnp.bfloat16
EPS = 1e-6

D_MODEL = 1024
HALF = 512
HEAD = 128
N_HEADS = 4
HGRN_CHUNK = 64
HGRN_SUB = 16
HGRN_CHUNKS_PER_ITER = 4
CONV_K = 31
CONV_PAD = 15
CONV_HALO = 16
N_EXPERTS = 8
ROUTE_LANES = 128
MAX_EXP = 80.0

VMEM_LIMIT_BYTES = 56 * 1024 * 1024

TOK_TILE = 512
MOE_BLOCK = 1024
MOE_FTILE = 1792
MOE_ROWS_SUB = 512
MOE_COLS_SUB = 256
FFN_FTILE = 256
SC_CORES = 2
SC_SUBCORES = 16
SC_ROW_CHUNK = 128
SEQ_DFT_TILE_BYTES = 4 * 1024 * 1024
DFT_GEN_ROWS = 128


def _dot(a, b):
    return jnp.dot(a, b, preferred_element_type=F32)


def _dot_nt(a, b):
    return lax.dot_general(a, b, (((1,), (1,)), ((), ())), preferred_element_type=F32)


def _cparams(*sem):
    return pltpu.CompilerParams(dimension_semantics=sem, vmem_limit_bytes=VMEM_LIMIT_BYTES)


def _rms(x, g):
    ms = jnp.mean(x * x, axis=-1, keepdims=True)
    return x * lax.rsqrt(ms + EPS) * g


def _silu(x):
    return x * jax.nn.sigmoid(x)


def _pack_bf16_pairs(x):
    w = x.shape[1] // 2
    bits = lax.bitcast_convert_type(x, jnp.uint32)
    return (bits[:, :w] & jnp.uint32(0xFFFF0000)) | (bits[:, w:] >> 16)


def _unpack_bf16_pairs(words):
    hi = lax.bitcast_convert_type(words & jnp.uint32(0xFFFF0000), F32)
    lo = lax.bitcast_convert_type(words << 16, F32)
    return hi, lo


def _gelu_tanh(x):
    return 0.5 * x * (1.0 + jnp.tanh(0.7978845608028654 * (x + 0.044715 * (x * x * x))))


def _two_part_specs(tm, d, tiles_a):
    return (pl.BlockSpec((tm, d), lambda i: (jnp.minimum(i, tiles_a - 1), 0)),
            pl.BlockSpec((tm, d), lambda i: (jnp.maximum(i - tiles_a, 0), 0)))


def _two_part_read(xa_ref, xb_ref, tiles_a):
    return jnp.where(pl.program_id(0) < tiles_a, xa_ref[...], xb_ref[...])


def _norm_proj_kernel(xa_ref, xb_ref, g_ref, w_ref, *rest, tn, tiles_a):
    o_ref = rest[-1]
    h = _rms(_two_part_read(xa_ref, xb_ref, tiles_a), g_ref[...]).astype(BF16)
    for c in range(o_ref.shape[1] // tn):
        o_ref[:, c * tn:(c + 1) * tn] = _dot(h, w_ref[:, c * tn:(c + 1) * tn]).astype(o_ref.dtype)


def norm_proj(xa, xb, g, w_bf16, ready_first=(), tm=TOK_TILE, tn=512):
    d = xa.shape[1]
    n = xa.shape[0] + xb.shape[0]
    tiles_a = xa.shape[0] // tm
    nout = w_bf16.shape[1]
    return pl.pallas_call(
        functools.partial(_norm_proj_kernel, tn=tn, tiles_a=tiles_a),
        grid=(n // tm,),
        in_specs=[*_two_part_specs(tm, d, tiles_a),
                  pl.BlockSpec((1, d), lambda i: (0, 0)),
                  pl.BlockSpec((d, nout), lambda i: (0, 0)),
                  *[pl.BlockSpec(memory_space=pl.ANY) for _ in ready_first]],
        out_specs=pl.BlockSpec((tm, nout), lambda i: (i, 0)),
        out_shape=jax.ShapeDtypeStruct((n, nout), BF16),
        compiler_params=_cparams("parallel"),
        name="norm_proj_ab",
    )(xa, xb, g.reshape(1, d), w_bf16, *ready_first)


def _norm_proj_cd_kernel(x_ref, g_ref, w_ref, dft_ref, zc_ref, wv_ref):
    h = _rms(x_ref[...], g_ref[...]).astype(BF16)
    for c in range(2):
        zc_ref[:, c * HALF:(c + 1) * HALF] = _dot(h, w_ref[:, c * HALF:(c + 1) * HALF])
    d = _dot(h, w_ref[:, 2 * HALF:3 * HALF]).astype(BF16)
    for g in range(N_HEADS):
        r = _dot(d[:, g * HEAD:(g + 1) * HEAD], dft_ref[...])
        wv_ref[:, g * HEAD:(g + 1) * HEAD] = r[:, :HEAD].astype(BF16)
        wv_ref[:, HALF + g * HEAD:HALF + (g + 1) * HEAD] = r[:, HEAD:].astype(BF16)


def norm_proj_cd(x, g, w_bf16, dft_c, tm=TOK_TILE):
    n, d = x.shape
    return pl.pallas_call(
        _norm_proj_cd_kernel,
        grid=(n // tm,),
        in_specs=[pl.BlockSpec((tm, d), lambda i: (i, 0)),
                  pl.BlockSpec((1, d), lambda i: (0, 0)),
                  pl.BlockSpec((d, 3 * HALF), lambda i: (0, 0)),
                  pl.BlockSpec((HEAD, 2 * HEAD), lambda i: (0, 0))],
        out_specs=[pl.BlockSpec((tm, 2 * HALF), lambda i: (i, 0)),
                   pl.BlockSpec((tm, 2 * HALF), lambda i: (i, 0))],
        out_shape=[jax.ShapeDtypeStruct((n, 2 * HALF), F32),
                   jax.ShapeDtypeStruct((n, 2 * HALF), BF16)],
        compiler_params=_cparams("parallel"),
        name="norm_proj_cd",
    )(x, g.reshape(1, d), w_bf16, dft_c)


def _hgrn_kernel(reset_ref, q_ref, f_ref, v_ref, lbl_ref, o_ref, st0, st1, st2, st3, oi_ref, qe_ref, u_ref, d_ref,
                 *, reverse, layer, tt):
    c, sub = HGRN_CHUNK, HGRN_SUB
    n_sub = c // sub
    n_chunks = tt // c
    states = (st0, st1, st2, st3)

    @pl.when(reset_ref[pl.program_id(0)] == 1)
    def _():
        for st in states:
            st[...] = jnp.zeros_like(st)

    lg = lbl_ref[...]
    e = jnp.exp(lg - jnp.max(lg, axis=0, keepdims=True))
    p = e / jnp.sum(e, axis=0, keepdims=True)
    lb_all = p[0:1, :]
    for r in range(1, layer + 1):
        lb_all = lb_all + p[r:r + 1, :]

    row = lax.broadcasted_iota(jnp.int32, (c, c), 0)
    col = lax.broadcasted_iota(jnp.int32, (c, c), 1)
    tri = jnp.where((col >= row) if reverse else (col <= row), 1.0, 0.0).astype(BF16)
    spans, keeps = [], []
    for i in range(n_sub):
        span = slice(i * sub, c) if reverse else slice(0, (i + 1) * sub)
        n_span = span.stop - span.start
        tr = lax.broadcasted_iota(jnp.int32, (sub, n_span), 0)
        sc = lax.broadcasted_iota(jnp.int32, (sub, n_span), 1)
        spans.append(span)
        keeps.append((sc >= tr) if reverse else (sc <= tr + i * sub))

    units = [(cj, h) for cj in range(HGRN_CHUNKS_PER_ITER) for h in range(N_HEADS)]

    def local_body(it, carry):
        ci = [it * HGRN_CHUNKS_PER_ITER + cj for cj in range(HGRN_CHUNKS_PER_ITER)]
        r0 = [pl.multiple_of(x * c, c) for x in ci]
        vals = {}
        for cj, h in units:
            cols = slice(h * HEAD, (h + 1) * HEAD)
            lb = lb_all[:, cols]
            q = _silu(q_ref[pl.ds(r0[cj], c), cols].astype(F32))
            v = v_ref[pl.ds(r0[cj], c), cols].astype(F32)
            f = lb + (1.0 - lb) * jax.nn.sigmoid(f_ref[pl.ds(r0[cj], c), cols].astype(F32))
            lf = jnp.log(f)
            lf_hi = lf.astype(BF16)
            lf_lo = (lf - lf_hi.astype(F32)).astype(BF16)
            vals[cj, h] = (q, v, 1.0 - f, _dot(tri, lf_hi) + _dot(tri, lf_lo))
        scores = {}
        for cj, h in units:
            cols = slice(h * HEAD, (h + 1) * HEAD)
            q, v, k, b = vals[cj, h]
            qe_ref[pl.ds(r0[cj], c), cols] = (q * jnp.exp(b)).astype(BF16)
            edge = b[0:1, :] if reverse else b[c - 1:c, :]
            ks = (k * jnp.exp(edge - b)).astype(BF16)
            u_ref[ci[cj] * N_HEADS + h] = _dot(v.T.astype(BF16), ks)
            d_ref[ci[cj] * N_HEADS + h] = jnp.broadcast_to(jnp.exp(edge), (8, HEAD))
            for i in range(n_sub):
                rows = slice(i * sub, (i + 1) * sub)
                if reverse:
                    anchor = b[(i + 1) * sub:(i + 1) * sub + 1, :] if i + 1 < n_sub else jnp.zeros((1, HEAD), F32)
                else:
                    anchor = b[i * sub - 1:i * sub, :] if i > 0 else jnp.zeros((1, HEAD), F32)
                qi = (q[rows] * jnp.exp(b[rows] - anchor)).astype(BF16)
                ki = (k[spans[i]] * jnp.exp(jnp.minimum(anchor - b[spans[i]], MAX_EXP))).astype(BF16)
                scores[cj, h, i] = _dot_nt(qi, ki)
        for cj, h in units:
            cols = slice(h * HEAD, (h + 1) * HEAD)
            vb = vals[cj, h][1].astype(BF16)
            for i in range(n_sub):
                a = jnp.where(keeps[i], scores[cj, h, i], 0.0).astype(BF16)
                oi_ref[pl.ds(r0[cj] + i * sub, sub), cols] = _dot(a, vb[spans[i]])
        return carry

    lax.fori_loop(0, n_chunks // HGRN_CHUNKS_PER_ITER, local_body, 0)

    cur = [st[...] for st in states]
    for ci in range(n_chunks):
        cc = (n_chunks - 1 - ci) if reverse else ci
        for h in range(N_HEADS):
            cols = slice(h * HEAD, (h + 1) * HEAD)
            o_ref[cc * c:(cc + 1) * c, cols] = (
                oi_ref[cc * c:(cc + 1) * c, cols]
                + _dot_nt(qe_ref[cc * c:(cc + 1) * c, cols], cur[h].astype(BF16))).astype(o_ref.dtype)
        cur = [cur[h] * d_ref[cc * N_HEADS + h][0:1, :] + u_ref[cc * N_HEADS + h] for h in range(N_HEADS)]
    for h in range(N_HEADS):
        states[h][...] = cur[h]


def hgrn_scan(z, lb_logits, resets, *, reverse, layer, f_col, tt=TOK_TILE):
    n = z.shape[0]
    nt = n // tt
    n_units = tt // HGRN_CHUNK * N_HEADS
    order = (lambda i, r: (nt - 1 - i, 0)) if reverse else (lambda i, r: (i, 0))
    blk = lambda cb: (lambda i, r: (order(i, r)[0], cb))
    grid_spec = pltpu.PrefetchScalarGridSpec(
        num_scalar_prefetch=1,
        grid=(nt,),
        in_specs=[pl.BlockSpec((tt, HALF), blk(0)),
                  pl.BlockSpec((tt, HALF), blk(f_col)),
                  pl.BlockSpec((tt, HALF), blk(3)),
                  pl.BlockSpec(lb_logits.shape, lambda i, r: (0, 0))],
        out_specs=pl.BlockSpec((tt, HALF), order),
        scratch_shapes=[pltpu.VMEM((HEAD, HEAD), F32)] * N_HEADS + [
            pltpu.VMEM((tt, HALF), F32),
            pltpu.VMEM((tt, HALF), BF16),
            pltpu.VMEM((n_units, HEAD, HEAD), F32),
            pltpu.VMEM((n_units, 8, HEAD), F32)],
    )
    return pl.pallas_call(
        functools.partial(_hgrn_kernel, reverse=reverse, layer=layer, tt=tt),
        grid_spec=grid_spec,
        out_shape=jax.ShapeDtypeStruct((n, HALF), BF16),
        compiler_params=_cparams("arbitrary"),
        name="hgrn_bwd" if reverse else "hgrn_fwd",
    )(resets, z, z, z, lb_logits)


def _mix_ab_kernel(xa_ref, xb_ref, of_ref, ob_ref, g_ref, u_ref, v_ref, onorm_ref, snorm_ref, wsp_ref, bsp_ref,
                   wout_ref, o_ref, *, tm, tiles_a):
    o = of_ref[...].astype(F32) + ob_ref[...].astype(F32)
    g = g_ref[...].astype(F32)
    parts = []
    for h in range(N_HEADS):
        cols = slice(h * HEAD, (h + 1) * HEAD)
        parts.append(_rms(o[:, cols], onorm_ref[:, cols]) * _silu(g[:, cols]))
    a = jnp.concatenate(parts, axis=1).astype(BF16)
    acc = _two_part_read(xa_ref, xb_ref, tiles_a) + _dot(a, wout_ref[0:HALF, :])

    u = _gelu_tanh(u_ref[...].astype(F32))
    vb = _rms(_gelu_tanh(v_ref[...].astype(F32)), snorm_ref[...]).astype(BF16)
    rows = []
    for c in range(tm // HEAD):
        mixed = []
        for gi in range(N_HEADS):
            vg = vb[c * HEAD:(c + 1) * HEAD, gi * HEAD:(gi + 1) * HEAD]
            mixed.append(_dot(wsp_ref[gi], vg) + bsp_ref[:, gi:gi + 1])
        rows.append(jnp.concatenate(mixed, axis=1))
    b = (u * jnp.concatenate(rows, axis=0)).astype(BF16)
    o_ref[...] = acc + _dot(b, wout_ref[HALF:2 * HALF, :])


def mix_ab(xa, xb, o_f, o_b, z, onorm, snorm, wsp_bf16, bsp_t, wout_bf16, tm=TOK_TILE):
    d = xa.shape[1]
    n = xa.shape[0] + xb.shape[0]
    tiles_a = xa.shape[0] // tm
    zblk = lambda cb: pl.BlockSpec((tm, HALF), lambda i: (i, cb))
    const = lambda shape: pl.BlockSpec(shape, lambda i: (0,) * len(shape))
    return pl.pallas_call(
        functools.partial(_mix_ab_kernel, tm=tm, tiles_a=tiles_a),
        grid=(n // tm,),
        in_specs=[*_two_part_specs(tm, d, tiles_a),
                  pl.BlockSpec((tm, HALF), lambda i: (i, 0)),
                  pl.BlockSpec((tm, HALF), lambda i: (i, 0)),
                  zblk(4), zblk(5), zblk(6),
                  const((1, HALF)), const((1, HALF)),
                  const((N_HEADS, HEAD, HEAD)), const((HEAD, N_HEADS)),
                  const((d, d))],
        out_specs=pl.BlockSpec((tm, d), lambda i: (i, 0)),
        out_shape=jax.ShapeDtypeStruct((n, d), F32),
        compiler_params=_cparams("parallel"),
        name="mix_ab",
    )(xa, xb, o_f, o_b, z, z, z, onorm.reshape(1, HALF), snorm.reshape(1, HALF), wsp_bf16, bsp_t, wout_bf16)


def _ffn_kernel(x_ref, g_ref, w1_ref, w3_ref, w2_ref, o_ref, *, tf):
    x = x_ref[...]
    h = _rms(x, g_ref[...]).astype(BF16)
    acc = x
    for c in range(w1_ref.shape[1] // tf):
        cols = slice(c * tf, (c + 1) * tf)
        act = (_silu(_dot(h, w1_ref[:, cols])) * _dot(h, w3_ref[:, cols])).astype(BF16)
        acc = acc + _dot(act, w2_ref[cols, :])
    o_ref[...] = acc


def ffn(x, g, w1, w3, w2, tm=TOK_TILE, tf=FFN_FTILE):
    n, d = x.shape
    f = w1.shape[1]
    const = lambda shape: pl.BlockSpec(shape, lambda i: (0,) * len(shape))
    return pl.pallas_call(
        functools.partial(_ffn_kernel, tf=tf),
        grid=(n // tm,),
        in_specs=[pl.BlockSpec((tm, d), lambda i: (i, 0)), const((1, d)),
                  const((d, f)), const((d, f)), const((f, d))],
        out_specs=pl.BlockSpec((tm, d), lambda i: (i, 0)),
        out_shape=jax.ShapeDtypeStruct((n, d), F32),
        compiler_params=_cparams("parallel"),
        name="ffn_dense",
    )(x, g.reshape(1, d), w1, w3, w2)


def _conv_kernel(first_ref, last_ref, a_ref, gt_ref, ap_ref, gp_ref, an_ref, gn_ref, w_ref, b_ref, lng_ref,
                 lnb_ref, o_ref, ypad_ref, *, ts, rb):
    i = pl.program_id(0)
    halo = CONV_HALO
    keep_prev = jnp.where(first_ref[i] == 1, 0.0, 1.0)
    keep_next = jnp.where(last_ref[i] == 1, 0.0, 1.0)
    ypad_ref[0:halo, :] = ap_ref[...] * jax.nn.sigmoid(gp_ref[...]) * keep_prev
    ypad_ref[halo:halo + ts, :] = a_ref[...] * jax.nn.sigmoid(gt_ref[...])
    ypad_ref[halo + ts:2 * halo + ts, :] = an_ref[...] * jax.nn.sigmoid(gn_ref[...]) * keep_next

    n_win = rb + 2 * halo

    def body(bi, carry):
        r0 = pl.multiple_of(bi * rb, rb)
        strips = []
        for s in range(HALF // HEAD):
            cols = slice(s * HEAD, (s + 1) * HEAD)
            win = ypad_ref[pl.ds(r0, n_win), cols]
            acc = jnp.zeros((rb, HEAD), F32) + b_ref[:, cols]
            for res in range(8):
                sh = win if res == 0 else pltpu.roll(win, n_win - res, axis=0)
                for j in range(CONV_K):
                    off = halo - CONV_PAD + j
                    if off % 8 == res:
                        base = off - res
                        acc = acc + w_ref[j:j + 1, cols] * sh[base:base + rb, :]
            strips.append(acc)
        acc = jnp.concatenate(strips, axis=1)
        mu = jnp.mean(acc, axis=-1, keepdims=True)
        xc = acc - mu
        var = jnp.mean(xc * xc, axis=-1, keepdims=True)
        y = xc * lax.rsqrt(var + EPS) * lng_ref[...] + lnb_ref[...]
        o_ref[pl.ds(r0, rb), :] = _silu(y).astype(o_ref.dtype)
        return carry

    lax.fori_loop(0, ts // rb, body, 0)


def conv_module(zc, firsts, lasts, w, b, ln_g, ln_b, ts=TOK_TILE, rb=64):
    n = zc.shape[0]
    nt = n // ts
    hb = ts // CONV_HALO
    n_hb = n // CONV_HALO
    main = lambda cb: pl.BlockSpec((ts, HALF), lambda i, f, l: (i, cb))
    prev = lambda cb: pl.BlockSpec((CONV_HALO, HALF), lambda i, f, l: (jnp.maximum(i * hb - 1, 0), cb))
    nxt = lambda cb: pl.BlockSpec((CONV_HALO, HALF), lambda i, f, l: (jnp.minimum((i + 1) * hb, n_hb - 1), cb))
    const = lambda shape: pl.BlockSpec(shape, lambda i, f, l: (0,) * len(shape))
    grid_spec = pltpu.PrefetchScalarGridSpec(
        num_scalar_prefetch=2,
        grid=(nt,),
        in_specs=[main(0), main(1), prev(0), prev(1), nxt(0), nxt(1),
                  const((CONV_K, HALF)), const((1, HALF)), const((1, HALF)), const((1, HALF))],
        out_specs=pl.BlockSpec((ts, HALF), lambda i, f, l: (i, 0)),
        scratch_shapes=[pltpu.VMEM((ts + 2 * CONV_HALO, HALF), F32)],
    )
    return pl.pallas_call(
        functools.partial(_conv_kernel, ts=ts, rb=rb),
        grid_spec=grid_spec,
        out_shape=jax.ShapeDtypeStruct((n, HALF), BF16),
        compiler_params=_cparams("parallel"),
        name="conv_module",
    )(firsts, lasts, zc, zc, zc, zc, zc, zc, w, b.reshape(1, HALF), ln_g.reshape(1, HALF), ln_b.reshape(1, HALF))


def _seq_dft_kernel(c_ref, s_ref, wv_ref, o_ref, *, scale):
    acc = _dot(c_ref[...], wv_ref[:, 0:HALF]) - _dot(s_ref[...], wv_ref[:, HALF:2 * HALF])
    o_ref[...] = (acc * scale).astype(o_ref.dtype)


def seq_dft(wv, cos_t, sin_t, row0, n_seq, seq_len):
    tmo = min(seq_len, max(256, SEQ_DFT_TILE_BYTES // (2 * seq_len)))
    nt = seq_len // tmo
    b0 = row0 // seq_len
    scale = 1.0 / float(np.sqrt(seq_len * HEAD))
    return pl.pallas_call(
        functools.partial(_seq_dft_kernel, scale=scale),
        grid=(n_seq, nt),
        in_specs=[pl.BlockSpec((tmo, seq_len), lambda b, i: (i, 0)),
                  pl.BlockSpec((tmo, seq_len), lambda b, i: (i, 0)),
                  pl.BlockSpec((seq_len, 2 * HALF), lambda b, i: (b0 + b, 0))],
        out_specs=pl.BlockSpec((tmo, HALF), lambda b, i: (b * nt + i, 0)),
        out_shape=jax.ShapeDtypeStruct((n_seq * seq_len, HALF), BF16),
        compiler_params=_cparams("parallel", "parallel"),
        name="seq_dft_%d" % seq_len,
    )(cos_t, sin_t, wv)


def _mix_cd_router_kernel(x_ref, c_ref, da_ref, db_ref, wout_ref, g_ref, rwh_ref, rwl_ref, strict_ref, x3_ref, h_ref,
                          route_ref, cnt_ref, carry_ref, *, tm, tiles_a):
    @pl.when(pl.program_id(0) == 0)
    def _():
        carry_ref[...] = jnp.zeros_like(carry_ref)

    d_mix = _two_part_read(da_ref, db_ref, tiles_a)
    n_parts = 2
    rp = tm // n_parts
    parts = []
    for p in range(n_parts):
        rows = slice(p * rp, (p + 1) * rp)
        x3 = (x_ref[rows, :] + _dot(c_ref[rows, :], wout_ref[0:HALF, :])
              + _dot(d_mix[rows, :], wout_ref[HALF:2 * HALF, :]))
        x3_ref[rows, :] = x3
        parts.append(_rms(x3, g_ref[...]))
    logit_parts = []
    for p in range(n_parts):
        rows = slice(p * rp, (p + 1) * rp)
        h = parts[p]
        h_hi = h.astype(BF16)
        h_lo = (h - h_hi.astype(F32)).astype(BF16)
        h_ref[rows, :] = _pack_bf16_pairs(h_hi.astype(F32))
        logit_parts.append(_dot(h_hi, rwh_ref[...]) + (_dot(h_lo, rwh_ref[...]) + _dot(h_hi, rwl_ref[...])))
    logits = jnp.concatenate(logit_parts, axis=0)
    lane = lax.broadcasted_iota(jnp.int32, (tm, ROUTE_LANES), 1)
    neg = jnp.float32(-jnp.inf)
    logits = jnp.where(lane < N_EXPERTS, logits, neg)
    m1 = jnp.max(logits, axis=-1, keepdims=True)
    i1 = jnp.min(jnp.where(logits == m1, lane, ROUTE_LANES), axis=-1, keepdims=True)
    rest = jnp.where(lane == i1, neg, logits)
    m2 = jnp.max(rest, axis=-1, keepdims=True)
    i2 = jnp.min(jnp.where(rest == m2, lane, ROUTE_LANES), axis=-1, keepdims=True)
    e2 = jnp.exp(m2 - m1)
    g1 = 1.0 / (1.0 + e2)
    g2 = e2 / (1.0 + e2)

    onehot = jnp.where((lane == i1) | (lane == i2), 1.0, 0.0)
    before = _dot(strict_ref[...], onehot.astype(BF16)) + carry_ref[0:1, :]
    r1 = jnp.sum(jnp.where(lane == i1, before, 0.0), axis=-1, keepdims=True)
    r2 = jnp.sum(jnp.where(lane == i2, before, 0.0), axis=-1, keepdims=True)
    total = carry_ref[0:1, :] + jnp.sum(onehot, axis=0, keepdims=True)
    carry_ref[...] = jnp.broadcast_to(total, carry_ref.shape)
    cnt_ref[...] = jnp.broadcast_to(total, cnt_ref.shape)

    route = jnp.where(lane == 0, i1.astype(F32), 0.0)
    route = jnp.where(lane == 1, i2.astype(F32), route)
    route = jnp.where(lane == 2, g1, route)
    route = jnp.where(lane == 3, g2, route)
    route = jnp.where(lane == 4, r1, route)
    route = jnp.where(lane == 5, r2, route)
    route_ref[...] = route


def mix_cd_router(x, c_out, d_a, d_b, wout_bf16, g, rw_hi, rw_lo, tm=TOK_TILE):
    n, d = x.shape
    tiles_a = d_a.shape[0] // tm
    const = lambda shape: pl.BlockSpec(shape, lambda i: (0,) * len(shape))
    tok = lambda w: pl.BlockSpec((tm, w), lambda i: (i, 0))
    strict = jnp.tril(jnp.ones((tm, tm), BF16), -1)
    return pl.pallas_call(
        functools.partial(_mix_cd_router_kernel, tm=tm, tiles_a=tiles_a),
        grid=(n // tm,),
        in_specs=[tok(d), tok(HALF), *_two_part_specs(tm, HALF, tiles_a), const((d, d)), const((1, d)),
                  const((d, ROUTE_LANES)), const((d, ROUTE_LANES)), const((tm, tm))],
        out_specs=[tok(d), tok(d // 2), tok(ROUTE_LANES), const((8, ROUTE_LANES))],
        out_shape=[jax.ShapeDtypeStruct((n, d), F32), jax.ShapeDtypeStruct((n, d // 2), jnp.uint32),
                   jax.ShapeDtypeStruct((n, ROUTE_LANES), F32), jax.ShapeDtypeStruct((8, ROUTE_LANES), F32)],
        scratch_shapes=[pltpu.VMEM((8, ROUTE_LANES), F32)],
        compiler_params=_cparams("arbitrary"),
        name="mix_cd_router",
    )(x, c_out, d_a, d_b, wout_bf16, g.reshape(1, d), rw_hi, rw_lo, strict)


_SC_WORKERS = SC_CORES * SC_SUBCORES


def _sc_mesh():
    return plsc.VectorSubcoreMesh(core_axis_name="c", subcore_axis_name="s")


def _sc_worker_id():
    return lax.axis_index("s") * SC_CORES + lax.axis_index("c")


def sc_dispatch(rows, dest0, dest1, pad_slots, n_slots, chunk=SC_ROW_CHUNK):
    n, w = rows.shape
    n_pad = pad_slots.shape[0]
    per_worker = n // _SC_WORKERS
    pad_per_worker = n_pad // _SC_WORKERS
    assert per_worker * _SC_WORKERS == n and per_worker % chunk == 0 and chunk % 8 == 0
    assert pad_per_worker * _SC_WORKERS == n_pad and pad_per_worker % chunk == 0

    @functools.partial(
        pl.kernel, mesh=_sc_mesh(),
        out_type=jax.ShapeDtypeStruct((n_slots, w), rows.dtype),
        scratch_types=[pltpu.VMEM((chunk,), jnp.int32), pltpu.VMEM((chunk, w), rows.dtype),
                       pltpu.SemaphoreType.DMA],
    )
    def scatter(rows_hbm, d0_hbm, d1_hbm, pad_hbm, zeros_hbm, out_hbm, idx_v, rows_v, sem):
        worker = _sc_worker_id()

        @pl.loop(0, per_worker // chunk)
        def _(c):
            off = worker * per_worker + c * chunk
            pltpu.sync_copy(rows_hbm.at[pl.ds(off, chunk)], rows_v)
            for d_hbm in (d0_hbm, d1_hbm):
                pltpu.sync_copy(d_hbm.at[pl.ds(off, chunk)], idx_v)
                pltpu.async_copy(rows_v, out_hbm.at[idx_v], sem).wait()

        pltpu.sync_copy(zeros_hbm, rows_v)

        @pl.loop(0, pad_per_worker // chunk)
        def _(c):
            off = worker * pad_per_worker + c * chunk
            pltpu.sync_copy(pad_hbm.at[pl.ds(off, chunk)], idx_v)
            pltpu.async_copy(rows_v, out_hbm.at[idx_v], sem).wait()

    return scatter(rows, dest0, dest1, pad_slots, jnp.zeros((chunk, w), rows.dtype))


def sc_row_gather(table, idx, chunk=SC_ROW_CHUNK):
    b, w = idx.shape[0], table.shape[1]
    per_worker = b // _SC_WORKERS
    assert per_worker * _SC_WORKERS == b and per_worker % chunk == 0 and chunk % 8 == 0

    @functools.partial(
        pl.kernel, mesh=_sc_mesh(),
        out_type=jax.ShapeDtypeStruct((b, w), table.dtype),
        scratch_types=[pltpu.VMEM((chunk,), jnp.int32), pltpu.VMEM((chunk, w), table.dtype),
                       pltpu.SemaphoreType.DMA],
    )
    def gather(table_hbm, idx_hbm, out_hbm, idx_v, rows_v, sem):
        worker = _sc_worker_id()

        @pl.loop(0, per_worker // chunk)
        def _(c):
            off = worker * per_worker + c * chunk
            pltpu.sync_copy(idx_hbm.at[pl.ds(off, chunk)], idx_v)
            pltpu.async_copy(table_hbm.at[idx_v], rows_v, sem).wait()
            pltpu.sync_copy(rows_v, out_hbm.at[pl.ds(off, chunk)])

    return gather(table, idx)


def _moe_ffn_kernel(be_ref, nu_ref, x_ref, w1_ref, w3_ref, w2_ref, o_ref, xbf_ref, acc_ref, *, rows_sub, cols_sub, nf):
    del be_ref
    b = pl.program_id(0)
    f = pl.program_id(1)
    active = b < nu_ref[0]
    tmb = x_ref.shape[0]
    tf = w1_ref.shape[2]
    half = x_ref.shape[1]

    @pl.when(active & (f == 0))
    def _():
        hi, lo = _unpack_bf16_pairs(x_ref[...])
        xbf_ref[:, 0:half] = hi.astype(BF16)
        xbf_ref[:, half:2 * half] = lo.astype(BF16)

    def partial_sums(first, last):
        for r in range(tmb // rows_sub):
            rows = slice(r * rows_sub, (r + 1) * rows_sub)
            xr = xbf_ref[rows, :]
            acc = jnp.zeros((rows_sub, 2 * half), F32) if first else acc_ref[rows, :]
            for c in range(tf // cols_sub):
                cols = slice(c * cols_sub, (c + 1) * cols_sub)
                act = (_silu(_dot(xr, w1_ref[0, :, cols])) * _dot(xr, w3_ref[0, :, cols])).astype(BF16)
                acc = acc + _dot(act, w2_ref[0, cols, :])
            if last:
                o_ref[rows, :] = _pack_bf16_pairs(acc.astype(BF16).astype(F32))
            else:
                acc_ref[rows, :] = acc

    @pl.when(active & (f == 0))
    def _():
        partial_sums(True, False)

    if nf > 2:
        @pl.when(active & (f > 0) & (f < nf - 1))
        def _():
            partial_sums(False, False)

    @pl.when(active & (f == nf - 1))
    def _():
        partial_sums(False, True)

    @pl.when(jnp.logical_not(active) & (f == 0))
    def _():
        o_ref[...] = jnp.zeros_like(o_ref)


def moe_ffn(xb, block_e, n_used, w1, w3, w2, tmb=MOE_BLOCK, tf=MOE_FTILE):
    n_slots, half = xb.shape
    d = 2 * half
    nb = n_slots // tmb
    nf = w1.shape[2] // tf
    assert nf >= 2, "the first hidden-column step initialises the sum and the last one writes the block"

    def bclamp(b, nu):
        return jnp.minimum(b, nu[0] - 1)

    def fclamp(b, f, nu):
        return jnp.where(b < nu[0], f, nf - 1)

    grid_spec = pltpu.PrefetchScalarGridSpec(
        num_scalar_prefetch=2,
        grid=(nb, nf),
        in_specs=[pl.BlockSpec((tmb, half), lambda b, f, be, nu: (bclamp(b, nu), 0)),
                  pl.BlockSpec((1, d, tf), lambda b, f, be, nu: (be[bclamp(b, nu)], 0, fclamp(b, f, nu))),
                  pl.BlockSpec((1, d, tf), lambda b, f, be, nu: (be[bclamp(b, nu)], 0, fclamp(b, f, nu))),
                  pl.BlockSpec((1, tf, d), lambda b, f, be, nu: (be[bclamp(b, nu)], fclamp(b, f, nu), 0))],
        out_specs=pl.BlockSpec((tmb, half), lambda b, f, be, nu: (b, 0)),
        scratch_shapes=[pltpu.VMEM((tmb, d), BF16), pltpu.VMEM((tmb, d), F32)],
    )
    return pl.pallas_call(
        functools.partial(_moe_ffn_kernel, rows_sub=MOE_ROWS_SUB, cols_sub=MOE_COLS_SUB, nf=nf),
        grid_spec=grid_spec,
        out_shape=jax.ShapeDtypeStruct((n_slots, half), jnp.uint32),
        compiler_params=_cparams("arbitrary", "arbitrary"),
        name="moe_ffn",
    )(block_e, n_used, xb, w1, w3, w2)


def _finish_kernel(x_ref, route_ref, g_ref, y0_ref, y1_ref, o_ref):
    half = y0_ref.shape[1]
    route = route_ref[...]
    g0, g1 = route[:, 2:3], route[:, 3:4]
    hi0, lo0 = _unpack_bf16_pairs(y0_ref[...])
    hi1, lo1 = _unpack_bf16_pairs(y1_ref[...])
    ya = x_ref[:, 0:half] + g0 * hi0 + g1 * hi1
    yb = x_ref[:, half:2 * half] + g0 * lo0 + g1 * lo1
    ms = (jnp.sum(ya * ya, axis=-1, keepdims=True) + jnp.sum(yb * yb, axis=-1, keepdims=True)) / (2 * half)
    inv = lax.rsqrt(ms + EPS)
    o_ref[:, 0:half] = ya * inv * g_ref[:, 0:half]
    o_ref[:, half:2 * half] = yb * inv * g_ref[:, half:2 * half]


def moe_finish(x, route, y_rows, g, row0, tm=TOK_TILE):
    d = x.shape[1]
    m = y_rows.shape[0] // 2
    n_tiles = m // tm
    t0 = row0 // tm
    return pl.pallas_call(
        _finish_kernel,
        grid=(n_tiles,),
        in_specs=[pl.BlockSpec((tm, d), lambda i: (t0 + i, 0)),
                  pl.BlockSpec((tm, ROUTE_LANES), lambda i: (t0 + i, 0)),
                  pl.BlockSpec((1, d), lambda i: (0, 0)),
                  pl.BlockSpec((tm, d // 2), lambda i: (i, 0)),
                  pl.BlockSpec((tm, d // 2), lambda i: (i + n_tiles, 0))],
        out_specs=pl.BlockSpec((tm, d), lambda i: (i, 0)),
        out_shape=jax.ShapeDtypeStruct((m, d), F32),
        compiler_params=_cparams("parallel"),
        name="moe_finish",
    )(x, route, g.reshape(1, d), y_rows, y_rows)


def _tile_flags(seq_lens, tile):
    firsts, lasts = [], []
    for length in seq_lens:
        k = length // tile
        firsts += [1] + [0] * (k - 1)
        lasts += [0] * (k - 1) + [1]
    return np.asarray(firsts, np.int32), np.asarray(lasts, np.int32)


def _dft_table_kernel(cphi_ref, sphi_ref, cth_ref, sth_ref, c_ref, s_ref):
    cphi, sphi = cphi_ref[...], sphi_ref[...]
    cth, sth = cth_ref[0], sth_ref[0]
    c_ref[...] = (cphi * cth - sphi * sth).astype(BF16)
    s_ref[...] = (sphi * cth + cphi * sth).astype(BF16)


def _angles(rows, t):
    k = lax.broadcasted_iota(jnp.int32, (rows.shape[0], t), 1)
    ang = ((rows[:, None] * k) % t).astype(F32) * (2.0 * np.pi / t)
    return jnp.cos(ang), jnp.sin(ang)


def _dft_tables(t):
    rows = min(DFT_GEN_ROWS, t)
    nt = t // rows
    cphi, sphi = _angles(jnp.arange(rows, dtype=jnp.int32), t)
    cth, sth = _angles(jnp.arange(nt, dtype=jnp.int32) * rows, t)
    tile = pl.BlockSpec((rows, t), lambda i: (0, 0))
    vec = pl.BlockSpec((1, 1, t), lambda i: (i, 0, 0))
    out = pl.BlockSpec((rows, t), lambda i: (i, 0))
    return pl.pallas_call(
        _dft_table_kernel,
        grid=(nt,),
        in_specs=[tile, tile, vec, vec],
        out_specs=[out, out],
        out_shape=[jax.ShapeDtypeStruct((t, t), BF16)] * 2,
        compiler_params=_cparams("parallel"),
        name="dft_table_%d" % t,
    )(cphi, sphi, cth.reshape(nt, 1, t), sth.reshape(nt, 1, t))


def kernel(x_prompt, x_sample, norm_mix, norm_ffn, norm_final, w_in_ab, hgrn_lb_logits, hgrn_out_norm, sgu_norm,
           sgu_w, sgu_b, w_out_ab, w_in_cd, conv_w, conv_b, conv_ln_g, conv_ln_b, w_out_cd, ffn_w1, ffn_w3, ffn_w2,
           router_w, moe_w1, moe_w3, moe_w2):
    bp, tp, d = x_prompt.shape
    bs, ts_, _ = x_sample.shape
    n_p, n_s = bp * tp, bs * ts_
    n = n_p + n_s
    seq_lens = [tp] * bp + [ts_] * bs
    tile = TOK_TILE
    firsts, lasts = _tile_flags(seq_lens, tile)
    depth = norm_mix.shape[0]
    assert depth == 2, "the layer schedule below is the two-layer trunk"
    xa, xb_in = x_prompt.reshape(n_p, d), x_sample.reshape(n_s, d)
    moe_bf16 = (moe_w1.astype(BF16), moe_w3.astype(BF16), moe_w2.astype(BF16))

    for layer in range(depth):
        j = layer // 2
        if layer % 2 == 0:
            z = norm_proj(xa, xb_in, norm_mix[layer], w_in_ab[j].astype(BF16), ready_first=moe_bf16)
            o_f = hgrn_scan(z, hgrn_lb_logits, jnp.asarray(firsts), reverse=False, layer=layer, f_col=1)
            o_b = hgrn_scan(z, hgrn_lb_logits, jnp.asarray(lasts[::-1].copy()), reverse=True, layer=layer, f_col=2)
            x = mix_ab(xa, xb_in, o_f, o_b, z, hgrn_out_norm[j], sgu_norm[j], sgu_w[j].astype(BF16), sgu_b[j].T,
                       w_out_ab[j].astype(BF16))
            x = ffn(x, norm_ffn[layer], ffn_w1[j].astype(BF16), ffn_w3[j].astype(BF16), ffn_w2[j].astype(BF16))
        else:
            ang = (lax.broadcasted_iota(jnp.int32, (HEAD, HEAD), 0) * lax.broadcasted_iota(jnp.int32, (HEAD, HEAD), 1)
                   % HEAD).astype(F32) * (2.0 * np.pi / HEAD)
            dft_c = jnp.concatenate([jnp.cos(ang), jnp.sin(ang)], axis=1).astype(BF16)
            zc, wv = norm_proj_cd(x, norm_mix[layer], w_in_cd[j].astype(BF16), dft_c)
            c_out = conv_module(zc, jnp.asarray(firsts), jnp.asarray(lasts), conv_w[j], conv_b[j], conv_ln_g[j],
                                conv_ln_b[j])
            cos_p, sin_p = _dft_tables(tp)
            cos_s, sin_s = _dft_tables(ts_)
            d_p = seq_dft(wv, cos_p, sin_p, 0, bp, tp)
            d_s = seq_dft(wv, cos_s, sin_s, n_p, bs, ts_)

            rw = jnp.zeros((d, ROUTE_LANES), F32).at[:, :N_EXPERTS].set(router_w[j])
            rw_hi = rw.astype(BF16)
            rw_lo = (rw - rw_hi.astype(F32)).astype(BF16)
            x3, h, route, counts = mix_cd_router(x, c_out, d_p, d_s, w_out_cd[j].astype(BF16), norm_ffn[layer],
                                                 rw_hi, rw_lo)

            tmb = MOE_BLOCK
            n_blocks = (2 * n + tmb - 1) // tmb + N_EXPERTS
            cnt = counts[0, :N_EXPERTS].astype(jnp.int32)
            padded = (cnt + tmb - 1) // tmb * tmb
            pend = jnp.cumsum(padded)
            pstart = pend - padded
            e_idx = route[:, 0:2].astype(jnp.int32)
            dest = pstart[e_idx] + route[:, 4:6].astype(jnp.int32)
            dest0, dest1 = dest[:, 0], dest[:, 1]
            block_start = jnp.arange(n_blocks, dtype=jnp.int32) * tmb
            block_e = jnp.minimum(jnp.sum((pend[None, :] <= block_start[:, None]).astype(jnp.int32), axis=1),
                                  N_EXPERTS - 1)
            n_used = (pend[-1] // tmb).astype(jnp.int32).reshape(1)
            n_slots = n_blocks * tmb
            jpad = jnp.arange(tmb, dtype=jnp.int32)[None, :]
            pad_slots = jnp.where(jpad < (padded - cnt)[:, None], (pstart + cnt)[:, None] + jpad,
                                  n_slots - 1).reshape(N_EXPERTS * tmb)

            xb = sc_dispatch(h, dest0, dest1, pad_slots, n_slots)
            yb = moe_ffn(xb, block_e, n_used, moe_bf16[0][j], moe_bf16[1][j], moe_bf16[2][j])
            outs = []
            for row0, m in ((0, n_p), (n_p, n_s)):
                idx = jnp.concatenate([dest0[row0:row0 + m], dest1[row0:row0 + m]])
                outs.append(moe_finish(x3, route, sc_row_gather(yb, idx), norm_final, row0))
            out_p, out_s = outs

    return out_p.reshape(bp, tp, d), out_s.reshape(bs, ts_, d)
```

```python
import functools

import numpy as np
import jax
import jax.numpy as jnp
from jax import lax
from jax.experimental import pallas as pl
from jax.experimental.pallas import tpu as pltpu
from jax.experimental.pallas import tpu_sc as plsc

F32 = jnp.float32
BF16 = jnp.bfloat16
EPS = 1e-6

D_MODEL = 1024
HALF = 512
HEAD = 128
N_HEADS = 4
HGRN_CHUNK = 64
HGRN_SUB = 16
HGRN_CHUNKS_PER_ITER = 4
CONV_K = 31
CONV_PAD = 15
CONV_HALO = 16
N_EXPERTS = 8
ROUTE_LANES = 128
MAX_EXP = 80.0

VMEM_LIMIT_BYTES = 56 * 1024 * 1024

TOK_TILE = 512
MOE_BLOCK = 1024
MOE_FTILE = 1792
MOE_ROWS_SUB = 512
MOE_COLS_SUB = 256
FFN_FTILE = 256
SC_CORES = 2
SC_SUBCORES = 16
SC_ROW_CHUNK = 128
SEQ_DFT_TILE_BYTES = 4 * 1024 * 1024
DFT_GEN_ROWS = 128


def _dot(a, b):
    return jnp.dot(a, b, preferred_element_type=F32)


def _dot_nt(a, b):
    return lax.dot_general(a, b, (((1,), (1,)), ((), ())), preferred_element_type=F32)


def _cparams(*sem):
    return pltpu.CompilerParams(dimension_semantics=sem, vmem_limit_bytes=VMEM_LIMIT_BYTES)


def _rms(x, g):
    ms = jnp.mean(x * x, axis=-1, keepdims=True)
    return x * lax.rsqrt(ms + EPS) * g


def _silu(x):
    return x * jax.nn.sigmoid(x)


def _pack_bf16_pairs(x):
    w = x.shape[1] // 2
    bits = lax.bitcast_convert_type(x, jnp.uint32)
    return (bits[:, :w] & jnp.uint32(0xFFFF0000)) | (bits[:, w:] >> 16)


def _unpack_bf16_pairs(words):
    hi = lax.bitcast_convert_type(words & jnp.uint32(0xFFFF0000), F32)
    lo = lax.bitcast_convert_type(words << 16, F32)
    return hi, lo


def _gelu_tanh(x):
    return 0.5 * x * (1.0 + jnp.tanh(0.7978845608028654 * (x + 0.044715 * (x * x * x))))


def _two_part_specs(tm, d, tiles_a):
    return (pl.BlockSpec((tm, d), lambda i: (jnp.minimum(i, tiles_a - 1), 0)),
            pl.BlockSpec((tm, d), lambda i: (jnp.maximum(i - tiles_a, 0), 0)))


def _two_part_read(xa_ref, xb_ref, tiles_a):
    return jnp.where(pl.program_id(0) < tiles_a, xa_ref[...], xb_ref[...])


def _norm_proj_kernel(xa_ref, xb_ref, g_ref, w_ref, *rest, tn, tiles_a):
    o_ref = rest[-1]
    h = _rms(_two_part_read(xa_ref, xb_ref, tiles_a), g_ref[...]).astype(BF16)
    for c in range(o_ref.shape[1] // tn):
        o_ref[:, c * tn:(c + 1) * tn] = _dot(h, w_ref[:, c * tn:(c + 1) * tn]).astype(o_ref.dtype)


def norm_proj(xa, xb, g, w_bf16, ready_first=(), tm=TOK_TILE, tn=512):
    d = xa.shape[1]
    n = xa.shape[0] + xb.shape[0]
    tiles_a = xa.shape[0] // tm
    nout = w_bf16.shape[1]
    return pl.pallas_call(
        functools.partial(_norm_proj_kernel, tn=tn, tiles_a=tiles_a),
        grid=(n // tm,),
        in_specs=[*_two_part_specs(tm, d, tiles_a),
                  pl.BlockSpec((1, d), lambda i: (0, 0)),
                  pl.BlockSpec((d, nout), lambda i: (0, 0)),
                  *[pl.BlockSpec(memory_space=pl.ANY) for _ in ready_first]],
        out_specs=pl.BlockSpec((tm, nout), lambda i: (i, 0)),
        out_shape=jax.ShapeDtypeStruct((n, nout), BF16),
        compiler_params=_cparams("parallel"),
        name="norm_proj_ab",
    )(xa, xb, g.reshape(1, d), w_bf16, *ready_first)


def _norm_proj_cd_kernel(x_ref, g_ref, w_ref, dft_ref, zc_ref, wv_ref):
    h = _rms(x_ref[...], g_ref[...]).astype(BF16)
    for c in range(2):
        zc_ref[:, c * HALF:(c + 1) * HALF] = _dot(h, w_ref[:, c * HALF:(c + 1) * HALF])
    d = _dot(h, w_ref[:, 2 * HALF:3 * HALF]).astype(BF16)
    for g in range(N_HEADS):
        r = _dot(d[:, g * HEAD:(g + 1) * HEAD], dft_ref[...])
        wv_ref[:, g * HEAD:(g + 1) * HEAD] = r[:, :HEAD].astype(BF16)
        wv_ref[:, HALF + g * HEAD:HALF + (g + 1) * HEAD] = r[:, HEAD:].astype(BF16)


def norm_proj_cd(x, g, w_bf16, dft_c, tm=TOK_TILE):
    n, d = x.shape
    return pl.pallas_call(
        _norm_proj_cd_kernel,
        grid=(n // tm,),
        in_specs=[pl.BlockSpec((tm, d), lambda i: (i, 0)),
                  pl.BlockSpec((1, d), lambda i: (0, 0)),
                  pl.BlockSpec((d, 3 * HALF), lambda i: (0, 0)),
                  pl.BlockSpec((HEAD, 2 * HEAD), lambda i: (0, 0))],
        out_specs=[pl.BlockSpec((tm, 2 * HALF), lambda i: (i, 0)),
                   pl.BlockSpec((tm, 2 * HALF), lambda i: (i, 0))],
        out_shape=[jax.ShapeDtypeStruct((n, 2 * HALF), F32),
                   jax.ShapeDtypeStruct((n, 2 * HALF), BF16)],
        compiler_params=_cparams("parallel"),
        name="norm_proj_cd",
    )(x, g.reshape(1, d), w_bf16, dft_c)


def _hgrn_kernel(reset_ref, q_ref, f_ref, v_ref, lbl_ref, o_ref, st0, st1, st2, st3, oi_ref, qe_ref, u_ref, d_ref,
                 *, reverse, layer, tt):
    c, sub = HGRN_CHUNK, HGRN_SUB
    n_sub = c // sub
    n_chunks = tt // c
    states = (st0, st1, st2, st3)

    @pl.when(reset_ref[pl.program_id(0)] == 1)
    def _():
        for st in states:
            st[...] = jnp.zeros_like(st)

    lg = lbl_ref[...]
    e = jnp.exp(lg - jnp.max(lg, axis=0, keepdims=True))
    p = e / jnp.sum(e, axis=0, keepdims=True)
    lb_all = p[0:1, :]
    for r in range(1, layer + 1):
        lb_all = lb_all + p[r:r + 1, :]

    row = lax.broadcasted_iota(jnp.int32, (c, c), 0)
    col = lax.broadcasted_iota(jnp.int32, (c, c), 1)
    tri = jnp.where((col >= row) if reverse else (col <= row), 1.0, 0.0).astype(BF16)
    spans, keeps = [], []
    for i in range(n_sub):
        span = slice(i * sub, c) if reverse else slice(0, (i + 1) * sub)
        n_span = span.stop - span.start
        tr = lax.broadcasted_iota(jnp.int32, (sub, n_span), 0)
        sc = lax.broadcasted_iota(jnp.int32, (sub, n_span), 1)
        spans.append(span)
        keeps.append((sc >= tr) if reverse else (sc <= tr + i * sub))

    units = [(cj, h) for cj in range(HGRN_CHUNKS_PER_ITER) for h in range(N_HEADS)]

    def local_body(it, carry):
        ci = [it * HGRN_CHUNKS_PER_ITER + cj for cj in range(HGRN_CHUNKS_PER_ITER)]
        r0 = [pl.multiple_of(x * c, c) for x in ci]
        vals = {}
        for cj, h in units:
            cols = slice(h * HEAD, (h + 1) * HEAD)
            lb = lb_all[:, cols]
            q = _silu(q_ref[pl.ds(r0[cj], c), cols].astype(F32))
            v = v_ref[pl.ds(r0[cj], c), cols].astype(F32)
            f = lb + (1.0 - lb) * jax.nn.sigmoid(f_ref[pl.ds(r0[cj], c), cols].astype(F32))
            lf = jnp.log(f)
            lf_hi = lf.astype(BF16)
            lf_lo = (lf - lf_hi.astype(F32)).astype(BF16)
            vals[cj, h] = (q, v, 1.0 - f, _dot(tri, lf_hi) + _dot(tri, lf_lo))
        scores = {}
        for cj, h in units:
            cols = slice(h * HEAD, (h + 1) * HEAD)
            q, v, k, b = vals[cj, h]
            qe_ref[pl.ds(r0[cj], c), cols] = (q * jnp.exp(b)).astype(BF16)
            edge = b[0:1, :] if reverse else b[c - 1:c, :]
            ks = (k * jnp.exp(edge - b)).astype(BF16)
            u_ref[ci[cj] * N_HEADS + h] = _dot(v.T.astype(BF16), ks)
            d_ref[ci[cj] * N_HEADS + h] = jnp.broadcast_to(jnp.exp(edge), (8, HEAD))
            for i in range(n_sub):
                rows = slice(i * sub, (i + 1) * sub)
                if reverse:
                    anchor = b[(i + 1) * sub:(i + 1) * sub + 1, :] if i + 1 < n_sub else jnp.zeros((1, HEAD), F32)
                else:
                    anchor = b[i * sub - 1:i * sub, :] if i > 0 else jnp.zeros((1, HEAD), F32)
                qi = (q[rows] * jnp.exp(b[rows] - anchor)).astype(BF16)
                ki = (k[spans[i]] * jnp.exp(jnp.minimum(anchor - b[spans[i]], MAX_EXP))).astype(BF16)
                scores[cj, h, i] = _dot_nt(qi, ki)
        for cj, h in units:
            cols = slice(h * HEAD, (h + 1) * HEAD)
            vb = vals[cj, h][1].astype(BF16)
            for i in range(n_sub):
                a = jnp.where(keeps[i], scores[cj, h, i], 0.0).astype(BF16)
                oi_ref[pl.ds(r0[cj] + i * sub, sub), cols] = _dot(a, vb[spans[i]])
        return carry

    lax.fori_loop(0, n_chunks // HGRN_CHUNKS_PER_ITER, local_body, 0)

    cur = [st[...] for st in states]
    for ci in range(n_chunks):
        cc = (n_chunks - 1 - ci) if reverse else ci
        for h in range(N_HEADS):
            cols = slice(h * HEAD, (h + 1) * HEAD)
            o_ref[cc * c:(cc + 1) * c, cols] = (
                oi_ref[cc * c:(cc + 1) * c, cols]
                + _dot_nt(qe_ref[cc * c:(cc + 1) * c, cols], cur[h].astype(BF16))).astype(o_ref.dtype)
        cur = [cur[h] * d_ref[cc * N_HEADS + h][0:1, :] + u_ref[cc * N_HEADS + h] for h in range(N_HEADS)]
    for h in range(N_HEADS):
        states[h][...] = cur[h]


def hgrn_scan(z, lb_logits, resets, *, reverse, layer, f_col, tt=TOK_TILE):
    n = z.shape[0]
    nt = n // tt
    n_units = tt // HGRN_CHUNK * N_HEADS
    order = (lambda i, r: (nt - 1 - i, 0)) if reverse else (lambda i, r: (i, 0))
    blk = lambda cb: (lambda i, r: (order(i, r)[0], cb))
    grid_spec = pltpu.PrefetchScalarGridSpec(
        num_scalar_prefetch=1,
        grid=(nt,),
        in_specs=[pl.BlockSpec((tt, HALF), blk(0)),
                  pl.BlockSpec((tt, HALF), blk(f_col)),
                  pl.BlockSpec((tt, HALF), blk(3)),
                  pl.BlockSpec(lb_logits.shape, lambda i, r: (0, 0))],
        out_specs=pl.BlockSpec((tt, HALF), order),
        scratch_shapes=[pltpu.VMEM((HEAD, HEAD), F32)] * N_HEADS + [
            pltpu.VMEM((tt, HALF), F32),
            pltpu.VMEM((tt, HALF), BF16),
            pltpu.VMEM((n_units, HEAD, HEAD), F32),
            pltpu.VMEM((n_units, 8, HEAD), F32)],
    )
    return pl.pallas_call(
        functools.partial(_hgrn_kernel, reverse=reverse, layer=layer, tt=tt),
        grid_spec=grid_spec,
        out_shape=jax.ShapeDtypeStruct((n, HALF), BF16),
        compiler_params=_cparams("arbitrary"),
        name="hgrn_bwd" if reverse else "hgrn_fwd",
    )(resets, z, z, z, lb_logits)


def _mix_ab_kernel(xa_ref, xb_ref, of_ref, ob_ref, g_ref, u_ref, v_ref, onorm_ref, snorm_ref, wsp_ref, bsp_ref,
                   wout_ref, o_ref, *, tm, tiles_a):
    o = of_ref[...].astype(F32) + ob_ref[...].astype(F32)
    g = g_ref[...].astype(F32)
    parts = []
    for h in range(N_HEADS):
        cols = slice(h * HEAD, (h + 1) * HEAD)
        parts.append(_rms(o[:, cols], onorm_ref[:, cols]) * _silu(g[:, cols]))
    a = jnp.concatenate(parts, axis=1).astype(BF16)
    acc = _two_part_read(xa_ref, xb_ref, tiles_a) + _dot(a, wout_ref[0:HALF, :])

    u = _gelu_tanh(u_ref[...].astype(F32))
    vb = _rms(_gelu_tanh(v_ref[...].astype(F32)), snorm_ref[...]).astype(BF16)
    rows = []
    for c in range(tm // HEAD):
        mixed = []
        for gi in range(N_HEADS):
            vg = vb[c * HEAD:(c + 1) * HEAD, gi * HEAD:(gi + 1) * HEAD]
            mixed.append(_dot(wsp_ref[gi], vg) + bsp_ref[:, gi:gi + 1])
        rows.append(jnp.concatenate(mixed, axis=1))
    b = (u * jnp.concatenate(rows, axis=0)).astype(BF16)
    o_ref[...] = acc + _dot(b, wout_ref[HALF:2 * HALF, :])


def mix_ab(xa, xb, o_f, o_b, z, onorm, snorm, wsp_bf16, bsp_t, wout_bf16, tm=TOK_TILE):
    d = xa.shape[1]
    n = xa.shape[0] + xb.shape[0]
    tiles_a = xa.shape[0] // tm
    zblk = lambda cb: pl.BlockSpec((tm, HALF), lambda i: (i, cb))
    const = lambda shape: pl.BlockSpec(shape, lambda i: (0,) * len(shape))
    return pl.pallas_call(
        functools.partial(_mix_ab_kernel, tm=tm, tiles_a=tiles_a),
        grid=(n // tm,),
        in_specs=[*_two_part_specs(tm, d, tiles_a),
                  pl.BlockSpec((tm, HALF), lambda i: (i, 0)),
                  pl.BlockSpec((tm, HALF), lambda i: (i, 0)),
                  zblk(4), zblk(5), zblk(6),
                  const((1, HALF)), const((1, HALF)),
                  const((N_HEADS, HEAD, HEAD)), const((HEAD, N_HEADS)),
                  const((d, d))],
        out_specs=pl.BlockSpec((tm, d), lambda i: (i, 0)),
        out_shape=jax.ShapeDtypeStruct((n, d), F32),
        compiler_params=_cparams("parallel"),
        name="mix_ab",
    )(xa, xb, o_f, o_b, z, z, z, onorm.reshape(1, HALF), snorm.reshape(1, HALF), wsp_bf16, bsp_t, wout_bf16)


def _ffn_kernel(x_ref, g_ref, w1_ref, w3_ref, w2_ref, o_ref, *, tf):
    x = x_ref[...]
    h = _rms(x, g_ref[...]).astype(BF16)
    acc = x
    for c in range(w1_ref.shape[1] // tf):
        cols = slice(c * tf, (c + 1) * tf)
        act = (_silu(_dot(h, w1_ref[:, cols])) * _dot(h, w3_ref[:, cols])).astype(BF16)
        acc = acc + _dot(act, w2_ref[cols, :])
    o_ref[...] = acc


def ffn(x, g, w1, w3, w2, tm=TOK_TILE, tf=FFN_FTILE):
    n, d = x.shape
    f = w1.shape[1]
    const = lambda shape: pl.BlockSpec(shape, lambda i: (0,) * len(shape))
    return pl.pallas_call(
        functools.partial(_ffn_kernel, tf=tf),
        grid=(n // tm,),
        in_specs=[pl.BlockSpec((tm, d), lambda i: (i, 0)), const((1, d)),
                  const((d, f)), const((d, f)), const((f, d))],
        out_specs=pl.BlockSpec((tm, d), lambda i: (i, 0)),
        out_shape=jax.ShapeDtypeStruct((n, d), F32),
        compiler_params=_cparams("parallel"),
        name="ffn_dense",
    )(x, g.reshape(1, d), w1, w3, w2)


def _conv_kernel(first_ref, last_ref, a_ref, gt_ref, ap_ref, gp_ref, an_ref, gn_ref, w_ref, b_ref, lng_ref,
                 lnb_ref, o_ref, ypad_ref, *, ts, rb):
    i = pl.program_id(0)
    halo = CONV_HALO
    keep_prev = jnp.where(first_ref[i] == 1, 0.0, 1.0)
    keep_next = jnp.where(last_ref[i] == 1, 0.0, 1.0)
    ypad_ref[0:halo, :] = ap_ref[...] * jax.nn.sigmoid(gp_ref[...]) * keep_prev
    ypad_ref[halo:halo + ts, :] = a_ref[...] * jax.nn.sigmoid(gt_ref[...])
    ypad_ref[halo + ts:2 * halo + ts, :] = an_ref[...] * jax.nn.sigmoid(gn_ref[...]) * keep_next

    n_win = rb + 2 * halo

    def body(bi, carry):
        r0 = pl.multiple_of(bi * rb, rb)
        strips = []
        for s in range(HALF // HEAD):
            cols = slice(s * HEAD, (s + 1) * HEAD)
            win = ypad_ref[pl.ds(r0, n_win), cols]
            acc = jnp.zeros((rb, HEAD), F32) + b_ref[:, cols]
            for res in range(8):
                sh = win if res == 0 else pltpu.roll(win, n_win - res, axis=0)
                for j in range(CONV_K):
                    off = halo - CONV_PAD + j
                    if off % 8 == res:
                        base = off - res
                        acc = acc + w_ref[j:j + 1, cols] * sh[base:base + rb, :]
            strips.append(acc)
        acc = jnp.concatenate(strips, axis=1)
        mu = jnp.mean(acc, axis=-1, keepdims=True)
        xc = acc - mu
        var = jnp.mean(xc * xc, axis=-1, keepdims=True)
        y = xc * lax.rsqrt(var + EPS) * lng_ref[...] + lnb_ref[...]
        o_ref[pl.ds(r0, rb), :] = _silu(y).astype(o_ref.dtype)
        return carry

    lax.fori_loop(0, ts // rb, body, 0)


def conv_module(zc, firsts, lasts, w, b, ln_g, ln_b, ts=TOK_TILE, rb=128):
    n = zc.shape[0]
    nt = n // ts
    hb = ts // CONV_HALO
    n_hb = n // CONV_HALO
    main = lambda cb: pl.BlockSpec((ts, HALF), lambda i, f, l: (i, cb))
    prev = lambda cb: pl.BlockSpec((CONV_HALO, HALF), lambda i, f, l: (jnp.maximum(i * hb - 1, 0), cb))
    nxt = lambda cb: pl.BlockSpec((CONV_HALO, HALF), lambda i, f, l: (jnp.minimum((i + 1) * hb, n_hb - 1), cb))
    const = lambda shape: pl.BlockSpec(shape, lambda i, f, l: (0,) * len(shape))
    grid_spec = pltpu.PrefetchScalarGridSpec(
        num_scalar_prefetch=2,
        grid=(nt,),
        in_specs=[main(0), main(1), prev(0), prev(1), nxt(0), nxt(1),
                  const((CONV_K, HALF)), const((1, HALF)), const((1, HALF)), const((1, HALF))],
        out_specs=pl.BlockSpec((ts, HALF), lambda i, f, l: (i, 0)),
        scratch_shapes=[pltpu.VMEM((ts + 2 * CONV_HALO, HALF), F32)],
    )
    return pl.pallas_call(
        functools.partial(_conv_kernel, ts=ts, rb=rb),
        grid_spec=grid_spec,
        out_shape=jax.ShapeDtypeStruct((n, HALF), BF16),
        compiler_params=_cparams("parallel"),
        name="conv_module",
    )(firsts, lasts, zc, zc, zc, zc, zc, zc, w, b.reshape(1, HALF), ln_g.reshape(1, HALF), ln_b.reshape(1, HALF))


def _seq_dft_kernel(c_ref, s_ref, wv_ref, o_ref, *, scale):
    acc = _dot(c_ref[...], wv_ref[:, 0:HALF]) - _dot(s_ref[...], wv_ref[:, HALF:2 * HALF])
    o_ref[...] = (acc * scale).astype(o_ref.dtype)


def seq_dft(wv, cos_t, sin_t, row0, n_seq, seq_len):
    tmo = min(seq_len, max(256, SEQ_DFT_TILE_BYTES // (2 * seq_len)))
    nt = seq_len // tmo
    b0 = row0 // seq_len
    scale = 1.0 / float(np.sqrt(seq_len * HEAD))
    return pl.pallas_call(
        functools.partial(_seq_dft_kernel, scale=scale),
        grid=(n_seq, nt),
        in_specs=[pl.BlockSpec((tmo, seq_len), lambda b, i: (i, 0)),
                  pl.BlockSpec((tmo, seq_len), lambda b, i: (i, 0)),
                  pl.BlockSpec((seq_len, 2 * HALF), lambda b, i: (b0 + b, 0))],
        out_specs=pl.BlockSpec((tmo, HALF), lambda b, i: (b * nt + i, 0)),
        out_shape=jax.ShapeDtypeStruct((n_seq * seq_len, HALF), BF16),
        compiler_params=_cparams("parallel", "parallel"),
        name="seq_dft_%d" % seq_len,
    )(cos_t, sin_t, wv)


def _mix_cd_router_kernel(x_ref, c_ref, da_ref, db_ref, wout_ref, g_ref, rwh_ref, rwl_ref, strict_ref, x3_ref, h_ref,
                          route_ref, cnt_ref, carry_ref, *, tm, tiles_a):
    @pl.when(pl.program_id(0) == 0)
    def _():
        carry_ref[...] = jnp.zeros_like(carry_ref)

    d_mix = _two_part_read(da_ref, db_ref, tiles_a)
    n_parts = 2
    rp = tm // n_parts
    parts = []
    for p in range(n_parts):
        rows = slice(p * rp, (p + 1) * rp)
        x3 = (x_ref[rows, :] + _dot(c_ref[rows, :], wout_ref[0:HALF, :])
              + _dot(d_mix[rows, :], wout_ref[HALF:2 * HALF, :]))
        x3_ref[rows, :] = x3
        parts.append(_rms(x3, g_ref[...]))
    logit_parts = []
    for p in range(n_parts):
        rows = slice(p * rp, (p + 1) * rp)
        h = parts[p]
        h_hi = h.astype(BF16)
        h_lo = (h - h_hi.astype(F32)).astype(BF16)
        h_ref[rows, :] = _pack_bf16_pairs(h_hi.astype(F32))
        logit_parts.append(_dot(h_hi, rwh_ref[...]) + (_dot(h_lo, rwh_ref[...]) + _dot(h_hi, rwl_ref[...])))
    logits = jnp.concatenate(logit_parts, axis=0)
    lane = lax.broadcasted_iota(jnp.int32, (tm, ROUTE_LANES), 1)
    neg = jnp.float32(-jnp.inf)
    logits = jnp.where(lane < N_EXPERTS, logits, neg)
    m1 = jnp.max(logits, axis=-1, keepdims=True)
    i1 = jnp.min(jnp.where(logits == m1, lane, ROUTE_LANES), axis=-1, keepdims=True)
    rest = jnp.where(lane == i1, neg, logits)
    m2 = jnp.max(rest, axis=-1, keepdims=True)
    i2 = jnp.min(jnp.where(rest == m2, lane, ROUTE_LANES), axis=-1, keepdims=True)
    e2 = jnp.exp(m2 - m1)
    g1 = 1.0 / (1.0 + e2)
    g2 = e2 / (1.0 + e2)

    onehot = jnp.where((lane == i1) | (lane == i2), 1.0, 0.0)
    before = _dot(strict_ref[...], onehot.astype(BF16)) + carry_ref[0:1, :]
    r1 = jnp.sum(jnp.where(lane == i1, before, 0.0), axis=-1, keepdims=True)
    r2 = jnp.sum(jnp.where(lane == i2, before, 0.0), axis=-1, keepdims=True)
    total = carry_ref[0:1, :] + jnp.sum(onehot, axis=0, keepdims=True)
    carry_ref[...] = jnp.broadcast_to(total, carry_ref.shape)
    cnt_ref[...] = jnp.broadcast_to(total, cnt_ref.shape)

    route = jnp.where(lane == 0, i1.astype(F32), 0.0)
    route = jnp.where(lane == 1, i2.astype(F32), route)
    route = jnp.where(lane == 2, g1, route)
    route = jnp.where(lane == 3, g2, route)
    route = jnp.where(lane == 4, r1, route)
    route = jnp.where(lane == 5, r2, route)
    route_ref[...] = route


def mix_cd_router(x, c_out, d_a, d_b, wout_bf16, g, rw_hi, rw_lo, tm=TOK_TILE):
    n, d = x.shape
    tiles_a = d_a.shape[0] // tm
    const = lambda shape: pl.BlockSpec(shape, lambda i: (0,) * len(shape))
    tok = lambda w: pl.BlockSpec((tm, w), lambda i: (i, 0))
    strict = jnp.tril(jnp.ones((tm, tm), BF16), -1)
    return pl.pallas_call(
        functools.partial(_mix_cd_router_kernel, tm=tm, tiles_a=tiles_a),
        grid=(n // tm,),
        in_specs=[tok(d), tok(HALF), *_two_part_specs(tm, HALF, tiles_a), const((d, d)), const((1, d)),
                  const((d, ROUTE_LANES)), const((d, ROUTE_LANES)), const((tm, tm))],
        out_specs=[tok(d), tok(d // 2), tok(ROUTE_LANES), const((8, ROUTE_LANES))],
        out_shape=[jax.ShapeDtypeStruct((n, d), F32), jax.ShapeDtypeStruct((n, d // 2), jnp.uint32),
                   jax.ShapeDtypeStruct((n, ROUTE_LANES), F32), jax.ShapeDtypeStruct((8, ROUTE_LANES), F32)],
        scratch_shapes=[pltpu.VMEM((8, ROUTE_LANES), F32)],
        compiler_params=_cparams("arbitrary"),
        name="mix_cd_router",
    )(x, c_out, d_a, d_b, wout_bf16, g.reshape(1, d), rw_hi, rw_lo, strict)


_SC_WORKERS = SC_CORES * SC_SUBCORES


def _sc_mesh():
    return plsc.VectorSubcoreMesh(core_axis_name="c", subcore_axis_name="s")


def _sc_worker_id():
    return lax.axis_index("s") * SC_CORES + lax.axis_index("c")


def sc_dispatch(rows, dest0, dest1, pad_slots, n_slots, chunk=SC_ROW_CHUNK):
    n, w = rows.shape
    n_pad = pad_slots.shape[0]
    per_worker = n // _SC_WORKERS
    pad_per_worker = n_pad // _SC_WORKERS
    assert per_worker * _SC_WORKERS == n and per_worker % chunk == 0 and chunk % 8 == 0
    assert pad_per_worker * _SC_WORKERS == n_pad and pad_per_worker % chunk == 0

    @functools.partial(
        pl.kernel, mesh=_sc_mesh(),
        out_type=jax.ShapeDtypeStruct((n_slots, w), rows.dtype),
        scratch_types=[pltpu.VMEM((chunk,), jnp.int32), pltpu.VMEM((chunk, w), rows.dtype),
                       pltpu.SemaphoreType.DMA],
    )
    def scatter(rows_hbm, d0_hbm, d1_hbm, pad_hbm, zeros_hbm, out_hbm, idx_v, rows_v, sem):
        worker = _sc_worker_id()

        @pl.loop(0, per_worker // chunk)
        def _(c):
            off = worker * per_worker + c * chunk
            pltpu.sync_copy(rows_hbm.at[pl.ds(off, chunk)], rows_v)
            for d_hbm in (d0_hbm, d1_hbm):
                pltpu.sync_copy(d_hbm.at[pl.ds(off, chunk)], idx_v)
                pltpu.async_copy(rows_v, out_hbm.at[idx_v], sem).wait()

        pltpu.sync_copy(zeros_hbm, rows_v)

        @pl.loop(0, pad_per_worker // chunk)
        def _(c):
            off = worker * pad_per_worker + c * chunk
            pltpu.sync_copy(pad_hbm.at[pl.ds(off, chunk)], idx_v)
            pltpu.async_copy(rows_v, out_hbm.at[idx_v], sem).wait()

    return scatter(rows, dest0, dest1, pad_slots, jnp.zeros((chunk, w), rows.dtype))


def sc_row_gather(table, idx, chunk=SC_ROW_CHUNK):
    b, w = idx.shape[0], table.shape[1]
    per_worker = b // _SC_WORKERS
    assert per_worker * _SC_WORKERS == b and per_worker % chunk == 0 and chunk % 8 == 0

    @functools.partial(
        pl.kernel, mesh=_sc_mesh(),
        out_type=jax.ShapeDtypeStruct((b, w), table.dtype),
        scratch_types=[pltpu.VMEM((chunk,), jnp.int32), pltpu.VMEM((chunk, w), table.dtype),
                       pltpu.SemaphoreType.DMA],
    )
    def gather(table_hbm, idx_hbm, out_hbm, idx_v, rows_v, sem):
        worker = _sc_worker_id()

        @pl.loop(0, per_worker // chunk)
        def _(c):
            off = worker * per_worker + c * chunk
            pltpu.sync_copy(idx_hbm.at[pl.ds(off, chunk)], idx_v)
            pltpu.async_copy(table_hbm.at[idx_v], rows_v, sem).wait()
            pltpu.sync_copy(rows_v, out_hbm.at[pl.ds(off, chunk)])

    return gather(table, idx)


def _moe_ffn_kernel(be_ref, nu_ref, x_ref, w1_ref, w3_ref, w2_ref, o_ref, xbf_ref, acc_ref, *, rows_sub, cols_sub, nf):
    del be_ref
    b = pl.program_id(0)
    f = pl.program_id(1)
    active = b < nu_ref[0]
    tmb = x_ref.shape[0]
    tf = w1_ref.shape[2]
    half = x_ref.shape[1]

    @pl.when(active & (f == 0))
    def _():
        hi, lo = _unpack_bf16_pairs(x_ref[...])
        xbf_ref[:, 0:half] = hi.astype(BF16)
        xbf_ref[:, half:2 * half] = lo.astype(BF16)

    def partial_sums(first, last):
        for r in range(tmb // rows_sub):
            rows = slice(r * rows_sub, (r + 1) * rows_sub)
            xr = xbf_ref[rows, :]
            acc = jnp.zeros((rows_sub, 2 * half), F32) if first else acc_ref[rows, :]
            for c in range(tf // cols_sub):
                cols = slice(c * cols_sub, (c + 1) * cols_sub)
                act = (_silu(_dot(xr, w1_ref[0, :, cols])) * _dot(xr, w3_ref[0, :, cols])).astype(BF16)
                acc = acc + _dot(act, w2_ref[0, cols, :])
            if last:
                o_ref[rows, :] = _pack_bf16_pairs(acc.astype(BF16).astype(F32))
            else:
                acc_ref[rows, :] = acc

    @pl.when(active & (f == 0))
    def _():
        partial_sums(True, False)

    if nf > 2:
        @pl.when(active & (f > 0) & (f < nf - 1))
        def _():
            partial_sums(False, False)

    @pl.when(active & (f == nf - 1))
    def _():
        partial_sums(False, True)

    @pl.when(jnp.logical_not(active) & (f == 0))
    def _():
        o_ref[...] = jnp.zeros_like(o_ref)


def moe_ffn(xb, block_e, n_used, w1, w3, w2, tmb=MOE_BLOCK, tf=MOE_FTILE):
    n_slots, half = xb.shape
    d = 2 * half
    nb = n_slots // tmb
    nf = w1.shape[2] // tf
    assert nf >= 2, "the first hidden-column step initialises the sum and the last one writes the block"

    def bclamp(b, nu):
        return jnp.minimum(b, nu[0] - 1)

    def fclamp(b, f, nu):
        return jnp.where(b < nu[0], f, nf - 1)

    grid_spec = pltpu.PrefetchScalarGridSpec(
        num_scalar_prefetch=2,
        grid=(nb, nf),
        in_specs=[pl.BlockSpec((tmb, half), lambda b, f, be, nu: (bclamp(b, nu), 0)),
                  pl.BlockSpec((1, d, tf), lambda b, f, be, nu: (be[bclamp(b, nu)], 0, fclamp(b, f, nu))),
                  pl.BlockSpec((1, d, tf), lambda b, f, be, nu: (be[bclamp(b, nu)], 0, fclamp(b, f, nu))),
                  pl.BlockSpec((1, tf, d), lambda b, f, be, nu: (be[bclamp(b, nu)], fclamp(b, f, nu), 0))],
        out_specs=pl.BlockSpec((tmb, half), lambda b, f, be, nu: (b, 0)),
        scratch_shapes=[pltpu.VMEM((tmb, d), BF16), pltpu.VMEM((tmb, d), F32)],
    )
    return pl.pallas_call(
        functools.partial(_moe_ffn_kernel, rows_sub=MOE_ROWS_SUB, cols_sub=MOE_COLS_SUB, nf=nf),
        grid_spec=grid_spec,
        out_shape=jax.ShapeDtypeStruct((n_slots, half), jnp.uint32),
        compiler_params=_cparams("arbitrary", "arbitrary"),
        name="moe_ffn",
    )(block_e, n_used, xb, w1, w3, w2)


def _finish_kernel(x_ref, route_ref, g_ref, y0_ref, y1_ref, o_ref):
    half = y0_ref.shape[1]
    route = route_ref[...]
    g0, g1 = route[:, 2:3], route[:, 3:4]
    hi0, lo0 = _unpack_bf16_pairs(y0_ref[...])
    hi1, lo1 = _unpack_bf16_pairs(y1_ref[...])
    ya = x_ref[:, 0:half] + g0 * hi0 + g1 * hi1
    yb = x_ref[:, half:2 * half] + g0 * lo0 + g1 * lo1
    ms = (jnp.sum(ya * ya, axis=-1, keepdims=True) + jnp.sum(yb * yb, axis=-1, keepdims=True)) / (2 * half)
    inv = lax.rsqrt(ms + EPS)
    o_ref[:, 0:half] = ya * inv * g_ref[:, 0:half]
    o_ref[:, half:2 * half] = yb * inv * g_ref[:, half:2 * half]


def moe_finish(x, route, y_rows, g, row0, tm=TOK_TILE):
    d = x.shape[1]
    m = y_rows.shape[0] // 2
    n_tiles = m // tm
    t0 = row0 // tm
    return pl.pallas_call(
        _finish_kernel,
        grid=(n_tiles,),
        in_specs=[pl.BlockSpec((tm, d), lambda i: (t0 + i, 0)),
                  pl.BlockSpec((tm, ROUTE_LANES), lambda i: (t0 + i, 0)),
                  pl.BlockSpec((1, d), lambda i: (0, 0)),
                  pl.BlockSpec((tm, d // 2), lambda i: (i, 0)),
                  pl.BlockSpec((tm, d // 2), lambda i: (i + n_tiles, 0))],
        out_specs=pl.BlockSpec((tm, d), lambda i: (i, 0)),
        out_shape=jax.ShapeDtypeStruct((m, d), F32),
        compiler_params=_cparams("parallel"),
        name="moe_finish",
    )(x, route, g.reshape(1, d), y_rows, y_rows)


def _tile_flags(seq_lens, tile):
    firsts, lasts = [], []
    for length in seq_lens:
        k = length // tile
        firsts += [1] + [0] * (k - 1)
        lasts += [0] * (k - 1) + [1]
    return np.asarray(firsts, np.int32), np.asarray(lasts, np.int32)


def _dft_table_kernel(cphi_ref, sphi_ref, cth_ref, sth_ref, c_ref, s_ref):
    cphi, sphi = cphi_ref[...], sphi_ref[...]
    cth, sth = cth_ref[0], sth_ref[0]
    c_ref[...] = (cphi * cth - sphi * sth).astype(BF16)
    s_ref[...] = (sphi * cth + cphi * sth).astype(BF16)


def _angles(rows, t):
    k = lax.broadcasted_iota(jnp.int32, (rows.shape[0], t), 1)
    ang = ((rows[:, None] * k) % t).astype(F32) * (2.0 * np.pi / t)
    return jnp.cos(ang), jnp.sin(ang)


def _dft_tables(t):
    rows = min(DFT_GEN_ROWS, t)
    nt = t // rows
    cphi, sphi = _angles(jnp.arange(rows, dtype=jnp.int32), t)
    cth, sth = _angles(jnp.arange(nt, dtype=jnp.int32) * rows, t)
    tile = pl.BlockSpec((rows, t), lambda i: (0, 0))
    vec = pl.BlockSpec((1, 1, t), lambda i: (i, 0, 0))
    out = pl.BlockSpec((rows, t), lambda i: (i, 0))
    return pl.pallas_call(
        _dft_table_kernel,
        grid=(nt,),
        in_specs=[tile, tile, vec, vec],
        out_specs=[out, out],
        out_shape=[jax.ShapeDtypeStruct((t, t), BF16)] * 2,
        compiler_params=_cparams("parallel"),
        name="dft_table_%d" % t,
    )(cphi, sphi, cth.reshape(nt, 1, t), sth.reshape(nt, 1, t))


def kernel(x_prompt, x_sample, norm_mix, norm_ffn, norm_final, w_in_ab, hgrn_lb_logits, hgrn_out_norm, sgu_norm,
           sgu_w, sgu_b, w_out_ab, w_in_cd, conv_w, conv_b, conv_ln_g, conv_ln_b, w_out_cd, ffn_w1, ffn_w3, ffn_w2,
           router_w, moe_w1, moe_w3, moe_w2):
    bp, tp, d = x_prompt.shape
    bs, ts_, _ = x_sample.shape
    n_p, n_s = bp * tp, bs * ts_
    n = n_p + n_s
    seq_lens = [tp] * bp + [ts_] * bs
    tile = TOK_TILE
    firsts, lasts = _tile_flags(seq_lens, tile)
    depth = norm_mix.shape[0]
    assert depth == 2, "the layer schedule below is the two-layer trunk"
    xa, xb_in = x_prompt.reshape(n_p, d), x_sample.reshape(n_s, d)
    moe_bf16 = tuple(w[0].astype(BF16) for w in (moe_w1, moe_w3, moe_w2))

    for layer in range(depth):
        j = layer // 2
        if layer % 2 == 0:
            z = norm_proj(xa, xb_in, norm_mix[layer], w_in_ab[j].astype(BF16), ready_first=moe_bf16)
            o_f = hgrn_scan(z, hgrn_lb_logits, jnp.asarray(firsts), reverse=False, layer=layer, f_col=1)
            o_b = hgrn_scan(z, hgrn_lb_logits, jnp.asarray(lasts[::-1].copy()), reverse=True, layer=layer, f_col=2)
            x = mix_ab(xa, xb_in, o_f, o_b, z, hgrn_out_norm[j], sgu_norm[j], sgu_w[j].astype(BF16), sgu_b[j].T,
                       w_out_ab[j].astype(BF16))
            x = ffn(x, norm_ffn[layer], ffn_w1[j].astype(BF16), ffn_w3[j].astype(BF16), ffn_w2[j].astype(BF16))
        else:
            ang = (lax.broadcasted_iota(jnp.int32, (HEAD, HEAD), 0) * lax.broadcasted_iota(jnp.int32, (HEAD, HEAD), 1)
                   % HEAD).astype(F32) * (2.0 * np.pi / HEAD)
            dft_c = jnp.concatenate([jnp.cos(ang), jnp.sin(ang)], axis=1).astype(BF16)
            zc, wv = norm_proj_cd(x, norm_mix[layer], w_in_cd[j].astype(BF16), dft_c)
            c_out = conv_module(zc, jnp.asarray(firsts), jnp.asarray(lasts), conv_w[j], conv_b[j], conv_ln_g[j],
                                conv_ln_b[j])
            cos_p, sin_p = _dft_tables(tp)
            cos_s, sin_s = _dft_tables(ts_)
            d_p = seq_dft(wv, cos_p, sin_p, 0, bp, tp)
            d_s = seq_dft(wv, cos_s, sin_s, n_p, bs, ts_)

            rw = jnp.zeros((d, ROUTE_LANES), F32).at[:, :N_EXPERTS].set(router_w[j])
            rw_hi = rw.astype(BF16)
            rw_lo = (rw - rw_hi.astype(F32)).astype(BF16)
            x3, h, route, counts = mix_cd_router(x, c_out, d_p, d_s, w_out_cd[j].astype(BF16), norm_ffn[layer],
                                                 rw_hi, rw_lo)

            tmb = MOE_BLOCK
            n_blocks = (2 * n + tmb - 1) // tmb + N_EXPERTS
            cnt = counts[0, :N_EXPERTS].astype(jnp.int32)
            padded = (cnt + tmb - 1) // tmb * tmb
            pend = jnp.cumsum(padded)
            pstart = pend - padded
            e_idx = route[:, 0:2].astype(jnp.int32)
            dest = pstart[e_idx] + route[:, 4:6].astype(jnp.int32)
            dest0, dest1 = dest[:, 0], dest[:, 1]
            block_start = jnp.arange(n_blocks, dtype=jnp.int32) * tmb
            block_e = jnp.minimum(jnp.sum((pend[None, :] <= block_start[:, None]).astype(jnp.int32), axis=1),
                                  N_EXPERTS - 1)
            n_used = (pend[-1] // tmb).astype(jnp.int32).reshape(1)
            n_slots = n_blocks * tmb
            jpad = jnp.arange(tmb, dtype=jnp.int32)[None, :]
            pad_slots = jnp.where(jpad < (padded - cnt)[:, None], (pstart + cnt)[:, None] + jpad,
                                  n_slots - 1).reshape(N_EXPERTS * tmb)

            xb = sc_dispatch(h, dest0, dest1, pad_slots, n_slots)
            yb = moe_ffn(xb, block_e, n_used, *moe_bf16)
            outs = []
            for row0, m in ((0, n_p), (n_p, n_s)):
                idx = jnp.concatenate([dest0[row0:row0 + m], dest1[row0:row0 + m]])
                outs.append(moe_finish(x3, route, sc_row_gather(yb, idx), norm_final, row0))
            out_p, out_s = outs

    return out_p.reshape(bp, tp, d), out_s.reshape(bs, ts_, d)
```

```python
import functools

import numpy as np
import jax
import jax.numpy as jnp
from jax import lax
from jax.experimental import pallas as pl
from jax.experimental.pallas import tpu as pltpu
from jax.experimental.pallas import tpu_sc as plsc

F32 = jnp.float32
BF16 = jnp.bfloat16
EPS = 1e-6

D_MODEL = 1024
HALF = 512
HEAD = 128
N_HEADS = 4
HGRN_CHUNK = 64
HGRN_SUB = 16
HGRN_CHUNKS_PER_ITER = 4
CONV_K = 31
CONV_PAD = 15
CONV_HALO = 16
N_EXPERTS = 8
ROUTE_LANES = 128
MAX_EXP = 80.0

VMEM_LIMIT_BYTES = 56 * 1024 * 1024

TOK_TILE = 512
MOE_BLOCK = 1024
MOE_FTILE = 1792
MOE_ROWS_SUB = 512
MOE_COLS_SUB = 256
FFN_FTILE = 256
SC_CORES = 2
SC_SUBCORES = 16
SC_ROW_CHUNK = 128
SEQ_DFT_TILE_BYTES = 4 * 1024 * 1024
DFT_GEN_ROWS = 128


def _dot(a, b):
    return jnp.dot(a, b, preferred_element_type=F32)


def _dot_nt(a, b):
    return lax.dot_general(a, b, (((1,), (1,)), ((), ())), preferred_element_type=F32)


def _cparams(*sem):
    return pltpu.CompilerParams(dimension_semantics=sem, vmem_limit_bytes=VMEM_LIMIT_BYTES)


def _rms(x, g):
    ms = jnp.mean(x * x, axis=-1, keepdims=True)
    return x * lax.rsqrt(ms + EPS) * g


def _silu(x):
    return x * jax.nn.sigmoid(x)


def _pack_bf16_pairs(x):
    w = x.shape[1] // 2
    bits = lax.bitcast_convert_type(x, jnp.uint32)
    return (bits[:, :w] & jnp.uint32(0xFFFF0000)) | (bits[:, w:] >> 16)


def _unpack_bf16_pairs(words):
    hi = lax.bitcast_convert_type(words & jnp.uint32(0xFFFF0000), F32)
    lo = lax.bitcast_convert_type(words << 16, F32)
    return hi, lo


def _gelu_tanh(x):
    return 0.5 * x * (1.0 + jnp.tanh(0.7978845608028654 * (x + 0.044715 * (x * x * x))))


def _two_part_specs(tm, d, tiles_a):
    return (pl.BlockSpec((tm, d), lambda i: (jnp.minimum(i, tiles_a - 1), 0)),
            pl.BlockSpec((tm, d), lambda i: (jnp.maximum(i - tiles_a, 0), 0)))


def _two_part_read(xa_ref, xb_ref, tiles_a):
    return jnp.where(pl.program_id(0) < tiles_a, xa_ref[...], xb_ref[...])


def _norm_proj_kernel(xa_ref, xb_ref, g_ref, w_ref, *rest, tn, tiles_a):
    o_ref = rest[-1]
    h = _rms(_two_part_read(xa_ref, xb_ref, tiles_a), g_ref[...]).astype(BF16)
    for c in range(o_ref.shape[1] // tn):
        o_ref[:, c * tn:(c + 1) * tn] = _dot(h, w_ref[:, c * tn:(c + 1) * tn]).astype(o_ref.dtype)


def norm_proj(xa, xb, g, w_bf16, ready_first=(), tm=TOK_TILE, tn=512):
    d = xa.shape[1]
    n = xa.shape[0] + xb.shape[0]
    tiles_a = xa.shape[0] // tm
    nout = w_bf16.shape[1]
    return pl.pallas_call(
        functools.partial(_norm_proj_kernel, tn=tn, tiles_a=tiles_a),
        grid=(n // tm,),
        in_specs=[*_two_part_specs(tm, d, tiles_a),
                  pl.BlockSpec((1, d), lambda i: (0, 0)),
                  pl.BlockSpec((d, nout), lambda i: (0, 0)),
                  *[pl.BlockSpec(memory_space=pl.ANY) for _ in ready_first]],
        out_specs=pl.BlockSpec((tm, nout), lambda i: (i, 0)),
        out_shape=jax.ShapeDtypeStruct((n, nout), BF16),
        compiler_params=_cparams("parallel"),
        name="norm_proj_ab",
    )(xa, xb, g.reshape(1, d), w_bf16, *ready_first)


def _norm_proj_cd_kernel(x_ref, g_ref, w_ref, dft_ref, zc_ref, wv_ref):
    h = _rms(x_ref[...], g_ref[...]).astype(BF16)
    for c in range(2):
        zc_ref[:, c * HALF:(c + 1) * HALF] = _dot(h, w_ref[:, c * HALF:(c + 1) * HALF])
    d = _dot(h, w_ref[:, 2 * HALF:3 * HALF]).astype(BF16)
    for g in range(N_HEADS):
        r = _dot(d[:, g * HEAD:(g + 1) * HEAD], dft_ref[...])
        wv_ref[:, g * HEAD:(g + 1) * HEAD] = r[:, :HEAD].astype(BF16)
        wv_ref[:, HALF + g * HEAD:HALF + (g + 1) * HEAD] = r[:, HEAD:].astype(BF16)


def norm_proj_cd(x, g, w_bf16, dft_c, tm=TOK_TILE):
    n, d = x.shape
    return pl.pallas_call(
        _norm_proj_cd_kernel,
        grid=(n // tm,),
        in_specs=[pl.BlockSpec((tm, d), lambda i: (i, 0)),
                  pl.BlockSpec((1, d), lambda i: (0, 0)),
                  pl.BlockSpec((d, 3 * HALF), lambda i: (0, 0)),
                  pl.BlockSpec((HEAD, 2 * HEAD), lambda i: (0, 0))],
        out_specs=[pl.BlockSpec((tm, 2 * HALF), lambda i: (i, 0)),
                   pl.BlockSpec((tm, 2 * HALF), lambda i: (i, 0))],
        out_shape=[jax.ShapeDtypeStruct((n, 2 * HALF), F32),
                   jax.ShapeDtypeStruct((n, 2 * HALF), BF16)],
        compiler_params=_cparams("parallel"),
        name="norm_proj_cd",
    )(x, g.reshape(1, d), w_bf16, dft_c)


def _hgrn_kernel(reset_ref, q_ref, f_ref, v_ref, lbl_ref, o_ref, st0, st1, st2, st3, oi_ref, qe_ref, u_ref, d_ref,
                 *, reverse, layer, tt):
    c, sub = HGRN_CHUNK, HGRN_SUB
    n_sub = c // sub
    n_chunks = tt // c
    states = (st0, st1, st2, st3)

    @pl.when(reset_ref[pl.program_id(0)] == 1)
    def _():
        for st in states:
            st[...] = jnp.zeros_like(st)

    lg = lbl_ref[...]
    e = jnp.exp(lg - jnp.max(lg, axis=0, keepdims=True))
    p = e / jnp.sum(e, axis=0, keepdims=True)
    lb_all = p[0:1, :]
    for r in range(1, layer + 1):
        lb_all = lb_all + p[r:r + 1, :]

    row = lax.broadcasted_iota(jnp.int32, (c, c), 0)
    col = lax.broadcasted_iota(jnp.int32, (c, c), 1)
    tri = jnp.where((col >= row) if reverse else (col <= row), 1.0, 0.0).astype(BF16)
    spans, keeps = [], []
    for i in range(n_sub):
        span = slice(i * sub, c) if reverse else slice(0, (i + 1) * sub)
        n_span = span.stop - span.start
        tr = lax.broadcasted_iota(jnp.int32, (sub, n_span), 0)
        sc = lax.broadcasted_iota(jnp.int32, (sub, n_span), 1)
        spans.append(span)
        keeps.append((sc >= tr) if reverse else (sc <= tr + i * sub))

    units = [(cj, h) for cj in range(HGRN_CHUNKS_PER_ITER) for h in range(N_HEADS)]

    def local_body(it, carry):
        ci = [it * HGRN_CHUNKS_PER_ITER + cj for cj in range(HGRN_CHUNKS_PER_ITER)]
        r0 = [pl.multiple_of(x * c, c) for x in ci]
        vals = {}
        for cj, h in units:
            cols = slice(h * HEAD, (h + 1) * HEAD)
            lb = lb_all[:, cols]
            q = _silu(q_ref[pl.ds(r0[cj], c), cols].astype(F32))
            v = v_ref[pl.ds(r0[cj], c), cols].astype(F32)
            f = lb + (1.0 - lb) * jax.nn.sigmoid(f_ref[pl.ds(r0[cj], c), cols].astype(F32))
            lf = jnp.log(f)
            lf_hi = lf.astype(BF16)
            lf_lo = (lf - lf_hi.astype(F32)).astype(BF16)
            vals[cj, h] = (q, v, 1.0 - f, _dot(tri, lf_hi) + _dot(tri, lf_lo))
        scores = {}
        for cj, h in units:
            cols = slice(h * HEAD, (h + 1) * HEAD)
            q, v, k, b = vals[cj, h]
            qe_ref[pl.ds(r0[cj], c), cols] = (q * jnp.exp(b)).astype(BF16)
            edge = b[0:1, :] if reverse else b[c - 1:c, :]
            ks = (k * jnp.exp(edge - b)).astype(BF16)
            u_ref[ci[cj] * N_HEADS + h] = _dot(v.T.astype(BF16), ks)
            d_ref[ci[cj] * N_HEADS + h] = jnp.broadcast_to(jnp.exp(edge), (8, HEAD))
            for i in range(n_sub):
                rows = slice(i * sub, (i + 1) * sub)
                if reverse:
                    anchor = b[(i + 1) * sub:(i + 1) * sub + 1, :] if i + 1 < n_sub else jnp.zeros((1, HEAD), F32)
                else:
                    anchor = b[i * sub - 1:i * sub, :] if i > 0 else jnp.zeros((1, HEAD), F32)
                qi = (q[rows] * jnp.exp(b[rows] - anchor)).astype(BF16)
                ki = (k[spans[i]] * jnp.exp(jnp.minimum(anchor - b[spans[i]], MAX_EXP))).astype(BF16)
                scores[cj, h, i] = _dot_nt(qi, ki)
        for cj, h in units:
            cols = slice(h * HEAD, (h + 1) * HEAD)
            vb = vals[cj, h][1].astype(BF16)
            for i in range(n_sub):
                a = jnp.where(keeps[i], scores[cj, h, i], 0.0).astype(BF16)
                oi_ref[pl.ds(r0[cj] + i * sub, sub), cols] = _dot(a, vb[spans[i]])
        return carry

    lax.fori_loop(0, n_chunks // HGRN_CHUNKS_PER_ITER, local_body, 0)

    cur = [st[...] for st in states]
    for ci in range(n_chunks):
        cc = (n_chunks - 1 - ci) if reverse else ci
        for h in range(N_HEADS):
            cols = slice(h * HEAD, (h + 1) * HEAD)
            o_ref[cc * c:(cc + 1) * c, cols] = (
                oi_ref[cc * c:(cc + 1) * c, cols]
                + _dot_nt(qe_ref[cc * c:(cc + 1) * c, cols], cur[h].astype(BF16))).astype(o_ref.dtype)
        cur = [cur[h] * d_ref[cc * N_HEADS + h][0:1, :] + u_ref[cc * N_HEADS + h] for h in range(N_HEADS)]
    for h in range(N_HEADS):
        states[h][...] = cur[h]


def hgrn_scan(z, lb_logits, resets, *, reverse, layer, f_col, tt=TOK_TILE):
    n = z.shape[0]
    nt = n // tt
    n_units = tt // HGRN_CHUNK * N_HEADS
    order = (lambda i, r: (nt - 1 - i, 0)) if reverse else (lambda i, r: (i, 0))
    blk = lambda cb: (lambda i, r: (order(i, r)[0], cb))
    grid_spec = pltpu.PrefetchScalarGridSpec(
        num_scalar_prefetch=1,
        grid=(nt,),
        in_specs=[pl.BlockSpec((tt, HALF), blk(0)),
                  pl.BlockSpec((tt, HALF), blk(f_col)),
                  pl.BlockSpec((tt, HALF), blk(3)),
                  pl.BlockSpec(lb_logits.shape, lambda i, r: (0, 0))],
        out_specs=pl.BlockSpec((tt, HALF), order),
        scratch_shapes=[pltpu.VMEM((HEAD, HEAD), F32)] * N_HEADS + [
            pltpu.VMEM((tt, HALF), F32),
            pltpu.VMEM((tt, HALF), BF16),
            pltpu.VMEM((n_units, HEAD, HEAD), F32),
            pltpu.VMEM((n_units, 8, HEAD), F32)],
    )
    return pl.pallas_call(
        functools.partial(_hgrn_kernel, reverse=reverse, layer=layer, tt=tt),
        grid_spec=grid_spec,
        out_shape=jax.ShapeDtypeStruct((n, HALF), BF16),
        compiler_params=_cparams("arbitrary"),
        name="hgrn_bwd" if reverse else "hgrn_fwd",
    )(resets, z, z, z, lb_logits)


def _mix_ab_kernel(xa_ref, xb_ref, of_ref, ob_ref, g_ref, u_ref, v_ref, onorm_ref, snorm_ref, wsp_ref, bsp_ref,
                   wout_ref, o_ref, *, tm, tiles_a):
    o = of_ref[...].astype(F32) + ob_ref[...].astype(F32)
    g = g_ref[...].astype(F32)
    parts = []
    for h in range(N_HEADS):
        cols = slice(h * HEAD, (h + 1) * HEAD)
        parts.append(_rms(o[:, cols], onorm_ref[:, cols]) * _silu(g[:, cols]))
    a = jnp.concatenate(parts, axis=1).astype(BF16)
    acc = _two_part_read(xa_ref, xb_ref, tiles_a) + _dot(a, wout_ref[0:HALF, :])

    u = _gelu_tanh(u_ref[...].astype(F32))
    vb = _rms(_gelu_tanh(v_ref[...].astype(F32)), snorm_ref[...]).astype(BF16)
    rows = []
    for c in range(tm // HEAD):
        mixed = []
        for gi in range(N_HEADS):
            vg = vb[c * HEAD:(c + 1) * HEAD, gi * HEAD:(gi + 1) * HEAD]
            mixed.append(_dot(wsp_ref[gi], vg) + bsp_ref[:, gi:gi + 1])
        rows.append(jnp.concatenate(mixed, axis=1))
    b = (u * jnp.concatenate(rows, axis=0)).astype(BF16)
    o_ref[...] = acc + _dot(b, wout_ref[HALF:2 * HALF, :])


def mix_ab(xa, xb, o_f, o_b, z, onorm, snorm, wsp_bf16, bsp_t, wout_bf16, tm=TOK_TILE):
    d = xa.shape[1]
    n = xa.shape[0] + xb.shape[0]
    tiles_a = xa.shape[0] // tm
    zblk = lambda cb: pl.BlockSpec((tm, HALF), lambda i: (i, cb))
    const = lambda shape: pl.BlockSpec(shape, lambda i: (0,) * len(shape))
    return pl.pallas_call(
        functools.partial(_mix_ab_kernel, tm=tm, tiles_a=tiles_a),
        grid=(n // tm,),
        in_specs=[*_two_part_specs(tm, d, tiles_a),
                  pl.BlockSpec((tm, HALF), lambda i: (i, 0)),
                  pl.BlockSpec((tm, HALF), lambda i: (i, 0)),
                  zblk(4), zblk(5), zblk(6),
                  const((1, HALF)), const((1, HALF)),
                  const((N_HEADS, HEAD, HEAD)), const((HEAD, N_HEADS)),
                  const((d, d))],
        out_specs=pl.BlockSpec((tm, d), lambda i: (i, 0)),
        out_shape=jax.ShapeDtypeStruct((n, d), F32),
        compiler_params=_cparams("parallel"),
        name="mix_ab",
    )(xa, xb, o_f, o_b, z, z, z, onorm.reshape(1, HALF), snorm.reshape(1, HALF), wsp_bf16, bsp_t, wout_bf16)


def _ffn_kernel(x_ref, g_ref, w1_ref, w3_ref, w2_ref, o_ref, *, tf):
    x = x_ref[...]
    h = _rms(x, g_ref[...]).astype(BF16)
    acc = x
    for c in range(w1_ref.shape[1] // tf):
        cols = slice(c * tf, (c + 1) * tf)
        act = (_silu(_dot(h, w1_ref[:, cols])) * _dot(h, w3_ref[:, cols])).astype(BF16)
        acc = acc + _dot(act, w2_ref[cols, :])
    o_ref[...] = acc


def ffn(x, g, w1, w3, w2, tm=TOK_TILE, tf=FFN_FTILE):
    n, d = x.shape
    f = w1.shape[1]
    const = lambda shape: pl.BlockSpec(shape, lambda i: (0,) * len(shape))
    return pl.pallas_call(
        functools.partial(_ffn_kernel, tf=tf),
        grid=(n // tm,),
        in_specs=[pl.BlockSpec((tm, d), lambda i: (i, 0)), const((1, d)),
                  const((d, f)), const((d, f)), const((f, d))],
        out_specs=pl.BlockSpec((tm, d), lambda i: (i, 0)),
        out_shape=jax.ShapeDtypeStruct((n, d), F32),
        compiler_params=_cparams("parallel"),
        name="ffn_dense",
    )(x, g.reshape(1, d), w1, w3, w2)


def _conv_kernel(first_ref, last_ref, a_ref, gt_ref, ap_ref, gp_ref, an_ref, gn_ref, w_ref, b_ref, lng_ref,
                 lnb_ref, o_ref, ypad_ref, *, ts, rb):
    i = pl.program_id(0)
    halo = CONV_HALO
    keep_prev = jnp.where(first_ref[i] == 1, 0.0, 1.0)
    keep_next = jnp.where(last_ref[i] == 1, 0.0, 1.0)
    ypad_ref[0:halo, :] = ap_ref[...] * jax.nn.sigmoid(gp_ref[...]) * keep_prev
    ypad_ref[halo:halo + ts, :] = a_ref[...] * jax.nn.sigmoid(gt_ref[...])
    ypad_ref[halo + ts:2 * halo + ts, :] = an_ref[...] * jax.nn.sigmoid(gn_ref[...]) * keep_next

    n_win = rb + 2 * halo

    def body(bi, carry):
        r0 = pl.multiple_of(bi * rb, rb)
        strips = []
        for s in range(HALF // HEAD):
            cols = slice(s * HEAD, (s + 1) * HEAD)
            win = ypad_ref[pl.ds(r0, n_win), cols]
            acc = jnp.zeros((rb, HEAD), F32) + b_ref[:, cols]
            for res in range(8):
                sh = win if res == 0 else pltpu.roll(win, n_win - res, axis=0)
                for j in range(CONV_K):
                    off = halo - CONV_PAD + j
                    if off % 8 == res:
                        base = off - res
                        acc = acc + w_ref[j:j + 1, cols] * sh[base:base + rb, :]
            strips.append(acc)
        acc = jnp.concatenate(strips, axis=1)
        mu = jnp.mean(acc, axis=-1, keepdims=True)
        xc = acc - mu
        var = jnp.mean(xc * xc, axis=-1, keepdims=True)
        y = xc * lax.rsqrt(var + EPS) * lng_ref[...] + lnb_ref[...]
        o_ref[pl.ds(r0, rb), :] = _silu(y).astype(o_ref.dtype)
        return carry

    lax.fori_loop(0, ts // rb, body, 0)


def conv_module(zc, firsts, lasts, w, b, ln_g, ln_b, ts=TOK_TILE, rb=128):
    n = zc.shape[0]
    nt = n // ts
    hb = ts // CONV_HALO
    n_hb = n // CONV_HALO
    main = lambda cb: pl.BlockSpec((ts, HALF), lambda i, f, l: (i, cb))
    prev = lambda cb: pl.BlockSpec((CONV_HALO, HALF), lambda i, f, l: (jnp.maximum(i * hb - 1, 0), cb))
    nxt = lambda cb: pl.BlockSpec((CONV_HALO, HALF), lambda i, f, l: (jnp.minimum((i + 1) * hb, n_hb - 1), cb))
    const = lambda shape: pl.BlockSpec(shape, lambda i, f, l: (0,) * len(shape))
    grid_spec = pltpu.PrefetchScalarGridSpec(
        num_scalar_prefetch=2,
        grid=(nt,),
        in_specs=[main(0), main(1), prev(0), prev(1), nxt(0), nxt(1),
                  const((CONV_K, HALF)), const((1, HALF)), const((1, HALF)), const((1, HALF))],
        out_specs=pl.BlockSpec((ts, HALF), lambda i, f, l: (i, 0)),
        scratch_shapes=[pltpu.VMEM((ts + 2 * CONV_HALO, HALF), F32)],
    )
    return pl.pallas_call(
        functools.partial(_conv_kernel, ts=ts, rb=rb),
        grid_spec=grid_spec,
        out_shape=jax.ShapeDtypeStruct((n, HALF), BF16),
        compiler_params=_cparams("parallel"),
        name="conv_module",
    )(firsts, lasts, zc, zc, zc, zc, zc, zc, w, b.reshape(1, HALF), ln_g.reshape(1, HALF), ln_b.reshape(1, HALF))


def _seq_dft_kernel(c_ref, s_ref, wv_ref, o_ref, *, scale):
    acc = _dot(c_ref[...], wv_ref[:, 0:HALF]) - _dot(s_ref[...], wv_ref[:, HALF:2 * HALF])
    o_ref[...] = (acc * scale).astype(o_ref.dtype)


def seq_dft(wv, cos_t, sin_t, row0, n_seq, seq_len):
    tmo = min(seq_len, max(256, SEQ_DFT_TILE_BYTES // (2 * seq_len)))
    nt = seq_len // tmo
    b0 = row0 // seq_len
    scale = 1.0 / float(np.sqrt(seq_len * HEAD))
    return pl.pallas_call(
        functools.partial(_seq_dft_kernel, scale=scale),
        grid=(n_seq, nt),
        in_specs=[pl.BlockSpec((tmo, seq_len), lambda b, i: (i, 0)),
                  pl.BlockSpec((tmo, seq_len), lambda b, i: (i, 0)),
                  pl.BlockSpec((seq_len, 2 * HALF), lambda b, i: (b0 + b, 0))],
        out_specs=pl.BlockSpec((tmo, HALF), lambda b, i: (b * nt + i, 0)),
        out_shape=jax.ShapeDtypeStruct((n_seq * seq_len, HALF), BF16),
        compiler_params=_cparams("parallel", "parallel"),
        name="seq_dft_%d" % seq_len,
    )(cos_t, sin_t, wv)


def _mix_cd_router_kernel(x_ref, c_ref, da_ref, db_ref, wout_ref, g_ref, rwh_ref, rwl_ref, strict_ref, x3_ref, h_ref,
                          route_ref, cnt_ref, carry_ref, *, tm, tiles_a):
    @pl.when(pl.program_id(0) == 0)
    def _():
        carry_ref[...] = jnp.zeros_like(carry_ref)

    d_mix = _two_part_read(da_ref, db_ref, tiles_a)
    n_parts = 2
    rp = tm // n_parts
    parts = []
    for p in range(n_parts):
        rows = slice(p * rp, (p + 1) * rp)
        x3 = (x_ref[rows, :] + _dot(c_ref[rows, :], wout_ref[0:HALF, :])
              + _dot(d_mix[rows, :], wout_ref[HALF:2 * HALF, :]))
        x3_ref[rows, :] = x3
        parts.append(_rms(x3, g_ref[...]))
    logit_parts = []
    for p in range(n_parts):
        rows = slice(p * rp, (p + 1) * rp)
        h = parts[p]
        h_hi = h.astype(BF16)
        h_lo = (h - h_hi.astype(F32)).astype(BF16)
        h_ref[rows, :] = _pack_bf16_pairs(h_hi.astype(F32))
        logit_parts.append(_dot(h_hi, rwh_ref[...]) + (_dot(h_lo, rwh_ref[...]) + _dot(h_hi, rwl_ref[...])))
    logits = jnp.concatenate(logit_parts, axis=0)
    lane = lax.broadcasted_iota(jnp.int32, (tm, ROUTE_LANES), 1)
    neg = jnp.float32(-jnp.inf)
    logits = jnp.where(lane < N_EXPERTS, logits, neg)
    m1 = jnp.max(logits, axis=-1, keepdims=True)
    i1 = jnp.min(jnp.where(logits == m1, lane, ROUTE_LANES), axis=-1, keepdims=True)
    rest = jnp.where(lane == i1, neg, logits)
    m2 = jnp.max(rest, axis=-1, keepdims=True)
    i2 = jnp.min(jnp.where(rest == m2, lane, ROUTE_LANES), axis=-1, keepdims=True)
    e2 = jnp.exp(m2 - m1)
    g1 = 1.0 / (1.0 + e2)
    g2 = e2 / (1.0 + e2)

    onehot = jnp.where((lane == i1) | (lane == i2), 1.0, 0.0)
    before = _dot(strict_ref[...], onehot.astype(BF16)) + carry_ref[0:1, :]
    r1 = jnp.sum(jnp.where(lane == i1, before, 0.0), axis=-1, keepdims=True)
    r2 = jnp.sum(jnp.where(lane == i2, before, 0.0), axis=-1, keepdims=True)
    total = carry_ref[0:1, :] + jnp.sum(onehot, axis=0, keepdims=True)
    carry_ref[...] = jnp.broadcast_to(total, carry_ref.shape)
    cnt_ref[...] = jnp.broadcast_to(total, cnt_ref.shape)

    route = jnp.where(lane == 0, i1.astype(F32), 0.0)
    route = jnp.where(lane == 1, i2.astype(F32), route)
    route = jnp.where(lane == 2, g1, route)
    route = jnp.where(lane == 3, g2, route)
    route = jnp.where(lane == 4, r1, route)
    route = jnp.where(lane == 5, r2, route)
    route_ref[...] = route


def mix_cd_router(x, c_out, d_a, d_b, wout_bf16, g, rw_hi, rw_lo, tm=TOK_TILE):
    n, d = x.shape
    tiles_a = d_a.shape[0] // tm
    const = lambda shape: pl.BlockSpec(shape, lambda i: (0,) * len(shape))
    tok = lambda w: pl.BlockSpec((tm, w), lambda i: (i, 0))
    strict = jnp.tril(jnp.ones((tm, tm), BF16), -1)
    return pl.pallas_call(
        functools.partial(_mix_cd_router_kernel, tm=tm, tiles_a=tiles_a),
        grid=(n // tm,),
        in_specs=[tok(d), tok(HALF), *_two_part_specs(tm, HALF, tiles_a), const((d, d)), const((1, d)),
                  const((d, ROUTE_LANES)), const((d, ROUTE_LANES)), const((tm, tm))],
        out_specs=[tok(d), tok(d // 2), tok(ROUTE_LANES), const((8, ROUTE_LANES))],
        out_shape=[jax.ShapeDtypeStruct((n, d), F32), jax.ShapeDtypeStruct((n, d // 2), jnp.uint32),
                   jax.ShapeDtypeStruct((n, ROUTE_LANES), F32), jax.ShapeDtypeStruct((8, ROUTE_LANES), F32)],
        scratch_shapes=[pltpu.VMEM((8, ROUTE_LANES), F32)],
        compiler_params=_cparams("arbitrary"),
        name="mix_cd_router",
    )(x, c_out, d_a, d_b, wout_bf16, g.reshape(1, d), rw_hi, rw_lo, strict)


_SC_WORKERS = SC_CORES * SC_SUBCORES


def _sc_mesh():
    return plsc.VectorSubcoreMesh(core_axis_name="c", subcore_axis_name="s")


def _sc_worker_id():
    return lax.axis_index("s") * SC_CORES + lax.axis_index("c")


def sc_dispatch(rows, dest0, dest1, pad_slots, n_slots, chunk=SC_ROW_CHUNK):
    n, w = rows.shape
    n_pad = pad_slots.shape[0]
    per_worker = n // _SC_WORKERS
    pad_per_worker = n_pad // _SC_WORKERS
    assert per_worker * _SC_WORKERS == n and per_worker % chunk == 0 and chunk % 8 == 0
    assert pad_per_worker * _SC_WORKERS == n_pad and pad_per_worker % chunk == 0

    @functools.partial(
        pl.kernel, mesh=_sc_mesh(),
        out_type=jax.ShapeDtypeStruct((n_slots, w), rows.dtype),
        scratch_types=[pltpu.VMEM((chunk,), jnp.int32), pltpu.VMEM((chunk,), jnp.int32),
                       pltpu.VMEM((chunk, w), rows.dtype), pltpu.SemaphoreType.DMA, pltpu.SemaphoreType.DMA],
    )
    def scatter(rows_hbm, d0_hbm, d1_hbm, pad_hbm, zeros_hbm, out_hbm, idx0_v, idx1_v, rows_v, sem0, sem1):
        worker = _sc_worker_id()

        @pl.loop(0, per_worker // chunk)
        def _(c):
            off = worker * per_worker + c * chunk
            pltpu.sync_copy(rows_hbm.at[pl.ds(off, chunk)], rows_v)
            pltpu.sync_copy(d0_hbm.at[pl.ds(off, chunk)], idx0_v)
            pltpu.sync_copy(d1_hbm.at[pl.ds(off, chunk)], idx1_v)
            first = pltpu.async_copy(rows_v, out_hbm.at[idx0_v], sem0)
            second = pltpu.async_copy(rows_v, out_hbm.at[idx1_v], sem1)
            first.wait()
            second.wait()

        pltpu.sync_copy(zeros_hbm, rows_v)

        @pl.loop(0, pad_per_worker // chunk)
        def _(c):
            off = worker * pad_per_worker + c * chunk
            pltpu.sync_copy(pad_hbm.at[pl.ds(off, chunk)], idx0_v)
            pltpu.async_copy(rows_v, out_hbm.at[idx0_v], sem0).wait()

    return scatter(rows, dest0, dest1, pad_slots, jnp.zeros((chunk, w), rows.dtype))


def sc_row_gather(table, idx, chunk=SC_ROW_CHUNK // 2):
    b, w = idx.shape[0], table.shape[1]
    per_worker = b // _SC_WORKERS
    assert per_worker * _SC_WORKERS == b and per_worker % (2 * chunk) == 0 and chunk % 8 == 0

    @functools.partial(
        pl.kernel, mesh=_sc_mesh(),
        out_type=jax.ShapeDtypeStruct((b, w), table.dtype),
        scratch_types=[pltpu.VMEM((chunk,), jnp.int32), pltpu.VMEM((chunk,), jnp.int32),
                       pltpu.VMEM((chunk, w), table.dtype), pltpu.VMEM((chunk, w), table.dtype),
                       pltpu.SemaphoreType.DMA, pltpu.SemaphoreType.DMA],
    )
    def gather(table_hbm, idx_hbm, out_hbm, idx0_v, idx1_v, rows0_v, rows1_v, sem0, sem1):
        worker = _sc_worker_id()

        @pl.loop(0, per_worker // (2 * chunk))
        def _(c):
            off0 = worker * per_worker + c * (2 * chunk)
            off1 = off0 + chunk
            pltpu.sync_copy(idx_hbm.at[pl.ds(off0, chunk)], idx0_v)
            pltpu.sync_copy(idx_hbm.at[pl.ds(off1, chunk)], idx1_v)
            first = pltpu.async_copy(table_hbm.at[idx0_v], rows0_v, sem0)
            second = pltpu.async_copy(table_hbm.at[idx1_v], rows1_v, sem1)
            first.wait()
            pltpu.sync_copy(rows0_v, out_hbm.at[pl.ds(off0, chunk)])
            second.wait()
            pltpu.sync_copy(rows1_v, out_hbm.at[pl.ds(off1, chunk)])

    return gather(table, idx)


def _moe_ffn_kernel(be_ref, nu_ref, x_ref, w1_ref, w3_ref, w2_ref, o_ref, xbf_ref, acc_ref, *, rows_sub, cols_sub, nf):
    del be_ref
    b = pl.program_id(0)
    f = pl.program_id(1)
    active = b < nu_ref[0]
    tmb = x_ref.shape[0]
    tf = w1_ref.shape[2]
    half = x_ref.shape[1]

    @pl.when(active & (f == 0))
    def _():
        hi, lo = _unpack_bf16_pairs(x_ref[...])
        xbf_ref[:, 0:half] = hi.astype(BF16)
        xbf_ref[:, half:2 * half] = lo.astype(BF16)

    def partial_sums(first, last):
        for r in range(tmb // rows_sub):
            rows = slice(r * rows_sub, (r + 1) * rows_sub)
            xr = xbf_ref[rows, :]
            acc = jnp.zeros((rows_sub, 2 * half), F32) if first else acc_ref[rows, :]
            for c in range(tf // cols_sub):
                cols = slice(c * cols_sub, (c + 1) * cols_sub)
                act = (_silu(_dot(xr, w1_ref[0, :, cols])) * _dot(xr, w3_ref[0, :, cols])).astype(BF16)
                acc = acc + _dot(act, w2_ref[0, cols, :])
            if last:
                o_ref[rows, :] = _pack_bf16_pairs(acc.astype(BF16).astype(F32))
            else:
                acc_ref[rows, :] = acc

    @pl.when(active & (f == 0))
    def _():
        partial_sums(True, False)

    if nf > 2:
        @pl.when(active & (f > 0) & (f < nf - 1))
        def _():
            partial_sums(False, False)

    @pl.when(active & (f == nf - 1))
    def _():
        partial_sums(False, True)

    @pl.when(jnp.logical_not(active) & (f == 0))
    def _():
        o_ref[...] = jnp.zeros_like(o_ref)


def moe_ffn(xb, block_e, n_used, w1, w3, w2, tmb=MOE_BLOCK, tf=MOE_FTILE):
    n_slots, half = xb.shape
    d = 2 * half
    nb = n_slots // tmb
    nf = w1.shape[2] // tf
    assert nf >= 2, "the first hidden-column step initialises the sum and the last one writes the block"

    def bclamp(b, nu):
        return jnp.minimum(b, nu[0] - 1)

    def fclamp(b, f, nu):
        return jnp.where(b < nu[0], f, nf - 1)

    grid_spec = pltpu.PrefetchScalarGridSpec(
        num_scalar_prefetch=2,
        grid=(nb, nf),
        in_specs=[pl.BlockSpec((tmb, half), lambda b, f, be, nu: (bclamp(b, nu), 0)),
                  pl.BlockSpec((1, d, tf), lambda b, f, be, nu: (be[bclamp(b, nu)], 0, fclamp(b, f, nu))),
                  pl.BlockSpec((1, d, tf), lambda b, f, be, nu: (be[bclamp(b, nu)], 0, fclamp(b, f, nu))),
                  pl.BlockSpec((1, tf, d), lambda b, f, be, nu: (be[bclamp(b, nu)], fclamp(b, f, nu), 0))],
        out_specs=pl.BlockSpec((tmb, half), lambda b, f, be, nu: (b, 0)),
        scratch_shapes=[pltpu.VMEM((tmb, d), BF16), pltpu.VMEM((tmb, d), F32)],
    )
    return pl.pallas_call(
        functools.partial(_moe_ffn_kernel, rows_sub=MOE_ROWS_SUB, cols_sub=MOE_COLS_SUB, nf=nf),
        grid_spec=grid_spec,
        out_shape=jax.ShapeDtypeStruct((n_slots, half), jnp.uint32),
        compiler_params=_cparams("arbitrary", "arbitrary"),
        name="moe_ffn",
    )(block_e, n_used, xb, w1, w3, w2)


def _finish_kernel(x_ref, route_ref, g_ref, y0_ref, y1_ref, o_ref):
    half = y0_ref.shape[1]
    route = route_ref[...]
    g0, g1 = route[:, 2:3], route[:, 3:4]
    hi0, lo0 = _unpack_bf16_pairs(y0_ref[...])
    hi1, lo1 = _unpack_bf16_pairs(y1_ref[...])
    ya = x_ref[:, 0:half] + g0 * hi0 + g1 * hi1
    yb = x_ref[:, half:2 * half] + g0 * lo0 + g1 * lo1
    ms = (jnp.sum(ya * ya, axis=-1, keepdims=True) + jnp.sum(yb * yb, axis=-1, keepdims=True)) / (2 * half)
    inv = lax.rsqrt(ms + EPS)
    o_ref[:, 0:half] = ya * inv * g_ref[:, 0:half]
    o_ref[:, half:2 * half] = yb * inv * g_ref[:, half:2 * half]


def moe_finish(x, route, y_rows, g, row0, tm=TOK_TILE):
    d = x.shape[1]
    m = y_rows.shape[0] // 2
    n_tiles = m // tm
    t0 = row0 // tm
    return pl.pallas_call(
        _finish_kernel,
        grid=(n_tiles,),
        in_specs=[pl.BlockSpec((tm, d), lambda i: (t0 + i, 0)),
                  pl.BlockSpec((tm, ROUTE_LANES), lambda i: (t0 + i, 0)),
                  pl.BlockSpec((1, d), lambda i: (0, 0)),
                  pl.BlockSpec((tm, d // 2), lambda i: (i, 0)),
                  pl.BlockSpec((tm, d // 2), lambda i: (i + n_tiles, 0))],
        out_specs=pl.BlockSpec((tm, d), lambda i: (i, 0)),
        out_shape=jax.ShapeDtypeStruct((m, d), F32),
        compiler_params=_cparams("parallel"),
        name="moe_finish",
    )(x, route, g.reshape(1, d), y_rows, y_rows)


def _tile_flags(seq_lens, tile):
    firsts, lasts = [], []
    for length in seq_lens:
        k = length // tile
        firsts += [1] + [0] * (k - 1)
        lasts += [0] * (k - 1) + [1]
    return np.asarray(firsts, np.int32), np.asarray(lasts, np.int32)


def _dft_table_kernel(cphi_ref, sphi_ref, cth_ref, sth_ref, c_ref, s_ref):
    cphi, sphi = cphi_ref[...], sphi_ref[...]
    cth, sth = cth_ref[0], sth_ref[0]
    c_ref[...] = (cphi * cth - sphi * sth).astype(BF16)
    s_ref[...] = (sphi * cth + cphi * sth).astype(BF16)


def _angles(rows, t):
    k = lax.broadcasted_iota(jnp.int32, (rows.shape[0], t), 1)
    ang = ((rows[:, None] * k) % t).astype(F32) * (2.0 * np.pi / t)
    return jnp.cos(ang), jnp.sin(ang)


def _dft_tables(t):
    rows = min(DFT_GEN_ROWS, t)
    nt = t // rows
    cphi, sphi = _angles(jnp.arange(rows, dtype=jnp.int32), t)
    cth, sth = _angles(jnp.arange(nt, dtype=jnp.int32) * rows, t)
    tile = pl.BlockSpec((rows, t), lambda i: (0, 0))
    vec = pl.BlockSpec((1, 1, t), lambda i: (i, 0, 0))
    out = pl.BlockSpec((rows, t), lambda i: (i, 0))
    return pl.pallas_call(
        _dft_table_kernel,
        grid=(nt,),
        in_specs=[tile, tile, vec, vec],
        out_specs=[out, out],
        out_shape=[jax.ShapeDtypeStruct((t, t), BF16)] * 2,
        compiler_params=_cparams("parallel"),
        name="dft_table_%d" % t,
    )(cphi, sphi, cth.reshape(nt, 1, t), sth.reshape(nt, 1, t))


def kernel(x_prompt, x_sample, norm_mix, norm_ffn, norm_final, w_in_ab, hgrn_lb_logits, hgrn_out_norm, sgu_norm,
           sgu_w, sgu_b, w_out_ab, w_in_cd, conv_w, conv_b, conv_ln_g, conv_ln_b, w_out_cd, ffn_w1, ffn_w3, ffn_w2,
           router_w, moe_w1, moe_w3, moe_w2):
    bp, tp, d = x_prompt.shape
    bs, ts_, _ = x_sample.shape
    n_p, n_s = bp * tp, bs * ts_
    n = n_p + n_s
    seq_lens = [tp] * bp + [ts_] * bs
    tile = TOK_TILE
    firsts, lasts = _tile_flags(seq_lens, tile)
    depth = norm_mix.shape[0]
    assert depth == 2, "the layer schedule below is the two-layer trunk"
    xa, xb_in = x_prompt.reshape(n_p, d), x_sample.reshape(n_s, d)
    moe_bf16 = tuple(w[0].astype(BF16) for w in (moe_w1, moe_w3, moe_w2))

    for layer in range(depth):
        j = layer // 2
        if layer % 2 == 0:
            z = norm_proj(xa, xb_in, norm_mix[layer], w_in_ab[j].astype(BF16), ready_first=moe_bf16)
            o_f = hgrn_scan(z, hgrn_lb_logits, jnp.asarray(firsts), reverse=False, layer=layer, f_col=1)
            o_b = hgrn_scan(z, hgrn_lb_logits, jnp.asarray(lasts[::-1].copy()), reverse=True, layer=layer, f_col=2)
            x = mix_ab(xa, xb_in, o_f, o_b, z, hgrn_out_norm[j], sgu_norm[j], sgu_w[j].astype(BF16), sgu_b[j].T,
                       w_out_ab[j].astype(BF16))
            x = ffn(x, norm_ffn[layer], ffn_w1[j].astype(BF16), ffn_w3[j].astype(BF16), ffn_w2[j].astype(BF16))
        else:
            ang = (lax.broadcasted_iota(jnp.int32, (HEAD, HEAD), 0) * lax.broadcasted_iota(jnp.int32, (HEAD, HEAD), 1)
                   % HEAD).astype(F32) * (2.0 * np.pi / HEAD)
            dft_c = jnp.concatenate([jnp.cos(ang), jnp.sin(ang)], axis=1).astype(BF16)
            zc, wv = norm_proj_cd(x, norm_mix[layer], w_in_cd[j].astype(BF16), dft_c)
            c_out = conv_module(zc, jnp.asarray(firsts), jnp.asarray(lasts), conv_w[j], conv_b[j], conv_ln_g[j],
                                conv_ln_b[j])
            cos_p, sin_p = _dft_tables(tp)
            cos_s, sin_s = _dft_tables(ts_)
            d_p = seq_dft(wv, cos_p, sin_p, 0, bp, tp)
            d_s = seq_dft(wv, cos_s, sin_s, n_p, bs, ts_)

            rw = jnp.zeros((d, ROUTE_LANES), F32).at[:, :N_EXPERTS].set(router_w[j])
            rw_hi = rw.astype(BF16)
            rw_lo = (rw - rw_hi.astype(F32)).astype(BF16)
            x3, h, route, counts = mix_cd_router(x, c_out, d_p, d_s, w_out_cd[j].astype(BF16), norm_ffn[layer],
                                                 rw_hi, rw_lo)

            tmb = MOE_BLOCK
            n_blocks = (2 * n + tmb - 1) // tmb + N_EXPERTS
            cnt = counts[0, :N_EXPERTS].astype(jnp.int32)
            padded = (cnt + tmb - 1) // tmb * tmb
            pend = jnp.cumsum(padded)
            pstart = pend - padded
            e_idx = route[:, 0:2].astype(jnp.int32)
            dest = pstart[e_idx] + route[:, 4:6].astype(jnp.int32)
            dest0, dest1 = dest[:, 0], dest[:, 1]
            block_start = jnp.arange(n_blocks, dtype=jnp.int32) * tmb
            block_e = jnp.minimum(jnp.sum((pend[None, :] <= block_start[:, None]).astype(jnp.int32), axis=1),
                                  N_EXPERTS - 1)
            n_used = (pend[-1] // tmb).astype(jnp.int32).reshape(1)
            n_slots = n_blocks * tmb
            jpad = jnp.arange(tmb, dtype=jnp.int32)[None, :]
            pad_slots = jnp.where(jpad < (padded - cnt)[:, None], (pstart + cnt)[:, None] + jpad,
                                  n_slots - 1).reshape(N_EXPERTS * tmb)

            xb = sc_dispatch(h, dest0, dest1, pad_slots, n_slots)
            yb = moe_ffn(xb, block_e, n_used, *moe_bf16)
            outs = []
            for row0, m in ((0, n_p), (n_p, n_s)):
                idx = jnp.concatenate([dest0[row0:row0 + m], dest1[row0:row0 + m]])
                outs.append(moe_finish(x3, route, sc_row_gather(yb, idx), norm_final, row0))
            out_p, out_s = outs

    return out_p.reshape(bp, tp, d), out_s.reshape(bs, ts_, d)
```

```python
import functools

import numpy as np
import jax
import jax.numpy as jnp
from jax import lax
from jax.experimental import pallas as pl
from jax.experimental.pallas import tpu as pltpu
from jax.experimental.pallas import tpu_sc as plsc

F32 = jnp.float32
BF16 = jnp.bfloat16
EPS = 1e-6

D_MODEL = 1024
HALF = 512
HEAD = 128
N_HEADS = 4
HGRN_CHUNK = 64
HGRN_SUB = 16
HGRN_CHUNKS_PER_ITER = 8
CONV_K = 31
CONV_PAD = 15
CONV_HALO = 16
N_EXPERTS = 8
ROUTE_LANES = 128
MAX_EXP = 80.0

VMEM_LIMIT_BYTES = 56 * 1024 * 1024

TOK_TILE = 512
MOE_BLOCK = 1024
MOE_FTILE = 1792
MOE_ROWS_SUB = 512
MOE_COLS_SUB = 256
FFN_FTILE = 256
SC_CORES = 2
SC_SUBCORES = 16
SC_ROW_CHUNK = 128
SEQ_DFT_TILE_BYTES = 4 * 1024 * 1024
DFT_GEN_ROWS = 128


def _dot(a, b):
    return jnp.dot(a, b, preferred_element_type=F32)


def _dot_nt(a, b):
    return lax.dot_general(a, b, (((1,), (1,)), ((), ())), preferred_element_type=F32)


def _cparams(*sem):
    return pltpu.CompilerParams(dimension_semantics=sem, vmem_limit_bytes=VMEM_LIMIT_BYTES)


def _rms(x, g):
    ms = jnp.mean(x * x, axis=-1, keepdims=True)
    return x * lax.rsqrt(ms + EPS) * g


def _silu(x):
    return x * jax.nn.sigmoid(x)


def _pack_bf16_pairs(x):
    w = x.shape[1] // 2
    bits = lax.bitcast_convert_type(x, jnp.uint32)
    return (bits[:, :w] & jnp.uint32(0xFFFF0000)) | (bits[:, w:] >> 16)


def _unpack_bf16_pairs(words):
    hi = lax.bitcast_convert_type(words & jnp.uint32(0xFFFF0000), F32)
    lo = lax.bitcast_convert_type(words << 16, F32)
    return hi, lo


def _gelu_tanh(x):
    c = 0.7978845608028654
    return x * (0.5 + 0.5 * jnp.tanh(x * (c + (c * 0.044715) * (x * x))))


def _two_part_specs(tm, d, tiles_a):
    return (pl.BlockSpec((tm, d), lambda i: (jnp.minimum(i, tiles_a - 1), 0)),
            pl.BlockSpec((tm, d), lambda i: (jnp.maximum(i - tiles_a, 0), 0)))


def _two_part_read(xa_ref, xb_ref, tiles_a):
    return jnp.where(pl.program_id(0) < tiles_a, xa_ref[...], xb_ref[...])


def _norm_proj_kernel(xa_ref, xb_ref, g_ref, w_ref, *rest, tn, tiles_a):
    o_ref = rest[-1]
    h = _rms(_two_part_read(xa_ref, xb_ref, tiles_a), g_ref[...]).astype(BF16)
    for c in range(o_ref.shape[1] // tn):
        o_ref[:, c * tn:(c + 1) * tn] = _dot(h, w_ref[:, c * tn:(c + 1) * tn]).astype(o_ref.dtype)


def norm_proj(xa, xb, g, w_bf16, ready_first=(), tm=TOK_TILE, tn=512):
    d = xa.shape[1]
    n = xa.shape[0] + xb.shape[0]
    tiles_a = xa.shape[0] // tm
    nout = w_bf16.shape[1]
    return pl.pallas_call(
        functools.partial(_norm_proj_kernel, tn=tn, tiles_a=tiles_a),
        grid=(n // tm,),
        in_specs=[*_two_part_specs(tm, d, tiles_a),
                  pl.BlockSpec((1, d), lambda i: (0, 0)),
                  pl.BlockSpec((d, nout), lambda i: (0, 0)),
                  *[pl.BlockSpec(memory_space=pl.ANY) for _ in ready_first]],
        out_specs=pl.BlockSpec((tm, nout), lambda i: (i, 0)),
        out_shape=jax.ShapeDtypeStruct((n, nout), BF16),
        compiler_params=_cparams("parallel"),
        name="norm_proj_ab",
    )(xa, xb, g.reshape(1, d), w_bf16, *ready_first)


def _norm_proj_cd_kernel(x_ref, g_ref, w_ref, dft_ref, zc_ref, wv_ref):
    h = _rms(x_ref[...], g_ref[...]).astype(BF16)
    for c in range(2):
        zc_ref[:, c * HALF:(c + 1) * HALF] = _dot(h, w_ref[:, c * HALF:(c + 1) * HALF])
    d = _dot(h, w_ref[:, 2 * HALF:3 * HALF]).astype(BF16)
    for g in range(N_HEADS):
        r = _dot(d[:, g * HEAD:(g + 1) * HEAD], dft_ref[...])
        wv_ref[:, g * HEAD:(g + 1) * HEAD] = r[:, :HEAD].astype(BF16)
        wv_ref[:, HALF + g * HEAD:HALF + (g + 1) * HEAD] = r[:, HEAD:].astype(BF16)


def norm_proj_cd(x, g, w_bf16, dft_c, tm=TOK_TILE):
    n, d = x.shape
    return pl.pallas_call(
        _norm_proj_cd_kernel,
        grid=(n // tm,),
        in_specs=[pl.BlockSpec((tm, d), lambda i: (i, 0)),
                  pl.BlockSpec((1, d), lambda i: (0, 0)),
                  pl.BlockSpec((d, 3 * HALF), lambda i: (0, 0)),
                  pl.BlockSpec((HEAD, 2 * HEAD), lambda i: (0, 0))],
        out_specs=[pl.BlockSpec((tm, 2 * HALF), lambda i: (i, 0)),
                   pl.BlockSpec((tm, 2 * HALF), lambda i: (i, 0))],
        out_shape=[jax.ShapeDtypeStruct((n, 2 * HALF), F32),
                   jax.ShapeDtypeStruct((n, 2 * HALF), BF16)],
        compiler_params=_cparams("parallel"),
        name="norm_proj_cd",
    )(x, g.reshape(1, d), w_bf16, dft_c)


def _hgrn_kernel(reset_ref, q_ref, f_ref, v_ref, lbl_ref, o_ref, st0, st1, st2, st3, oi_ref, qe_ref, u_ref, d_ref,
                 *, reverse, layer, tt):
    c, sub = HGRN_CHUNK, HGRN_SUB
    n_sub = c // sub
    n_chunks = tt // c
    states = (st0, st1, st2, st3)

    @pl.when(reset_ref[pl.program_id(0)] == 1)
    def _():
        for st in states:
            st[...] = jnp.zeros_like(st)

    lg = lbl_ref[...]
    e = jnp.exp(lg - jnp.max(lg, axis=0, keepdims=True))
    p = e / jnp.sum(e, axis=0, keepdims=True)
    lb_all = p[0:1, :]
    for r in range(1, layer + 1):
        lb_all = lb_all + p[r:r + 1, :]

    row = lax.broadcasted_iota(jnp.int32, (c, c), 0)
    col = lax.broadcasted_iota(jnp.int32, (c, c), 1)
    tri = jnp.where((col >= row) if reverse else (col <= row), 1.0, 0.0).astype(BF16)
    spans, keeps = [], []
    for i in range(n_sub):
        span = slice(i * sub, c) if reverse else slice(0, (i + 1) * sub)
        n_span = span.stop - span.start
        tr = lax.broadcasted_iota(jnp.int32, (sub, n_span), 0)
        sc = lax.broadcasted_iota(jnp.int32, (sub, n_span), 1)
        spans.append(span)
        keeps.append((sc >= tr) if reverse else (sc <= tr + i * sub))

    units = [(cj, h) for cj in range(HGRN_CHUNKS_PER_ITER) for h in range(N_HEADS)]

    def local_body(it, carry):
        ci = [it * HGRN_CHUNKS_PER_ITER + cj for cj in range(HGRN_CHUNKS_PER_ITER)]
        r0 = [pl.multiple_of(x * c, c) for x in ci]
        vals = {}
        for cj, h in units:
            cols = slice(h * HEAD, (h + 1) * HEAD)
            lb = lb_all[:, cols]
            q = _silu(q_ref[pl.ds(r0[cj], c), cols].astype(F32))
            v = v_ref[pl.ds(r0[cj], c), cols].astype(F32)
            f = lb + (1.0 - lb) * jax.nn.sigmoid(f_ref[pl.ds(r0[cj], c), cols].astype(F32))
            lf = jnp.log(f)
            lf_hi = lf.astype(BF16)
            lf_lo = (lf - lf_hi.astype(F32)).astype(BF16)
            vals[cj, h] = (q, v, 1.0 - f, _dot(tri, lf_hi) + _dot(tri, lf_lo))
        scores = {}
        for cj, h in units:
            cols = slice(h * HEAD, (h + 1) * HEAD)
            q, v, k, b = vals[cj, h]
            qe_ref[pl.ds(r0[cj], c), cols] = (q * jnp.exp(b)).astype(BF16)
            edge = b[0:1, :] if reverse else b[c - 1:c, :]
            ks = (k * jnp.exp(edge - b)).astype(BF16)
            u_ref[ci[cj] * N_HEADS + h] = _dot(v.T.astype(BF16), ks)
            d_ref[ci[cj] * N_HEADS + h] = jnp.broadcast_to(jnp.exp(edge), (8, HEAD))
            for i in range(n_sub):
                rows = slice(i * sub, (i + 1) * sub)
                if reverse:
                    anchor = b[(i + 1) * sub:(i + 1) * sub + 1, :] if i + 1 < n_sub else jnp.zeros((1, HEAD), F32)
                else:
                    anchor = b[i * sub - 1:i * sub, :] if i > 0 else jnp.zeros((1, HEAD), F32)
                qi = (q[rows] * jnp.exp(b[rows] - anchor)).astype(BF16)
                ki = (k[spans[i]] * jnp.exp(jnp.minimum(anchor - b[spans[i]], MAX_EXP))).astype(BF16)
                scores[cj, h, i] = _dot_nt(qi, ki)
        for cj, h in units:
            cols = slice(h * HEAD, (h + 1) * HEAD)
            vb = vals[cj, h][1].astype(BF16)
            for i in range(n_sub):
                a = jnp.where(keeps[i], scores[cj, h, i], 0.0).astype(BF16)
                oi_ref[pl.ds(r0[cj] + i * sub, sub), cols] = _dot(a, vb[spans[i]])
        return carry

    lax.fori_loop(0, n_chunks // HGRN_CHUNKS_PER_ITER, local_body, 0)

    cur = [st[...] for st in states]
    for ci in range(n_chunks):
        cc = (n_chunks - 1 - ci) if reverse else ci
        for h in range(N_HEADS):
            cols = slice(h * HEAD, (h + 1) * HEAD)
            o_ref[cc * c:(cc + 1) * c, cols] = (
                oi_ref[cc * c:(cc + 1) * c, cols]
                + _dot_nt(qe_ref[cc * c:(cc + 1) * c, cols], cur[h].astype(BF16))).astype(o_ref.dtype)
        cur = [cur[h] * d_ref[cc * N_HEADS + h][0:1, :] + u_ref[cc * N_HEADS + h] for h in range(N_HEADS)]
    for h in range(N_HEADS):
        states[h][...] = cur[h]


def hgrn_scan(z, lb_logits, resets, *, reverse, layer, f_col, tt=TOK_TILE):
    n = z.shape[0]
    nt = n // tt
    n_units = tt // HGRN_CHUNK * N_HEADS
    order = (lambda i, r: (nt - 1 - i, 0)) if reverse else (lambda i, r: (i, 0))
    blk = lambda cb: (lambda i, r: (order(i, r)[0], cb))
    grid_spec = pltpu.PrefetchScalarGridSpec(
        num_scalar_prefetch=1,
        grid=(nt,),
        in_specs=[pl.BlockSpec((tt, HALF), blk(0)),
                  pl.BlockSpec((tt, HALF), blk(f_col)),
                  pl.BlockSpec((tt, HALF), blk(3)),
                  pl.BlockSpec(lb_logits.shape, lambda i, r: (0, 0))],
        out_specs=pl.BlockSpec((tt, HALF), order),
        scratch_shapes=[pltpu.VMEM((HEAD, HEAD), F32)] * N_HEADS + [
            pltpu.VMEM((tt, HALF), F32),
            pltpu.VMEM((tt, HALF), BF16),
            pltpu.VMEM((n_units, HEAD, HEAD), F32),
            pltpu.VMEM((n_units, 8, HEAD), F32)],
    )
    return pl.pallas_call(
        functools.partial(_hgrn_kernel, reverse=reverse, layer=layer, tt=tt),
        grid_spec=grid_spec,
        out_shape=jax.ShapeDtypeStruct((n, HALF), BF16),
        compiler_params=_cparams("arbitrary"),
        name="hgrn_bwd" if reverse else "hgrn_fwd",
    )(resets, z, z, z, lb_logits)


def _mix_ab_kernel(xa_ref, xb_ref, of_ref, ob_ref, g_ref, u_ref, v_ref, onorm_ref, snorm_ref, wsp_ref, bsp_ref,
                   wout_ref, o_ref, *, tm, tiles_a):
    o = of_ref[...].astype(F32) + ob_ref[...].astype(F32)
    g = g_ref[...].astype(F32)
    parts = []
    for h in range(N_HEADS):
        cols = slice(h * HEAD, (h + 1) * HEAD)
        parts.append(_rms(o[:, cols], onorm_ref[:, cols]) * _silu(g[:, cols]))
    a = jnp.concatenate(parts, axis=1).astype(BF16)
    acc = _two_part_read(xa_ref, xb_ref, tiles_a) + _dot(a, wout_ref[0:HALF, :])

    u = _gelu_tanh(u_ref[...].astype(F32))
    vb = _rms(_gelu_tanh(v_ref[...].astype(F32)), snorm_ref[...]).astype(BF16)
    rows = []
    for c in range(tm // HEAD):
        mixed = []
        for gi in range(N_HEADS):
            vg = vb[c * HEAD:(c + 1) * HEAD, gi * HEAD:(gi + 1) * HEAD]
            mixed.append(_dot(wsp_ref[gi], vg) + bsp_ref[:, gi:gi + 1])
        rows.append(jnp.concatenate(mixed, axis=1))
    b = (u * jnp.concatenate(rows, axis=0)).astype(BF16)
    o_ref[...] = acc + _dot(b, wout_ref[HALF:2 * HALF, :])


def mix_ab(xa, xb, o_f, o_b, z, onorm, snorm, wsp_bf16, bsp_t, wout_bf16, tm=TOK_TILE):
    d = xa.shape[1]
    n = xa.shape[0] + xb.shape[0]
    tiles_a = xa.shape[0] // tm
    zblk = lambda cb: pl.BlockSpec((tm, HALF), lambda i: (i, cb))
    const = lambda shape: pl.BlockSpec(shape, lambda i: (0,) * len(shape))
    return pl.pallas_call(
        functools.partial(_mix_ab_kernel, tm=tm, tiles_a=tiles_a),
        grid=(n // tm,),
        in_specs=[*_two_part_specs(tm, d, tiles_a),
                  pl.BlockSpec((tm, HALF), lambda i: (i, 0)),
                  pl.BlockSpec((tm, HALF), lambda i: (i, 0)),
                  zblk(4), zblk(5), zblk(6),
                  const((1, HALF)), const((1, HALF)),
                  const((N_HEADS, HEAD, HEAD)), const((HEAD, N_HEADS)),
                  const((d, d))],
        out_specs=pl.BlockSpec((tm, d), lambda i: (i, 0)),
        out_shape=jax.ShapeDtypeStruct((n, d), F32),
        compiler_params=_cparams("parallel"),
        name="mix_ab",
    )(xa, xb, o_f, o_b, z, z, z, onorm.reshape(1, HALF), snorm.reshape(1, HALF), wsp_bf16, bsp_t, wout_bf16)


def _ffn_kernel(x_ref, g_ref, w1_ref, w3_ref, w2_ref, o_ref, *, tf):
    x = x_ref[...]
    h = _rms(x, g_ref[...]).astype(BF16)
    acc = x
    for c in range(w1_ref.shape[1] // tf):
        cols = slice(c * tf, (c + 1) * tf)
        act = (_silu(_dot(h, w1_ref[:, cols])) * _dot(h, w3_ref[:, cols])).astype(BF16)
        acc = acc + _dot(act, w2_ref[cols, :])
    o_ref[...] = acc


def ffn(x, g, w1, w3, w2, tm=TOK_TILE, tf=FFN_FTILE):
    n, d = x.shape
    f = w1.shape[1]
    const = lambda shape: pl.BlockSpec(shape, lambda i: (0,) * len(shape))
    return pl.pallas_call(
        functools.partial(_ffn_kernel, tf=tf),
        grid=(n // tm,),
        in_specs=[pl.BlockSpec((tm, d), lambda i: (i, 0)), const((1, d)),
                  const((d, f)), const((d, f)), const((f, d))],
        out_specs=pl.BlockSpec((tm, d), lambda i: (i, 0)),
        out_shape=jax.ShapeDtypeStruct((n, d), F32),
        compiler_params=_cparams("parallel"),
        name="ffn_dense",
    )(x, g.reshape(1, d), w1, w3, w2)


def _conv_kernel(first_ref, last_ref, a_ref, gt_ref, ap_ref, gp_ref, an_ref, gn_ref, w_ref, b_ref, lng_ref,
                 lnb_ref, o_ref, ypad_ref, *, ts, rb):
    i = pl.program_id(0)
    halo = CONV_HALO
    keep_prev = jnp.where(first_ref[i] == 1, 0.0, 1.0)
    keep_next = jnp.where(last_ref[i] == 1, 0.0, 1.0)
    ypad_ref[0:halo, :] = ap_ref[...] * jax.nn.sigmoid(gp_ref[...]) * keep_prev
    ypad_ref[halo:halo + ts, :] = a_ref[...] * jax.nn.sigmoid(gt_ref[...])
    ypad_ref[halo + ts:2 * halo + ts, :] = an_ref[...] * jax.nn.sigmoid(gn_ref[...]) * keep_next

    n_win = rb + 2 * halo

    def body(bi, carry):
        r0 = pl.multiple_of(bi * rb, rb)
        strips = []
        for s in range(HALF // HEAD):
            cols = slice(s * HEAD, (s + 1) * HEAD)
            win = ypad_ref[pl.ds(r0, n_win), cols]
            acc = jnp.zeros((rb, HEAD), F32) + b_ref[:, cols]
            for res in range(8):
                sh = win if res == 0 else pltpu.roll(win, n_win - res, axis=0)
                for j in range(CONV_K):
                    off = halo - CONV_PAD + j
                    if off % 8 == res:
                        base = off - res
                        acc = acc + w_ref[j:j + 1, cols] * sh[base:base + rb, :]
            strips.append(acc)
        acc = jnp.concatenate(strips, axis=1)
        mu = jnp.mean(acc, axis=-1, keepdims=True)
        xc = acc - mu
        var = jnp.mean(xc * xc, axis=-1, keepdims=True)
        y = xc * lax.rsqrt(var + EPS) * lng_ref[...] + lnb_ref[...]
        o_ref[pl.ds(r0, rb), :] = _silu(y).astype(o_ref.dtype)
        return carry

    lax.fori_loop(0, ts // rb, body, 0)


def conv_module(zc, firsts, lasts, w, b, ln_g, ln_b, ts=TOK_TILE, rb=128):
    n = zc.shape[0]
    nt = n // ts
    hb = ts // CONV_HALO
    n_hb = n // CONV_HALO
    main = lambda cb: pl.BlockSpec((ts, HALF), lambda i, f, l: (i, cb))
    prev = lambda cb: pl.BlockSpec((CONV_HALO, HALF), lambda i, f, l: (jnp.maximum(i * hb - 1, 0), cb))
    nxt = lambda cb: pl.BlockSpec((CONV_HALO, HALF), lambda i, f, l: (jnp.minimum((i + 1) * hb, n_hb - 1), cb))
    const = lambda shape: pl.BlockSpec(shape, lambda i, f, l: (0,) * len(shape))
    grid_spec = pltpu.PrefetchScalarGridSpec(
        num_scalar_prefetch=2,
        grid=(nt,),
        in_specs=[main(0), main(1), prev(0), prev(1), nxt(0), nxt(1),
                  const((CONV_K, HALF)), const((1, HALF)), const((1, HALF)), const((1, HALF))],
        out_specs=pl.BlockSpec((ts, HALF), lambda i, f, l: (i, 0)),
        scratch_shapes=[pltpu.VMEM((ts + 2 * CONV_HALO, HALF), F32)],
    )
    return pl.pallas_call(
        functools.partial(_conv_kernel, ts=ts, rb=rb),
        grid_spec=grid_spec,
        out_shape=jax.ShapeDtypeStruct((n, HALF), BF16),
        compiler_params=_cparams("parallel"),
        name="conv_module",
    )(firsts, lasts, zc, zc, zc, zc, zc, zc, w, b.reshape(1, HALF), ln_g.reshape(1, HALF), ln_b.reshape(1, HALF))


def _seq_dft_kernel(c_ref, s_ref, wv_ref, o_ref, *, scale):
    acc = _dot(c_ref[...], wv_ref[:, 0:HALF]) - _dot(s_ref[...], wv_ref[:, HALF:2 * HALF])
    o_ref[...] = (acc * scale).astype(o_ref.dtype)


def seq_dft(wv, cos_t, sin_t, row0, n_seq, seq_len):
    tmo = min(seq_len, max(256, SEQ_DFT_TILE_BYTES // (2 * seq_len)))
    nt = seq_len // tmo
    b0 = row0 // seq_len
    scale = 1.0 / float(np.sqrt(seq_len * HEAD))
    return pl.pallas_call(
        functools.partial(_seq_dft_kernel, scale=scale),
        grid=(n_seq, nt),
        in_specs=[pl.BlockSpec((tmo, seq_len), lambda b, i: (i, 0)),
                  pl.BlockSpec((tmo, seq_len), lambda b, i: (i, 0)),
                  pl.BlockSpec((seq_len, 2 * HALF), lambda b, i: (b0 + b, 0))],
        out_specs=pl.BlockSpec((tmo, HALF), lambda b, i: (b * nt + i, 0)),
        out_shape=jax.ShapeDtypeStruct((n_seq * seq_len, HALF), BF16),
        compiler_params=_cparams("parallel", "parallel"),
        name="seq_dft_%d" % seq_len,
    )(cos_t, sin_t, wv)


def _mix_cd_router_kernel(x_ref, c_ref, da_ref, db_ref, wout_ref, g_ref, rwh_ref, rwl_ref, strict_ref, x3_ref, h_ref,
                          route_ref, cnt_ref, carry_ref, *, tm, tiles_a):
    @pl.when(pl.program_id(0) == 0)
    def _():
        carry_ref[...] = jnp.zeros_like(carry_ref)

    d_mix = _two_part_read(da_ref, db_ref, tiles_a)
    n_parts = 2
    rp = tm // n_parts
    parts = []
    for p in range(n_parts):
        rows = slice(p * rp, (p + 1) * rp)
        x3 = (x_ref[rows, :] + _dot(c_ref[rows, :], wout_ref[0:HALF, :])
              + _dot(d_mix[rows, :], wout_ref[HALF:2 * HALF, :]))
        x3_ref[rows, :] = x3
        parts.append(_rms(x3, g_ref[...]))
    logit_parts = []
    for p in range(n_parts):
        rows = slice(p * rp, (p + 1) * rp)
        h = parts[p]
        h_hi = h.astype(BF16)
        h_lo = (h - h_hi.astype(F32)).astype(BF16)
        h_ref[rows, :] = _pack_bf16_pairs(h_hi.astype(F32))
        logit_parts.append(_dot(h_hi, rwh_ref[...]) + (_dot(h_lo, rwh_ref[...]) + _dot(h_hi, rwl_ref[...])))
    logits = jnp.concatenate(logit_parts, axis=0)
    lane = lax.broadcasted_iota(jnp.int32, (tm, ROUTE_LANES), 1)
    neg = jnp.float32(-jnp.inf)
    logits = jnp.where(lane < N_EXPERTS, logits, neg)
    m1 = jnp.max(logits, axis=-1, keepdims=True)
    i1 = jnp.min(jnp.where(logits == m1, lane, ROUTE_LANES), axis=-1, keepdims=True)
    rest = jnp.where(lane == i1, neg, logits)
    m2 = jnp.max(rest, axis=-1, keepdims=True)
    i2 = jnp.min(jnp.where(rest == m2, lane, ROUTE_LANES), axis=-1, keepdims=True)
    e2 = jnp.exp(m2 - m1)
    g1 = 1.0 / (1.0 + e2)
    g2 = e2 / (1.0 + e2)

    onehot = jnp.where((lane == i1) | (lane == i2), 1.0, 0.0)
    before = _dot(strict_ref[...], onehot.astype(BF16)) + carry_ref[0:1, :]
    r1 = jnp.sum(jnp.where(lane == i1, before, 0.0), axis=-1, keepdims=True)
    r2 = jnp.sum(jnp.where(lane == i2, before, 0.0), axis=-1, keepdims=True)
    total = carry_ref[0:1, :] + jnp.sum(onehot, axis=0, keepdims=True)
    carry_ref[...] = jnp.broadcast_to(total, carry_ref.shape)
    cnt_ref[...] = jnp.broadcast_to(total, cnt_ref.shape)

    route = jnp.where(lane == 0, i1.astype(F32), 0.0)
    route = jnp.where(lane == 1, i2.astype(F32), route)
    route = jnp.where(lane == 2, g1, route)
    route = jnp.where(lane == 3, g2, route)
    route = jnp.where(lane == 4, r1, route)
    route = jnp.where(lane == 5, r2, route)
    route_ref[...] = route


def mix_cd_router(x, c_out, d_a, d_b, wout_bf16, g, rw_hi, rw_lo, tm=TOK_TILE):
    n, d = x.shape
    tiles_a = d_a.shape[0] // tm
    const = lambda shape: pl.BlockSpec(shape, lambda i: (0,) * len(shape))
    tok = lambda w: pl.BlockSpec((tm, w), lambda i: (i, 0))
    strict = jnp.tril(jnp.ones((tm, tm), BF16), -1)
    return pl.pallas_call(
        functools.partial(_mix_cd_router_kernel, tm=tm, tiles_a=tiles_a),
        grid=(n // tm,),
        in_specs=[tok(d), tok(HALF), *_two_part_specs(tm, HALF, tiles_a), const((d, d)), const((1, d)),
                  const((d, ROUTE_LANES)), const((d, ROUTE_LANES)), const((tm, tm))],
        out_specs=[tok(d), tok(d // 2), tok(ROUTE_LANES), const((8, ROUTE_LANES))],
        out_shape=[jax.ShapeDtypeStruct((n, d), F32), jax.ShapeDtypeStruct((n, d // 2), jnp.uint32),
                   jax.ShapeDtypeStruct((n, ROUTE_LANES), F32), jax.ShapeDtypeStruct((8, ROUTE_LANES), F32)],
        scratch_shapes=[pltpu.VMEM((8, ROUTE_LANES), F32)],
        compiler_params=_cparams("arbitrary"),
        name="mix_cd_router",
    )(x, c_out, d_a, d_b, wout_bf16, g.reshape(1, d), rw_hi, rw_lo, strict)


_SC_WORKERS = SC_CORES * SC_SUBCORES


def _sc_mesh():
    return plsc.VectorSubcoreMesh(core_axis_name="c", subcore_axis_name="s")


def _sc_worker_id():
    return lax.axis_index("s") * SC_CORES + lax.axis_index("c")


def sc_dispatch(rows, dest0, dest1, pad_slots, n_slots, chunk=SC_ROW_CHUNK):
    n, w = rows.shape
    n_pad = pad_slots.shape[0]
    per_worker = n // _SC_WORKERS
    pad_per_worker = n_pad // _SC_WORKERS
    assert per_worker * _SC_WORKERS == n and per_worker % chunk == 0 and chunk % 8 == 0
    assert pad_per_worker * _SC_WORKERS == n_pad and pad_per_worker % chunk == 0

    @functools.partial(
        pl.kernel, mesh=_sc_mesh(),
        out_type=jax.ShapeDtypeStruct((n_slots, w), rows.dtype),
        scratch_types=[pltpu.VMEM((chunk,), jnp.int32), pltpu.VMEM((chunk, w), rows.dtype),
                       pltpu.SemaphoreType.DMA],
    )
    def scatter(rows_hbm, d0_hbm, d1_hbm, pad_hbm, zeros_hbm, out_hbm, idx_v, rows_v, sem):
        worker = _sc_worker_id()

        @pl.loop(0, per_worker // chunk)
        def _(c):
            off = worker * per_worker + c * chunk
            pltpu.sync_copy(rows_hbm.at[pl.ds(off, chunk)], rows_v)
            for d_hbm in (d0_hbm, d1_hbm):
                pltpu.sync_copy(d_hbm.at[pl.ds(off, chunk)], idx_v)
                pltpu.async_copy(rows_v, out_hbm.at[idx_v], sem).wait()

        pltpu.sync_copy(zeros_hbm, rows_v)

        @pl.loop(0, pad_per_worker // chunk)
        def _(c):
            off = worker * pad_per_worker + c * chunk
            pltpu.sync_copy(pad_hbm.at[pl.ds(off, chunk)], idx_v)
            pltpu.async_copy(rows_v, out_hbm.at[idx_v], sem).wait()

    return scatter(rows, dest0, dest1, pad_slots, jnp.zeros((chunk, w), rows.dtype))


def sc_row_gather(table, idx, chunk=SC_ROW_CHUNK):
    b, w = idx.shape[0], table.shape[1]
    per_worker = b // _SC_WORKERS
    assert per_worker * _SC_WORKERS == b and per_worker % chunk == 0 and chunk % 8 == 0

    @functools.partial(
        pl.kernel, mesh=_sc_mesh(),
        out_type=jax.ShapeDtypeStruct((b, w), table.dtype),
        scratch_types=[pltpu.VMEM((chunk,), jnp.int32), pltpu.VMEM((chunk, w), table.dtype),
                       pltpu.SemaphoreType.DMA],
    )
    def gather(table_hbm, idx_hbm, out_hbm, idx_v, rows_v, sem):
        worker = _sc_worker_id()

        @pl.loop(0, per_worker // chunk)
        def _(c):
            off = worker * per_worker + c * chunk
            pltpu.sync_copy(idx_hbm.at[pl.ds(off, chunk)], idx_v)
            pltpu.async_copy(table_hbm.at[idx_v], rows_v, sem).wait()
            pltpu.sync_copy(rows_v, out_hbm.at[pl.ds(off, chunk)])

    return gather(table, idx)


def _moe_ffn_kernel(be_ref, nu_ref, x_ref, w1_ref, w3_ref, w2_ref, o_ref, xbf_ref, acc_ref, *, rows_sub, cols_sub, nf):
    del be_ref
    b = pl.program_id(0)
    f = pl.program_id(1)
    active = b < nu_ref[0]
    tmb = x_ref.shape[0]
    tf = w1_ref.shape[2]
    half = x_ref.shape[1]

    @pl.when(active & (f == 0))
    def _():
        hi, lo = _unpack_bf16_pairs(x_ref[...])
        xbf_ref[:, 0:half] = hi.astype(BF16)
        xbf_ref[:, half:2 * half] = lo.astype(BF16)

    def partial_sums(first, last):
        for r in range(tmb // rows_sub):
            rows = slice(r * rows_sub, (r + 1) * rows_sub)
            xr = xbf_ref[rows, :]
            acc = jnp.zeros((rows_sub, 2 * half), F32) if first else acc_ref[rows, :]
            for c in range(tf // cols_sub):
                cols = slice(c * cols_sub, (c + 1) * cols_sub)
                act = (_silu(_dot(xr, w1_ref[0, :, cols])) * _dot(xr, w3_ref[0, :, cols])).astype(BF16)
                acc = acc + _dot(act, w2_ref[0, cols, :])
            if last:
                o_ref[rows, :] = _pack_bf16_pairs(acc.astype(BF16).astype(F32))
            else:
                acc_ref[rows, :] = acc

    @pl.when(active & (f == 0))
    def _():
        partial_sums(True, False)

    if nf > 2:
        @pl.when(active & (f > 0) & (f < nf - 1))
        def _():
            partial_sums(False, False)

    @pl.when(active & (f == nf - 1))
    def _():
        partial_sums(False, True)

    @pl.when(jnp.logical_not(active) & (f == 0))
    def _():
        o_ref[...] = jnp.zeros_like(o_ref)


def moe_ffn(xb, block_e, n_used, w1, w3, w2, tmb=MOE_BLOCK, tf=MOE_FTILE):
    n_slots, half = xb.shape
    d = 2 * half
    nb = n_slots // tmb
    nf = w1.shape[2] // tf
    assert nf >= 2, "the first hidden-column step initialises the sum and the last one writes the block"

    def bclamp(b, nu):
        return jnp.minimum(b, nu[0] - 1)

    def fclamp(b, f, nu):
        return jnp.where(b < nu[0], f, nf - 1)

    grid_spec = pltpu.PrefetchScalarGridSpec(
        num_scalar_prefetch=2,
        grid=(nb, nf),
        in_specs=[pl.BlockSpec((tmb, half), lambda b, f, be, nu: (bclamp(b, nu), 0)),
                  pl.BlockSpec((1, d, tf), lambda b, f, be, nu: (be[bclamp(b, nu)], 0, fclamp(b, f, nu))),
                  pl.BlockSpec((1, d, tf), lambda b, f, be, nu: (be[bclamp(b, nu)], 0, fclamp(b, f, nu))),
                  pl.BlockSpec((1, tf, d), lambda b, f, be, nu: (be[bclamp(b, nu)], fclamp(b, f, nu), 0))],
        out_specs=pl.BlockSpec((tmb, half), lambda b, f, be, nu: (b, 0)),
        scratch_shapes=[pltpu.VMEM((tmb, d), BF16), pltpu.VMEM((tmb, d), F32)],
    )
    return pl.pallas_call(
        functools.partial(_moe_ffn_kernel, rows_sub=MOE_ROWS_SUB, cols_sub=MOE_COLS_SUB, nf=nf),
        grid_spec=grid_spec,
        out_shape=jax.ShapeDtypeStruct((n_slots, half), jnp.uint32),
        compiler_params=_cparams("arbitrary", "arbitrary"),
        name="moe_ffn",
    )(block_e, n_used, xb, w1, w3, w2)


def _finish_kernel(x_ref, route_ref, g_ref, y0_ref, y1_ref, o_ref):
    half = y0_ref.shape[1]
    route = route_ref[...]
    g0, g1 = route[:, 2:3], route[:, 3:4]
    hi0, lo0 = _unpack_bf16_pairs(y0_ref[...])
    hi1, lo1 = _unpack_bf16_pairs(y1_ref[...])
    ya = x_ref[:, 0:half] + g0 * hi0 + g1 * hi1
    yb = x_ref[:, half:2 * half] + g0 * lo0 + g1 * lo1
    ms = (jnp.sum(ya * ya, axis=-1, keepdims=True) + jnp.sum(yb * yb, axis=-1, keepdims=True)) / (2 * half)
    inv = lax.rsqrt(ms + EPS)
    o_ref[:, 0:half] = ya * inv * g_ref[:, 0:half]
    o_ref[:, half:2 * half] = yb * inv * g_ref[:, half:2 * half]


def moe_finish(x, route, y_rows, g, row0, tm=TOK_TILE):
    d = x.shape[1]
    m = y_rows.shape[0] // 2
    n_tiles = m // tm
    t0 = row0 // tm
    return pl.pallas_call(
        _finish_kernel,
        grid=(n_tiles,),
        in_specs=[pl.BlockSpec((tm, d), lambda i: (t0 + i, 0)),
                  pl.BlockSpec((tm, ROUTE_LANES), lambda i: (t0 + i, 0)),
                  pl.BlockSpec((1, d), lambda i: (0, 0)),
                  pl.BlockSpec((tm, d // 2), lambda i: (i, 0)),
                  pl.BlockSpec((tm, d // 2), lambda i: (i + n_tiles, 0))],
        out_specs=pl.BlockSpec((tm, d), lambda i: (i, 0)),
        out_shape=jax.ShapeDtypeStruct((m, d), F32),
        compiler_params=_cparams("parallel"),
        name="moe_finish",
    )(x, route, g.reshape(1, d), y_rows, y_rows)


def _tile_flags(seq_lens, tile):
    firsts, lasts = [], []
    for length in seq_lens:
        k = length // tile
        firsts += [1] + [0] * (k - 1)
        lasts += [0] * (k - 1) + [1]
    return np.asarray(firsts, np.int32), np.asarray(lasts, np.int32)


def _dft_table_kernel(cphi_ref, sphi_ref, cth_ref, sth_ref, c_ref, s_ref):
    cphi, sphi = cphi_ref[...], sphi_ref[...]
    cth, sth = cth_ref[0], sth_ref[0]
    c_ref[...] = (cphi * cth - sphi * sth).astype(BF16)
    s_ref[...] = (sphi * cth + cphi * sth).astype(BF16)


def _angles(rows, t):
    k = lax.broadcasted_iota(jnp.int32, (rows.shape[0], t), 1)
    ang = ((rows[:, None] * k) % t).astype(F32) * (2.0 * np.pi / t)
    return jnp.cos(ang), jnp.sin(ang)


def _dft_tables(t):
    rows = min(DFT_GEN_ROWS, t)
    nt = t // rows
    cphi, sphi = _angles(jnp.arange(rows, dtype=jnp.int32), t)
    cth, sth = _angles(jnp.arange(nt, dtype=jnp.int32) * rows, t)
    tile = pl.BlockSpec((rows, t), lambda i: (0, 0))
    vec = pl.BlockSpec((1, 1, t), lambda i: (i, 0, 0))
    out = pl.BlockSpec((rows, t), lambda i: (i, 0))
    return pl.pallas_call(
        _dft_table_kernel,
        grid=(nt,),
        in_specs=[tile, tile, vec, vec],
        out_specs=[out, out],
        out_shape=[jax.ShapeDtypeStruct((t, t), BF16)] * 2,
        compiler_params=_cparams("parallel"),
        name="dft_table_%d" % t,
    )(cphi, sphi, cth.reshape(nt, 1, t), sth.reshape(nt, 1, t))


def kernel(x_prompt, x_sample, norm_mix, norm_ffn, norm_final, w_in_ab, hgrn_lb_logits, hgrn_out_norm, sgu_norm,
           sgu_w, sgu_b, w_out_ab, w_in_cd, conv_w, conv_b, conv_ln_g, conv_ln_b, w_out_cd, ffn_w1, ffn_w3, ffn_w2,
           router_w, moe_w1, moe_w3, moe_w2):
    bp, tp, d = x_prompt.shape
    bs, ts_, _ = x_sample.shape
    n_p, n_s = bp * tp, bs * ts_
    n = n_p + n_s
    seq_lens = [tp] * bp + [ts_] * bs
    tile = TOK_TILE
    firsts, lasts = _tile_flags(seq_lens, tile)
    depth = norm_mix.shape[0]
    assert depth == 2, "the layer schedule below is the two-layer trunk"
    xa, xb_in = x_prompt.reshape(n_p, d), x_sample.reshape(n_s, d)
    moe_bf16 = tuple(w[0].astype(BF16) for w in (moe_w1, moe_w3, moe_w2))

    for layer in range(depth):
        j = layer // 2
        if layer % 2 == 0:
            z = norm_proj(xa, xb_in, norm_mix[layer], w_in_ab[j].astype(BF16), ready_first=moe_bf16)
            o_f = hgrn_scan(z, hgrn_lb_logits, jnp.asarray(firsts), reverse=False, layer=layer, f_col=1)
            o_b = hgrn_scan(z, hgrn_lb_logits, jnp.asarray(lasts[::-1].copy()), reverse=True, layer=layer, f_col=2)
            x = mix_ab(xa, xb_in, o_f, o_b, z, hgrn_out_norm[j], sgu_norm[j], sgu_w[j].astype(BF16), sgu_b[j].T,
                       w_out_ab[j].astype(BF16))
            x = ffn(x, norm_ffn[layer], ffn_w1[j].astype(BF16), ffn_w3[j].astype(BF16), ffn_w2[j].astype(BF16))
        else:
            ang = (lax.broadcasted_iota(jnp.int32, (HEAD, HEAD), 0) * lax.broadcasted_iota(jnp.int32, (HEAD, HEAD), 1)
                   % HEAD).astype(F32) * (2.0 * np.pi / HEAD)
            dft_c = jnp.concatenate([jnp.cos(ang), jnp.sin(ang)], axis=1).astype(BF16)
            zc, wv = norm_proj_cd(x, norm_mix[layer], w_in_cd[j].astype(BF16), dft_c)
            c_out = conv_module(zc, jnp.asarray(firsts), jnp.asarray(lasts), conv_w[j], conv_b[j], conv_ln_g[j],
                                conv_ln_b[j])
            cos_p, sin_p = _dft_tables(tp)
            cos_s, sin_s = _dft_tables(ts_)
            d_p = seq_dft(wv, cos_p, sin_p, 0, bp, tp)
            d_s = seq_dft(wv, cos_s, sin_s, n_p, bs, ts_)

            rw = jnp.zeros((d, ROUTE_LANES), F32).at[:, :N_EXPERTS].set(router_w[j])
            rw_hi = rw.astype(BF16)
            rw_lo = (rw - rw_hi.astype(F32)).astype(BF16)
            x3, h, route, counts = mix_cd_router(x, c_out, d_p, d_s, w_out_cd[j].astype(BF16), norm_ffn[layer],
                                                 rw_hi, rw_lo)

            tmb = MOE_BLOCK
            n_blocks = (2 * n + tmb - 1) // tmb + N_EXPERTS
            cnt = counts[0, :N_EXPERTS].astype(jnp.int32)
            padded = (cnt + tmb - 1) // tmb * tmb
            pend = jnp.cumsum(padded)
            pstart = pend - padded
            e_idx = route[:, 0:2].astype(jnp.int32)
            dest = pstart[e_idx] + route[:, 4:6].astype(jnp.int32)
            dest0, dest1 = dest[:, 0], dest[:, 1]
            block_start = jnp.arange(n_blocks, dtype=jnp.int32) * tmb
            block_e = jnp.minimum(jnp.sum((pend[None, :] <= block_start[:, None]).astype(jnp.int32), axis=1),
                                  N_EXPERTS - 1)
            n_used = (pend[-1] // tmb).astype(jnp.int32).reshape(1)
            n_slots = n_blocks * tmb
            jpad = jnp.arange(tmb, dtype=jnp.int32)[None, :]
            pad_slots = jnp.where(jpad < (padded - cnt)[:, None], (pstart + cnt)[:, None] + jpad,
                                  n_slots - 1).reshape(N_EXPERTS * tmb)

            xb = sc_dispatch(h, dest0, dest1, pad_slots, n_slots)
            yb = moe_ffn(xb, block_e, n_used, *moe_bf16)
            outs = []
            for row0, m in ((0, n_p), (n_p, n_s)):
                idx = jnp.concatenate([dest0[row0:row0 + m], dest1[row0:row0 + m]])
                outs.append(moe_finish(x3, route, sc_row_gather(yb, idx), norm_final, row0))
            out_p, out_s = outs

    return out_p.reshape(bp, tp, d), out_s.reshape(bs, ts_, d)
```

```python
import functools

import numpy as np
import jax
import jax.numpy as jnp
from jax import lax
from jax.experimental import pallas as pl
from jax.experimental.pallas import tpu as pltpu
from jax.experimental.pallas import tpu_sc as plsc

F32 = jnp.float32
BF16 = jnp.bfloat16
EPS = 1e-6

D_MODEL = 1024
HALF = 512
HEAD = 128
N_HEADS = 4
HGRN_CHUNK = 64
HGRN_SUB = 16
HGRN_CHUNKS_PER_ITER = 8
CONV_K = 31
CONV_PAD = 15
CONV_HALO = 16
N_EXPERTS = 8
ROUTE_LANES = 128
MAX_EXP = 80.0

VMEM_LIMIT_BYTES = 56 * 1024 * 1024

TOK_TILE = 512
WIDE_TILE = 1024
MOE_BLOCK = 1024
MOE_FTILE = 1792
MOE_ROWS_SUB = 512
MOE_COLS_SUB = 256
FFN_FTILE = 256
SC_CORES = 2
SC_SUBCORES = 16
SC_ROW_CHUNK = 128
SEQ_DFT_TILE_BYTES = 4 * 1024 * 1024
DFT_GEN_ROWS = 128


def _dot(a, b):
    return jnp.dot(a, b, preferred_element_type=F32)


def _dot_nt(a, b):
    return lax.dot_general(a, b, (((1,), (1,)), ((), ())), preferred_element_type=F32)


def _cparams(*sem):
    return pltpu.CompilerParams(dimension_semantics=sem, vmem_limit_bytes=VMEM_LIMIT_BYTES)


def _rms(x, g):
    ms = jnp.mean(x * x, axis=-1, keepdims=True)
    return x * lax.rsqrt(ms + EPS) * g


def _silu(x):
    return x * jax.nn.sigmoid(x)


def _pack_bf16_pairs(x):
    w = x.shape[1] // 2
    bits = lax.bitcast_convert_type(x, jnp.uint32)
    return (bits[:, :w] & jnp.uint32(0xFFFF0000)) | (bits[:, w:] >> 16)


def _unpack_bf16_pairs(words):
    hi = lax.bitcast_convert_type(words & jnp.uint32(0xFFFF0000), F32)
    lo = lax.bitcast_convert_type(words << 16, F32)
    return hi, lo


def _gelu_tanh(x):
    c = 0.7978845608028654
    return x * (0.5 + 0.5 * jnp.tanh(x * (c + (c * 0.044715) * (x * x))))


def _two_part_specs(tm, d, tiles_a):
    return (pl.BlockSpec((tm, d), lambda i: (jnp.minimum(i, tiles_a - 1), 0)),
            pl.BlockSpec((tm, d), lambda i: (jnp.maximum(i - tiles_a, 0), 0)))


def _two_part_read(xa_ref, xb_ref, tiles_a):
    return jnp.where(pl.program_id(0) < tiles_a, xa_ref[...], xb_ref[...])


def _norm_proj_kernel(xa_ref, xb_ref, g_ref, w_ref, *rest, tn, tiles_a):
    o_ref = rest[-1]
    h = _rms(_two_part_read(xa_ref, xb_ref, tiles_a), g_ref[...]).astype(BF16)
    for c in range(o_ref.shape[1] // tn):
        o_ref[:, c * tn:(c + 1) * tn] = _dot(h, w_ref[:, c * tn:(c + 1) * tn]).astype(o_ref.dtype)


def norm_proj(xa, xb, g, w_bf16, ready_first=(), tm=TOK_TILE, tn=512):
    d = xa.shape[1]
    n = xa.shape[0] + xb.shape[0]
    tiles_a = xa.shape[0] // tm
    nout = w_bf16.shape[1]
    return pl.pallas_call(
        functools.partial(_norm_proj_kernel, tn=tn, tiles_a=tiles_a),
        grid=(n // tm,),
        in_specs=[*_two_part_specs(tm, d, tiles_a),
                  pl.BlockSpec((1, d), lambda i: (0, 0)),
                  pl.BlockSpec((d, nout), lambda i: (0, 0)),
                  *[pl.BlockSpec(memory_space=pl.ANY) for _ in ready_first]],
        out_specs=pl.BlockSpec((tm, nout), lambda i: (i, 0)),
        out_shape=jax.ShapeDtypeStruct((n, nout), BF16),
        compiler_params=_cparams("parallel"),
        name="norm_proj_ab",
    )(xa, xb, g.reshape(1, d), w_bf16, *ready_first)


def _norm_proj_cd_kernel(x_ref, g_ref, w_ref, dft_ref, zc_ref, wv_ref):
    h = _rms(x_ref[...], g_ref[...]).astype(BF16)
    for c in range(2):
        zc_ref[:, c * HALF:(c + 1) * HALF] = _dot(h, w_ref[:, c * HALF:(c + 1) * HALF])
    d = _dot(h, w_ref[:, 2 * HALF:3 * HALF]).astype(BF16)
    for g in range(N_HEADS):
        r = _dot(d[:, g * HEAD:(g + 1) * HEAD], dft_ref[...])
        wv_ref[:, g * HEAD:(g + 1) * HEAD] = r[:, :HEAD].astype(BF16)
        wv_ref[:, HALF + g * HEAD:HALF + (g + 1) * HEAD] = r[:, HEAD:].astype(BF16)


def norm_proj_cd(x, g, w_bf16, dft_c, tm=TOK_TILE):
    n, d = x.shape
    return pl.pallas_call(
        _norm_proj_cd_kernel,
        grid=(n // tm,),
        in_specs=[pl.BlockSpec((tm, d), lambda i: (i, 0)),
                  pl.BlockSpec((1, d), lambda i: (0, 0)),
                  pl.BlockSpec((d, 3 * HALF), lambda i: (0, 0)),
                  pl.BlockSpec((HEAD, 2 * HEAD), lambda i: (0, 0))],
        out_specs=[pl.BlockSpec((tm, 2 * HALF), lambda i: (i, 0)),
                   pl.BlockSpec((tm, 2 * HALF), lambda i: (i, 0))],
        out_shape=[jax.ShapeDtypeStruct((n, 2 * HALF), F32),
                   jax.ShapeDtypeStruct((n, 2 * HALF), BF16)],
        compiler_params=_cparams("parallel"),
        name="norm_proj_cd",
    )(x, g.reshape(1, d), w_bf16, dft_c)


def _hgrn_kernel(reset_ref, q_ref, f_ref, v_ref, lbl_ref, o_ref, st0, st1, st2, st3, oi_ref, qe_ref, u_ref, d_ref,
                 *, reverse, layer, tt):
    c, sub = HGRN_CHUNK, HGRN_SUB
    n_sub = c // sub
    n_chunks = tt // c
    states = (st0, st1, st2, st3)

    @pl.when(reset_ref[pl.program_id(0)] == 1)
    def _():
        for st in states:
            st[...] = jnp.zeros_like(st)

    lg = lbl_ref[...]
    e = jnp.exp(lg - jnp.max(lg, axis=0, keepdims=True))
    p = e / jnp.sum(e, axis=0, keepdims=True)
    lb_all = p[0:1, :]
    for r in range(1, layer + 1):
        lb_all = lb_all + p[r:r + 1, :]

    row = lax.broadcasted_iota(jnp.int32, (c, c), 0)
    col = lax.broadcasted_iota(jnp.int32, (c, c), 1)
    tri = jnp.where((col >= row) if reverse else (col <= row), 1.0, 0.0).astype(BF16)
    spans, keeps = [], []
    for i in range(n_sub):
        span = slice(i * sub, c) if reverse else slice(0, (i + 1) * sub)
        n_span = span.stop - span.start
        tr = lax.broadcasted_iota(jnp.int32, (sub, n_span), 0)
        sc = lax.broadcasted_iota(jnp.int32, (sub, n_span), 1)
        spans.append(span)
        keeps.append((sc >= tr) if reverse else (sc <= tr + i * sub))

    units = [(cj, h) for cj in range(HGRN_CHUNKS_PER_ITER) for h in range(N_HEADS)]

    def local_body(it, carry):
        ci = [it * HGRN_CHUNKS_PER_ITER + cj for cj in range(HGRN_CHUNKS_PER_ITER)]
        r0 = [pl.multiple_of(x * c, c) for x in ci]
        vals = {}
        for cj, h in units:
            cols = slice(h * HEAD, (h + 1) * HEAD)
            lb = lb_all[:, cols]
            q = _silu(q_ref[pl.ds(r0[cj], c), cols].astype(F32))
            v = v_ref[pl.ds(r0[cj], c), cols].astype(F32)
            f = lb + (1.0 - lb) * jax.nn.sigmoid(f_ref[pl.ds(r0[cj], c), cols].astype(F32))
            lf = jnp.log(f)
            lf_hi = lf.astype(BF16)
            lf_lo = (lf - lf_hi.astype(F32)).astype(BF16)
            vals[cj, h] = (q, v, 1.0 - f, _dot(tri, lf_hi) + _dot(tri, lf_lo))
        scores = {}
        for cj, h in units:
            cols = slice(h * HEAD, (h + 1) * HEAD)
            q, v, k, b = vals[cj, h]
            qe_ref[pl.ds(r0[cj], c), cols] = (q * jnp.exp(b)).astype(BF16)
            edge = b[0:1, :] if reverse else b[c - 1:c, :]
            ks = (k * jnp.exp(edge - b)).astype(BF16)
            u_ref[ci[cj] * N_HEADS + h] = _dot(v.T.astype(BF16), ks)
            d_ref[ci[cj] * N_HEADS + h] = jnp.broadcast_to(jnp.exp(edge), (8, HEAD))
            for i in range(n_sub):
                rows = slice(i * sub, (i + 1) * sub)
                if reverse:
                    anchor = b[(i + 1) * sub:(i + 1) * sub + 1, :] if i + 1 < n_sub else jnp.zeros((1, HEAD), F32)
                else:
                    anchor = b[i * sub - 1:i * sub, :] if i > 0 else jnp.zeros((1, HEAD), F32)
                qi = (q[rows] * jnp.exp(b[rows] - anchor)).astype(BF16)
                ki = (k[spans[i]] * jnp.exp(jnp.minimum(anchor - b[spans[i]], MAX_EXP))).astype(BF16)
                scores[cj, h, i] = _dot_nt(qi, ki)
        for cj, h in units:
            cols = slice(h * HEAD, (h + 1) * HEAD)
            vb = vals[cj, h][1].astype(BF16)
            for i in range(n_sub):
                a = jnp.where(keeps[i], scores[cj, h, i], 0.0).astype(BF16)
                oi_ref[pl.ds(r0[cj] + i * sub, sub), cols] = _dot(a, vb[spans[i]])
        return carry

    lax.fori_loop(0, n_chunks // HGRN_CHUNKS_PER_ITER, local_body, 0)

    cur = [st[...] for st in states]
    for ci in range(n_chunks):
        cc = (n_chunks - 1 - ci) if reverse else ci
        for h in range(N_HEADS):
            cols = slice(h * HEAD, (h + 1) * HEAD)
            o_ref[cc * c:(cc + 1) * c, cols] = (
                oi_ref[cc * c:(cc + 1) * c, cols]
                + _dot_nt(qe_ref[cc * c:(cc + 1) * c, cols], cur[h].astype(BF16))).astype(o_ref.dtype)
        cur = [cur[h] * d_ref[cc * N_HEADS + h][0:1, :] + u_ref[cc * N_HEADS + h] for h in range(N_HEADS)]
    for h in range(N_HEADS):
        states[h][...] = cur[h]


def hgrn_scan(z, lb_logits, resets, *, reverse, layer, f_col, tt=TOK_TILE):
    n = z.shape[0]
    nt = n // tt
    n_units = tt // HGRN_CHUNK * N_HEADS
    order = (lambda i, r: (nt - 1 - i, 0)) if reverse else (lambda i, r: (i, 0))
    blk = lambda cb: (lambda i, r: (order(i, r)[0], cb))
    grid_spec = pltpu.PrefetchScalarGridSpec(
        num_scalar_prefetch=1,
        grid=(nt,),
        in_specs=[pl.BlockSpec((tt, HALF), blk(0)),
                  pl.BlockSpec((tt, HALF), blk(f_col)),
                  pl.BlockSpec((tt, HALF), blk(3)),
                  pl.BlockSpec(lb_logits.shape, lambda i, r: (0, 0))],
        out_specs=pl.BlockSpec((tt, HALF), order),
        scratch_shapes=[pltpu.VMEM((HEAD, HEAD), F32)] * N_HEADS + [
            pltpu.VMEM((tt, HALF), F32),
            pltpu.VMEM((tt, HALF), BF16),
            pltpu.VMEM((n_units, HEAD, HEAD), F32),
            pltpu.VMEM((n_units, 8, HEAD), F32)],
    )
    return pl.pallas_call(
        functools.partial(_hgrn_kernel, reverse=reverse, layer=layer, tt=tt),
        grid_spec=grid_spec,
        out_shape=jax.ShapeDtypeStruct((n, HALF), BF16),
        compiler_params=_cparams("arbitrary"),
        name="hgrn_bwd" if reverse else "hgrn_fwd",
    )(resets, z, z, z, lb_logits)


def _mix_ab_kernel(xa_ref, xb_ref, of_ref, ob_ref, g_ref, u_ref, v_ref, onorm_ref, snorm_ref, wsp_ref, bsp_ref,
                   wout_ref, o_ref, *, tm, tiles_a):
    o = of_ref[...].astype(F32) + ob_ref[...].astype(F32)
    g = g_ref[...].astype(F32)
    parts = []
    for h in range(N_HEADS):
        cols = slice(h * HEAD, (h + 1) * HEAD)
        parts.append(_rms(o[:, cols], onorm_ref[:, cols]) * _silu(g[:, cols]))
    a = jnp.concatenate(parts, axis=1).astype(BF16)
    acc = _two_part_read(xa_ref, xb_ref, tiles_a) + _dot(a, wout_ref[0:HALF, :])

    u = _gelu_tanh(u_ref[...].astype(F32))
    vb = _rms(_gelu_tanh(v_ref[...].astype(F32)), snorm_ref[...]).astype(BF16)
    rows = []
    for c in range(tm // HEAD):
        mixed = []
        for gi in range(N_HEADS):
            vg = vb[c * HEAD:(c + 1) * HEAD, gi * HEAD:(gi + 1) * HEAD]
            mixed.append(_dot(wsp_ref[gi], vg) + bsp_ref[:, gi:gi + 1])
        rows.append(jnp.concatenate(mixed, axis=1))
    b = (u * jnp.concatenate(rows, axis=0)).astype(BF16)
    o_ref[...] = acc + _dot(b, wout_ref[HALF:2 * HALF, :])


def mix_ab(xa, xb, o_f, o_b, z, onorm, snorm, wsp_bf16, bsp_t, wout_bf16, tm=TOK_TILE):
    d = xa.shape[1]
    n = xa.shape[0] + xb.shape[0]
    tiles_a = xa.shape[0] // tm
    zblk = lambda cb: pl.BlockSpec((tm, HALF), lambda i: (i, cb))
    const = lambda shape: pl.BlockSpec(shape, lambda i: (0,) * len(shape))
    return pl.pallas_call(
        functools.partial(_mix_ab_kernel, tm=tm, tiles_a=tiles_a),
        grid=(n // tm,),
        in_specs=[*_two_part_specs(tm, d, tiles_a),
                  pl.BlockSpec((tm, HALF), lambda i: (i, 0)),
                  pl.BlockSpec((tm, HALF), lambda i: (i, 0)),
                  zblk(4), zblk(5), zblk(6),
                  const((1, HALF)), const((1, HALF)),
                  const((N_HEADS, HEAD, HEAD)), const((HEAD, N_HEADS)),
                  const((d, d))],
        out_specs=pl.BlockSpec((tm, d), lambda i: (i, 0)),
        out_shape=jax.ShapeDtypeStruct((n, d), F32),
        compiler_params=_cparams("parallel"),
        name="mix_ab",
    )(xa, xb, o_f, o_b, z, z, z, onorm.reshape(1, HALF), snorm.reshape(1, HALF), wsp_bf16, bsp_t, wout_bf16)


def _ffn_kernel(x_ref, g_ref, w1_ref, w3_ref, w2_ref, o_ref, *, tf):
    x = x_ref[...]
    h = _rms(x, g_ref[...]).astype(BF16)
    acc = x
    for c in range(w1_ref.shape[1] // tf):
        cols = slice(c * tf, (c + 1) * tf)
        act = (_silu(_dot(h, w1_ref[:, cols])) * _dot(h, w3_ref[:, cols])).astype(BF16)
        acc = acc + _dot(act, w2_ref[cols, :])
    o_ref[...] = acc


def ffn(x, g, w1, w3, w2, tm=TOK_TILE, tf=FFN_FTILE):
    n, d = x.shape
    f = w1.shape[1]
    const = lambda shape: pl.BlockSpec(shape, lambda i: (0,) * len(shape))
    return pl.pallas_call(
        functools.partial(_ffn_kernel, tf=tf),
        grid=(n // tm,),
        in_specs=[pl.BlockSpec((tm, d), lambda i: (i, 0)), const((1, d)),
                  const((d, f)), const((d, f)), const((f, d))],
        out_specs=pl.BlockSpec((tm, d), lambda i: (i, 0)),
        out_shape=jax.ShapeDtypeStruct((n, d), F32),
        compiler_params=_cparams("parallel"),
        name="ffn_dense",
    )(x, g.reshape(1, d), w1, w3, w2)


def _conv_kernel(first_ref, last_ref, a_ref, gt_ref, ap_ref, gp_ref, an_ref, gn_ref, w_ref, b_ref, lng_ref,
                 lnb_ref, o_ref, ypad_ref, *, ts, rb):
    i = pl.program_id(0)
    halo = CONV_HALO
    keep_prev = jnp.where(first_ref[i] == 1, 0.0, 1.0)
    keep_next = jnp.where(last_ref[i] == 1, 0.0, 1.0)
    ypad_ref[0:halo, :] = ap_ref[...] * jax.nn.sigmoid(gp_ref[...]) * keep_prev
    ypad_ref[halo:halo + ts, :] = a_ref[...] * jax.nn.sigmoid(gt_ref[...])
    ypad_ref[halo + ts:2 * halo + ts, :] = an_ref[...] * jax.nn.sigmoid(gn_ref[...]) * keep_next

    n_win = rb + 2 * halo

    def body(bi, carry):
        r0 = pl.multiple_of(bi * rb, rb)
        strips = []
        for s in range(HALF // HEAD):
            cols = slice(s * HEAD, (s + 1) * HEAD)
            win = ypad_ref[pl.ds(r0, n_win), cols]
            acc = jnp.zeros((rb, HEAD), F32) + b_ref[:, cols]
            for res in range(8):
                sh = win if res == 0 else pltpu.roll(win, n_win - res, axis=0)
                for j in range(CONV_K):
                    off = halo - CONV_PAD + j
                    if off % 8 == res:
                        base = off - res
                        acc = acc + w_ref[j:j + 1, cols] * sh[base:base + rb, :]
            strips.append(acc)
        acc = jnp.concatenate(strips, axis=1)
        mu = jnp.mean(acc, axis=-1, keepdims=True)
        xc = acc - mu
        var = jnp.mean(xc * xc, axis=-1, keepdims=True)
        y = xc * lax.rsqrt(var + EPS) * lng_ref[...] + lnb_ref[...]
        o_ref[pl.ds(r0, rb), :] = _silu(y).astype(o_ref.dtype)
        return carry

    lax.fori_loop(0, ts // rb, body, 0)


def conv_module(zc, firsts, lasts, w, b, ln_g, ln_b, ts=TOK_TILE, rb=128):
    n = zc.shape[0]
    nt = n // ts
    hb = ts // CONV_HALO
    n_hb = n // CONV_HALO
    main = lambda cb: pl.BlockSpec((ts, HALF), lambda i, f, l: (i, cb))
    prev = lambda cb: pl.BlockSpec((CONV_HALO, HALF), lambda i, f, l: (jnp.maximum(i * hb - 1, 0), cb))
    nxt = lambda cb: pl.BlockSpec((CONV_HALO, HALF), lambda i, f, l: (jnp.minimum((i + 1) * hb, n_hb - 1), cb))
    const = lambda shape: pl.BlockSpec(shape, lambda i, f, l: (0,) * len(shape))
    grid_spec = pltpu.PrefetchScalarGridSpec(
        num_scalar_prefetch=2,
        grid=(nt,),
        in_specs=[main(0), main(1), prev(0), prev(1), nxt(0), nxt(1),
                  const((CONV_K, HALF)), const((1, HALF)), const((1, HALF)), const((1, HALF))],
        out_specs=pl.BlockSpec((ts, HALF), lambda i, f, l: (i, 0)),
        scratch_shapes=[pltpu.VMEM((ts + 2 * CONV_HALO, HALF), F32)],
    )
    return pl.pallas_call(
        functools.partial(_conv_kernel, ts=ts, rb=rb),
        grid_spec=grid_spec,
        out_shape=jax.ShapeDtypeStruct((n, HALF), BF16),
        compiler_params=_cparams("parallel"),
        name="conv_module",
    )(firsts, lasts, zc, zc, zc, zc, zc, zc, w, b.reshape(1, HALF), ln_g.reshape(1, HALF), ln_b.reshape(1, HALF))


def _seq_dft_kernel(c_ref, s_ref, wv_ref, o_ref, *, scale):
    acc = _dot(c_ref[...], wv_ref[:, 0:HALF]) - _dot(s_ref[...], wv_ref[:, HALF:2 * HALF])
    o_ref[...] = (acc * scale).astype(o_ref.dtype)


def seq_dft(wv, cos_t, sin_t, row0, n_seq, seq_len):
    tmo = min(seq_len, max(256, SEQ_DFT_TILE_BYTES // (2 * seq_len)))
    nt = seq_len // tmo
    b0 = row0 // seq_len
    scale = 1.0 / float(np.sqrt(seq_len * HEAD))
    return pl.pallas_call(
        functools.partial(_seq_dft_kernel, scale=scale),
        grid=(n_seq, nt),
        in_specs=[pl.BlockSpec((tmo, seq_len), lambda b, i: (i, 0)),
                  pl.BlockSpec((tmo, seq_len), lambda b, i: (i, 0)),
                  pl.BlockSpec((seq_len, 2 * HALF), lambda b, i: (b0 + b, 0))],
        out_specs=pl.BlockSpec((tmo, HALF), lambda b, i: (b * nt + i, 0)),
        out_shape=jax.ShapeDtypeStruct((n_seq * seq_len, HALF), BF16),
        compiler_params=_cparams("parallel", "parallel"),
        name="seq_dft_%d" % seq_len,
    )(cos_t, sin_t, wv)


def _mix_cd_router_kernel(x_ref, c_ref, da_ref, db_ref, wout_ref, g_ref, rwh_ref, rwl_ref, strict_ref, x3_ref, h_ref,
                          route_ref, cnt_ref, carry_ref, *, tm, tiles_a):
    @pl.when(pl.program_id(0) == 0)
    def _():
        carry_ref[...] = jnp.zeros_like(carry_ref)

    d_mix = _two_part_read(da_ref, db_ref, tiles_a)
    n_parts = 2
    rp = tm // n_parts
    parts = []
    for p in range(n_parts):
        rows = slice(p * rp, (p + 1) * rp)
        x3 = (x_ref[rows, :] + _dot(c_ref[rows, :], wout_ref[0:HALF, :])
              + _dot(d_mix[rows, :], wout_ref[HALF:2 * HALF, :]))
        x3_ref[rows, :] = x3
        parts.append(_rms(x3, g_ref[...]))
    logit_parts = []
    for p in range(n_parts):
        rows = slice(p * rp, (p + 1) * rp)
        h = parts[p]
        h_hi = h.astype(BF16)
        h_lo = (h - h_hi.astype(F32)).astype(BF16)
        h_ref[rows, :] = _pack_bf16_pairs(h_hi.astype(F32))
        logit_parts.append(_dot(h_hi, rwh_ref[...]) + (_dot(h_lo, rwh_ref[...]) + _dot(h_hi, rwl_ref[...])))
    logits = jnp.concatenate(logit_parts, axis=0)
    lane = lax.broadcasted_iota(jnp.int32, (tm, ROUTE_LANES), 1)
    neg = jnp.float32(-jnp.inf)
    logits = jnp.where(lane < N_EXPERTS, logits, neg)
    m1 = jnp.max(logits, axis=-1, keepdims=True)
    i1 = jnp.min(jnp.where(logits == m1, lane, ROUTE_LANES), axis=-1, keepdims=True)
    rest = jnp.where(lane == i1, neg, logits)
    m2 = jnp.max(rest, axis=-1, keepdims=True)
    i2 = jnp.min(jnp.where(rest == m2, lane, ROUTE_LANES), axis=-1, keepdims=True)
    e2 = jnp.exp(m2 - m1)
    g1 = 1.0 / (1.0 + e2)
    g2 = e2 / (1.0 + e2)

    onehot = jnp.where((lane == i1) | (lane == i2), 1.0, 0.0)
    before = _dot(strict_ref[...], onehot.astype(BF16)) + carry_ref[0:1, :]
    r1 = jnp.sum(jnp.where(lane == i1, before, 0.0), axis=-1, keepdims=True)
    r2 = jnp.sum(jnp.where(lane == i2, before, 0.0), axis=-1, keepdims=True)
    total = carry_ref[0:1, :] + jnp.sum(onehot, axis=0, keepdims=True)
    carry_ref[...] = jnp.broadcast_to(total, carry_ref.shape)
    cnt_ref[...] = jnp.broadcast_to(total, cnt_ref.shape)

    route = jnp.where(lane == 0, i1.astype(F32), 0.0)
    route = jnp.where(lane == 1, i2.astype(F32), route)
    route = jnp.where(lane == 2, g1, route)
    route = jnp.where(lane == 3, g2, route)
    route = jnp.where(lane == 4, r1, route)
    route = jnp.where(lane == 5, r2, route)
    route_ref[...] = route


def mix_cd_router(x, c_out, d_a, d_b, wout_bf16, g, rw_hi, rw_lo, tm=TOK_TILE):
    n, d = x.shape
    tiles_a = d_a.shape[0] // tm
    const = lambda shape: pl.BlockSpec(shape, lambda i: (0,) * len(shape))
    tok = lambda w: pl.BlockSpec((tm, w), lambda i: (i, 0))
    strict = jnp.tril(jnp.ones((tm, tm), BF16), -1)
    return pl.pallas_call(
        functools.partial(_mix_cd_router_kernel, tm=tm, tiles_a=tiles_a),
        grid=(n // tm,),
        in_specs=[tok(d), tok(HALF), *_two_part_specs(tm, HALF, tiles_a), const((d, d)), const((1, d)),
                  const((d, ROUTE_LANES)), const((d, ROUTE_LANES)), const((tm, tm))],
        out_specs=[tok(d), tok(d // 2), tok(ROUTE_LANES), const((8, ROUTE_LANES))],
        out_shape=[jax.ShapeDtypeStruct((n, d), F32), jax.ShapeDtypeStruct((n, d // 2), jnp.uint32),
                   jax.ShapeDtypeStruct((n, ROUTE_LANES), F32), jax.ShapeDtypeStruct((8, ROUTE_LANES), F32)],
        scratch_shapes=[pltpu.VMEM((8, ROUTE_LANES), F32)],
        compiler_params=_cparams("arbitrary"),
        name="mix_cd_router",
    )(x, c_out, d_a, d_b, wout_bf16, g.reshape(1, d), rw_hi, rw_lo, strict)


_SC_WORKERS = SC_CORES * SC_SUBCORES


def _sc_mesh():
    return plsc.VectorSubcoreMesh(core_axis_name="c", subcore_axis_name="s")


def _sc_worker_id():
    return lax.axis_index("s") * SC_CORES + lax.axis_index("c")


def sc_dispatch(rows, dest0, dest1, pad_slots, n_slots, chunk=SC_ROW_CHUNK):
    n, w = rows.shape
    n_pad = pad_slots.shape[0]
    per_worker = n // _SC_WORKERS
    pad_per_worker = n_pad // _SC_WORKERS
    assert per_worker * _SC_WORKERS == n and per_worker % chunk == 0 and chunk % 8 == 0
    assert pad_per_worker * _SC_WORKERS == n_pad and pad_per_worker % chunk == 0

    @functools.partial(
        pl.kernel, mesh=_sc_mesh(),
        out_type=jax.ShapeDtypeStruct((n_slots, w), rows.dtype),
        scratch_types=[pltpu.VMEM((chunk,), jnp.int32), pltpu.VMEM((chunk, w), rows.dtype),
                       pltpu.SemaphoreType.DMA],
    )
    def scatter(rows_hbm, d0_hbm, d1_hbm, pad_hbm, zeros_hbm, out_hbm, idx_v, rows_v, sem):
        worker = _sc_worker_id()

        @pl.loop(0, per_worker // chunk)
        def _(c):
            off = worker * per_worker + c * chunk
            pltpu.sync_copy(rows_hbm.at[pl.ds(off, chunk)], rows_v)
            for d_hbm in (d0_hbm, d1_hbm):
                pltpu.sync_copy(d_hbm.at[pl.ds(off, chunk)], idx_v)
                pltpu.async_copy(rows_v, out_hbm.at[idx_v], sem).wait()

        pltpu.sync_copy(zeros_hbm, rows_v)

        @pl.loop(0, pad_per_worker // chunk)
        def _(c):
            off = worker * pad_per_worker + c * chunk
            pltpu.sync_copy(pad_hbm.at[pl.ds(off, chunk)], idx_v)
            pltpu.async_copy(rows_v, out_hbm.at[idx_v], sem).wait()

    return scatter(rows, dest0, dest1, pad_slots, jnp.zeros((chunk, w), rows.dtype))


def sc_row_gather(table, idx, chunk=SC_ROW_CHUNK):
    b, w = idx.shape[0], table.shape[1]
    per_worker = b // _SC_WORKERS
    assert per_worker * _SC_WORKERS == b and per_worker % chunk == 0 and chunk % 8 == 0

    @functools.partial(
        pl.kernel, mesh=_sc_mesh(),
        out_type=jax.ShapeDtypeStruct((b, w), table.dtype),
        scratch_types=[pltpu.VMEM((chunk,), jnp.int32), pltpu.VMEM((chunk, w), table.dtype),
                       pltpu.SemaphoreType.DMA],
    )
    def gather(table_hbm, idx_hbm, out_hbm, idx_v, rows_v, sem):
        worker = _sc_worker_id()

        @pl.loop(0, per_worker // chunk)
        def _(c):
            off = worker * per_worker + c * chunk
            pltpu.sync_copy(idx_hbm.at[pl.ds(off, chunk)], idx_v)
            pltpu.async_copy(table_hbm.at[idx_v], rows_v, sem).wait()
            pltpu.sync_copy(rows_v, out_hbm.at[pl.ds(off, chunk)])

    return gather(table, idx)


def _moe_ffn_kernel(be_ref, nu_ref, x_ref, w1_ref, w3_ref, w2_ref, o_ref, xbf_ref, acc_ref, *, rows_sub, cols_sub, nf):
    del be_ref
    b = pl.program_id(0)
    f = pl.program_id(1)
    active = b < nu_ref[0]
    tmb = x_ref.shape[0]
    tf = w1_ref.shape[2]
    half = x_ref.shape[1]

    @pl.when(active & (f == 0))
    def _():
        hi, lo = _unpack_bf16_pairs(x_ref[...])
        xbf_ref[:, 0:half] = hi.astype(BF16)
        xbf_ref[:, half:2 * half] = lo.astype(BF16)

    def partial_sums(first, last):
        for r in range(tmb // rows_sub):
            rows = slice(r * rows_sub, (r + 1) * rows_sub)
            xr = xbf_ref[rows, :]
            acc = jnp.zeros((rows_sub, 2 * half), F32) if first else acc_ref[rows, :]
            for c in range(tf // cols_sub):
                cols = slice(c * cols_sub, (c + 1) * cols_sub)
                act = (_silu(_dot(xr, w1_ref[0, :, cols])) * _dot(xr, w3_ref[0, :, cols])).astype(BF16)
                acc = acc + _dot(act, w2_ref[0, cols, :])
            if last:
                o_ref[rows, :] = _pack_bf16_pairs(acc.astype(BF16).astype(F32))
            else:
                acc_ref[rows, :] = acc

    @pl.when(active & (f == 0))
    def _():
        partial_sums(True, False)

    if nf > 2:
        @pl.when(active & (f > 0) & (f < nf - 1))
        def _():
            partial_sums(False, False)

    @pl.when(active & (f == nf - 1))
    def _():
        partial_sums(False, True)

    @pl.when(jnp.logical_not(active) & (f == 0))
    def _():
        o_ref[...] = jnp.zeros_like(o_ref)


def moe_ffn(xb, block_e, n_used, w1, w3, w2, tmb=MOE_BLOCK, tf=MOE_FTILE):
    n_slots, half = xb.shape
    d = 2 * half
    nb = n_slots // tmb
    nf = w1.shape[2] // tf
    assert nf >= 2, "the first hidden-column step initialises the sum and the last one writes the block"

    def bclamp(b, nu):
        return jnp.minimum(b, nu[0] - 1)

    def fclamp(b, f, nu):
        return jnp.where(b < nu[0], f, nf - 1)

    grid_spec = pltpu.PrefetchScalarGridSpec(
        num_scalar_prefetch=2,
        grid=(nb, nf),
        in_specs=[pl.BlockSpec((tmb, half), lambda b, f, be, nu: (bclamp(b, nu), 0)),
                  pl.BlockSpec((1, d, tf), lambda b, f, be, nu: (be[bclamp(b, nu)], 0, fclamp(b, f, nu))),
                  pl.BlockSpec((1, d, tf), lambda b, f, be, nu: (be[bclamp(b, nu)], 0, fclamp(b, f, nu))),
                  pl.BlockSpec((1, tf, d), lambda b, f, be, nu: (be[bclamp(b, nu)], fclamp(b, f, nu), 0))],
        out_specs=pl.BlockSpec((tmb, half), lambda b, f, be, nu: (b, 0)),
        scratch_shapes=[pltpu.VMEM((tmb, d), BF16), pltpu.VMEM((tmb, d), F32)],
    )
    return pl.pallas_call(
        functools.partial(_moe_ffn_kernel, rows_sub=MOE_ROWS_SUB, cols_sub=MOE_COLS_SUB, nf=nf),
        grid_spec=grid_spec,
        out_shape=jax.ShapeDtypeStruct((n_slots, half), jnp.uint32),
        compiler_params=_cparams("arbitrary", "arbitrary"),
        name="moe_ffn",
    )(block_e, n_used, xb, w1, w3, w2)


def _finish_kernel(x_ref, route_ref, g_ref, y0_ref, y1_ref, o_ref):
    half = y0_ref.shape[1]
    route = route_ref[...]
    g0, g1 = route[:, 2:3], route[:, 3:4]
    hi0, lo0 = _unpack_bf16_pairs(y0_ref[...])
    hi1, lo1 = _unpack_bf16_pairs(y1_ref[...])
    ya = x_ref[:, 0:half] + g0 * hi0 + g1 * hi1
    yb = x_ref[:, half:2 * half] + g0 * lo0 + g1 * lo1
    ms = (jnp.sum(ya * ya, axis=-1, keepdims=True) + jnp.sum(yb * yb, axis=-1, keepdims=True)) / (2 * half)
    inv = lax.rsqrt(ms + EPS)
    o_ref[:, 0:half] = ya * inv * g_ref[:, 0:half]
    o_ref[:, half:2 * half] = yb * inv * g_ref[:, half:2 * half]


def moe_finish(x, route, y_rows, g, row0, tm=TOK_TILE):
    d = x.shape[1]
    m = y_rows.shape[0] // 2
    n_tiles = m // tm
    t0 = row0 // tm
    return pl.pallas_call(
        _finish_kernel,
        grid=(n_tiles,),
        in_specs=[pl.BlockSpec((tm, d), lambda i: (t0 + i, 0)),
                  pl.BlockSpec((tm, ROUTE_LANES), lambda i: (t0 + i, 0)),
                  pl.BlockSpec((1, d), lambda i: (0, 0)),
                  pl.BlockSpec((tm, d // 2), lambda i: (i, 0)),
                  pl.BlockSpec((tm, d // 2), lambda i: (i + n_tiles, 0))],
        out_specs=pl.BlockSpec((tm, d), lambda i: (i, 0)),
        out_shape=jax.ShapeDtypeStruct((m, d), F32),
        compiler_params=_cparams("parallel"),
        name="moe_finish",
    )(x, route, g.reshape(1, d), y_rows, y_rows)


def _tile_flags(seq_lens, tile):
    firsts, lasts = [], []
    for length in seq_lens:
        k = length // tile
        firsts += [1] + [0] * (k - 1)
        lasts += [0] * (k - 1) + [1]
    return np.asarray(firsts, np.int32), np.asarray(lasts, np.int32)


def _dft_table_kernel(cphi_ref, sphi_ref, cth_ref, sth_ref, c_ref, s_ref):
    cphi, sphi = cphi_ref[...], sphi_ref[...]
    cth, sth = cth_ref[0], sth_ref[0]
    c_ref[...] = (cphi * cth - sphi * sth).astype(BF16)
    s_ref[...] = (sphi * cth + cphi * sth).astype(BF16)


def _angles(rows, t):
    k = lax.broadcasted_iota(jnp.int32, (rows.shape[0], t), 1)
    ang = ((rows[:, None] * k) % t).astype(F32) * (2.0 * np.pi / t)
    return jnp.cos(ang), jnp.sin(ang)


def _dft_tables(t):
    rows = min(DFT_GEN_ROWS, t)
    nt = t // rows
    cphi, sphi = _angles(jnp.arange(rows, dtype=jnp.int32), t)
    cth, sth = _angles(jnp.arange(nt, dtype=jnp.int32) * rows, t)
    tile = pl.BlockSpec((rows, t), lambda i: (0, 0))
    vec = pl.BlockSpec((1, 1, t), lambda i: (i, 0, 0))
    out = pl.BlockSpec((rows, t), lambda i: (i, 0))
    return pl.pallas_call(
        _dft_table_kernel,
        grid=(nt,),
        in_specs=[tile, tile, vec, vec],
        out_specs=[out, out],
        out_shape=[jax.ShapeDtypeStruct((t, t), BF16)] * 2,
        compiler_params=_cparams("parallel"),
        name="dft_table_%d" % t,
    )(cphi, sphi, cth.reshape(nt, 1, t), sth.reshape(nt, 1, t))


def kernel(x_prompt, x_sample, norm_mix, norm_ffn, norm_final, w_in_ab, hgrn_lb_logits, hgrn_out_norm, sgu_norm,
           sgu_w, sgu_b, w_out_ab, w_in_cd, conv_w, conv_b, conv_ln_g, conv_ln_b, w_out_cd, ffn_w1, ffn_w3, ffn_w2,
           router_w, moe_w1, moe_w3, moe_w2):
    bp, tp, d = x_prompt.shape
    bs, ts_, _ = x_sample.shape
    n_p, n_s = bp * tp, bs * ts_
    n = n_p + n_s
    seq_lens = [tp] * bp + [ts_] * bs
    tile = TOK_TILE
    firsts, lasts = _tile_flags(seq_lens, tile)
    depth = norm_mix.shape[0]
    assert depth == 2, "the layer schedule below is the two-layer trunk"
    xa, xb_in = x_prompt.reshape(n_p, d), x_sample.reshape(n_s, d)
    moe_bf16 = tuple(w[0].astype(BF16) for w in (moe_w1, moe_w3, moe_w2))

    for layer in range(depth):
        j = layer // 2
        if layer % 2 == 0:
            z = norm_proj(xa, xb_in, norm_mix[layer], w_in_ab[j].astype(BF16), ready_first=moe_bf16, tm=WIDE_TILE)
            o_f = hgrn_scan(z, hgrn_lb_logits, jnp.asarray(firsts), reverse=False, layer=layer, f_col=1)
            o_b = hgrn_scan(z, hgrn_lb_logits, jnp.asarray(lasts[::-1].copy()), reverse=True, layer=layer, f_col=2)
            x = mix_ab(xa, xb_in, o_f, o_b, z, hgrn_out_norm[j], sgu_norm[j], sgu_w[j].astype(BF16), sgu_b[j].T,
                       w_out_ab[j].astype(BF16))
            x = ffn(x, norm_ffn[layer], ffn_w1[j].astype(BF16), ffn_w3[j].astype(BF16), ffn_w2[j].astype(BF16))
        else:
            ang = (lax.broadcasted_iota(jnp.int32, (HEAD, HEAD), 0) * lax.broadcasted_iota(jnp.int32, (HEAD, HEAD), 1)
                   % HEAD).astype(F32) * (2.0 * np.pi / HEAD)
            dft_c = jnp.concatenate([jnp.cos(ang), jnp.sin(ang)], axis=1).astype(BF16)
            zc, wv = norm_proj_cd(x, norm_mix[layer], w_in_cd[j].astype(BF16), dft_c, tm=WIDE_TILE)
            c_out = conv_module(zc, jnp.asarray(firsts), jnp.asarray(lasts), conv_w[j], conv_b[j], conv_ln_g[j],
                                conv_ln_b[j])
            cos_p, sin_p = _dft_tables(tp)
            cos_s, sin_s = _dft_tables(ts_)
            d_p = seq_dft(wv, cos_p, sin_p, 0, bp, tp)
            d_s = seq_dft(wv, cos_s, sin_s, n_p, bs, ts_)

            rw = jnp.zeros((d, ROUTE_LANES), F32).at[:, :N_EXPERTS].set(router_w[j])
            rw_hi = rw.astype(BF16)
            rw_lo = (rw - rw_hi.astype(F32)).astype(BF16)
            x3, h, route, counts = mix_cd_router(x, c_out, d_p, d_s, w_out_cd[j].astype(BF16), norm_ffn[layer],
                                                 rw_hi, rw_lo)

            tmb = MOE_BLOCK
            n_blocks = (2 * n + tmb - 1) // tmb + N_EXPERTS
            cnt = counts[0, :N_EXPERTS].astype(jnp.int32)
            padded = (cnt + tmb - 1) // tmb * tmb
            pend = jnp.cumsum(padded)
            pstart = pend - padded
            e_idx = route[:, 0:2].astype(jnp.int32)
            dest = pstart[e_idx] + route[:, 4:6].astype(jnp.int32)
            dest0, dest1 = dest[:, 0], dest[:, 1]
            block_start = jnp.arange(n_blocks, dtype=jnp.int32) * tmb
            block_e = jnp.minimum(jnp.sum((pend[None, :] <= block_start[:, None]).astype(jnp.int32), axis=1),
                                  N_EXPERTS - 1)
            n_used = (pend[-1] // tmb).astype(jnp.int32).reshape(1)
            n_slots = n_blocks * tmb
            jpad = jnp.arange(tmb, dtype=jnp.int32)[None, :]
            pad_slots = jnp.where(jpad < (padded - cnt)[:, None], (pstart + cnt)[:, None] + jpad,
                                  n_slots - 1).reshape(N_EXPERTS * tmb)

            xb = sc_dispatch(h, dest0, dest1, pad_slots, n_slots)
            yb = moe_ffn(xb, block_e, n_used, *moe_bf16)
            outs = []
            for row0, m in ((0, n_p), (n_p, n_s)):
                idx = jnp.concatenate([dest0[row0:row0 + m], dest1[row0:row0 + m]])
                outs.append(moe_finish(x3, route, sc_row_gather(yb, idx), norm_final, row0, tm=WIDE_TILE))
            out_p, out_s = outs

    return out_p.reshape(bp, tp, d), out_s.reshape(bs, ts_, d)
```
